```python
import jax, jax.numpy as jnp
from jax import lax
import numpy as np

D_MODEL = 1024
BATCH = 16
SEQ = 256
DEPTH = 1
DEC_BATCH = 8
DEC_SEQ = 1024
PAST_LEN = 256

GRID_W = 64
D_ATT = 512
N_HEADS_ATT = 8
HEAD_DIM = 64
WIN_H = 8
WIN_W = 16
BAND_W = 2 * WIN_W
ATT_QBLOCK = 128
ATT_SCALE = HEAD_DIM ** -0.5
NEG_INF = -1e30
D_RG = 512
N_RG_BLOCKS = 8
RG_BLOCK = D_RG // N_RG_BLOCKS
CONV_W = 4
CONV_PAD_L = 2
CONV_PAD_R = 1
RG_C = 8.0
D_IN = 3 * D_ATT + 2 * D_RG
D_MIX = D_ATT + D_RG
N_KEYS = 128
N_EXPERTS = N_KEYS * N_KEYS
PEER_HEADS = 8
PEER_D_KEY = 256
PEER_D_HALF = PEER_D_KEY // 2
PEER_TOPK = 16
PEER_CHUNK = 128
RMS_EPS = 1e-6

kernel_name = "hybrid_natten_rglru_peer_diffusion_step"


def rmsnorm(x, w):
    xf = x.astype(jnp.float32)
    y = xf * lax.rsqrt(jnp.mean(xf * xf, axis=-1, keepdims=True) + RMS_EPS)
    return (y * w.astype(jnp.float32)).astype(x.dtype)


def adaln_params(cvec, w_mod, b_mod):
    m = jax.nn.silu(cvec) @ w_mod + b_mod
    return m.reshape(cvec.shape[0], 6, D_MODEL)


def modulate(h, shift, scale):
    return h * (1 + scale[:, None, :]) + shift[:, None, :]


def mixing_inputs(x, mod, norm_w, w_in):
    B, T, _ = x.shape
    h = modulate(rmsnorm(x, norm_w), mod[:, 0], mod[:, 1])
    proj = h @ w_in
    q, k, v, xr, yr = jnp.split(proj, [D_ATT, 2 * D_ATT, 3 * D_ATT, 3 * D_ATT + D_RG], axis=-1)
    heads = lambda t: t.reshape(B, T, N_HEADS_ATT, HEAD_DIM)
    return heads(q), heads(k), heads(v), xr, yr


def context_attention(q, k, v):
    B, L, H, Dh = q.shape
    nb = L // ATT_QBLOCK
    q_blocks = jnp.moveaxis(q.reshape(B, nb, ATT_QBLOCK, H, Dh), 1, 0)

    def one_block(q_b):
        s = jnp.einsum('bqhd,bkhd->bhqk', q_b, k).astype(jnp.float32) * ATT_SCALE
        p = jax.nn.softmax(s, axis=-1).astype(v.dtype)
        return jnp.einsum('bhqk,bkhd->bqhd', p, v)

    out = lax.map(one_block, q_blocks)
    return jnp.moveaxis(out, 0, 1).reshape(B, L, H * Dh)


def neighbourhood_attention(q, k, v, k_ctx, v_ctx, rpb):
    B, T, H, Dh = q.shape
    rows = T // GRID_W
    kh = min(WIN_H, rows)
    ncb = GRID_W // WIN_W
    r = jnp.arange(rows)
    row_idx = jnp.clip(r - kh // 2, 0, rows - kh)[:, None] + jnp.arange(kh)[None, :]
    row_bias_idx = row_idx - r[:, None] + (WIN_H - 1)
    cb = jnp.arange(ncb) * WIN_W
    qcol = cb[:, None] + jnp.arange(WIN_W)[None, :]
    band_idx = jnp.clip(cb - WIN_W // 2, 0, GRID_W - BAND_W)[:, None] + jnp.arange(BAND_W)[None, :]
    col_start = jnp.clip(qcol - WIN_W // 2, 0, GRID_W - WIN_W)
    kcol = band_idx[:, None, :]
    col_valid = (kcol >= col_start[:, :, None]) & (kcol < col_start[:, :, None] + WIN_W)
    col_bias_idx = jnp.clip(kcol - qcol[:, :, None] + (WIN_W - 1), 0, 2 * WIN_W - 2)
    k_grid = k.reshape(B, rows, GRID_W, H, Dh)
    v_grid = v.reshape(B, rows, GRID_W, H, Dh)
    q_rows = jnp.moveaxis(q.reshape(B, rows, ncb, WIN_W, H, Dh), 1, 0)
    n_loc = kh * BAND_W

    def one_row(args):
        q_r, ridx, rbias = args
        gi_r = ridx[None, :, None]
        gi_c = band_idx[:, None, :]
        k_b = k_grid[:, gi_r, gi_c]
        v_b = v_grid[:, gi_r, gi_c]
        s_loc = jnp.einsum('bcqhd,bcjwhd->bhcqjw', q_r, k_b).astype(jnp.float32) * ATT_SCALE
        bias = rpb[:, rbias[None, None, :, None], col_bias_idx[:, :, None, :]]
        s_loc = jnp.where(col_valid[:, :, None, :], s_loc + bias.astype(jnp.float32), NEG_INF)
        s_ctx = jnp.einsum('bcqhd,blhd->bhcql', q_r, k_ctx).astype(jnp.float32) * ATT_SCALE
        s = jnp.concatenate([s_loc.reshape(B, H, ncb, WIN_W, n_loc), s_ctx], axis=-1)
        p = jax.nn.softmax(s, axis=-1).astype(v.dtype)
        p_loc = p[..., :n_loc].reshape(B, H, ncb, WIN_W, kh, BAND_W)
        return (jnp.einsum('bhcqjw,bcjwhd->bcqhd', p_loc, v_b)
                + jnp.einsum('bhcql,blhd->bcqhd', p[..., n_loc:], v_ctx))

    out = lax.map(one_row, (q_rows, row_idx, row_bias_idx))
    return jnp.moveaxis(out, 0, 1).reshape(B, T, H * Dh)


def centred_depthwise_conv(x, w, b):
    T = x.shape[1]
    xp = jnp.pad(x, ((0, 0), (CONV_PAD_L, CONV_PAD_R), (0, 0)))
    out = xp[:, 0:T] * w[0] + b
    for i in range(1, CONV_W):
        out = out + xp[:, i:i + T] * w[i]
    return out


def _lin_combine(left, right):
    a_l, b_l = left
    a_r, b_r = right
    return (a_l * a_r, a_r * b_l + b_r)


def rglru_direction(xc, w_a, b_a, w_i, b_i, lam, h0, reverse):
    B, T, _ = xc.shape
    xb = xc.reshape(B, T, N_RG_BLOCKS, RG_BLOCK)
    r = jax.nn.sigmoid(jnp.einsum('btnc,ncd->btnd', xb, w_a.astype(jnp.float32)).reshape(B, T, D_RG)
                       + b_a.astype(jnp.float32))
    i = jax.nn.sigmoid(jnp.einsum('btnc,ncd->btnd', xb, w_i.astype(jnp.float32)).reshape(B, T, D_RG)
                       + b_i.astype(jnp.float32))
    log_a = RG_C * r * jax.nn.log_sigmoid(lam.astype(jnp.float32))
    a = jnp.exp(log_a)
    b = jnp.sqrt(-jnp.expm1(2 * log_a)) * (i * xc)
    edge = -1 if reverse else 0
    b = b.at[:, edge].add(a[:, edge] * h0.astype(jnp.float32))
    _, h = lax.associative_scan(_lin_combine, (a, b), reverse=reverse, axis=1)
    return h


def bidir_rglru(xr, yr, h0, conv_w, conv_b, w_a, b_a, w_i, b_i, lam):
    xc = centred_depthwise_conv(xr, conv_w, conv_b).astype(jnp.float32)
    h_f = rglru_direction(xc, w_a[0], b_a[0], w_i[0], b_i[0], lam[0], h0[:, 0], False)
    h_b = rglru_direction(xc, w_a[1], b_a[1], w_i[1], b_i[1], lam[1], h0[:, 1], True)
    y = ((h_f + h_b) * jax.nn.gelu(yr.astype(jnp.float32), approximate=True)).astype(xr.dtype)
    return y, h_f, h_b


def mixer_residual(x, mod, att, rg_out, w_out):
    out = jnp.concatenate([att, rg_out], axis=-1) @ w_out
    return x + mod[:, 2][:, None, :] * out


def peer_ffn(x, w_q, sub_keys, u, v):
    N = x.shape[0]
    q = (x @ w_q).reshape(N, PEER_HEADS, 2, PEER_D_HALF)
    s = jnp.einsum('nhpd,pkd->nhpk', q, sub_keys).astype(jnp.float32)
    s_top, i_top = lax.top_k(s, PEER_TOPK)
    cand = (s_top[:, :, 0, :, None] + s_top[:, :, 1, None, :]).reshape(N, PEER_HEADS, PEER_TOPK * PEER_TOPK)
    best, best_pos = lax.top_k(cand, PEER_TOPK)
    idx1 = jnp.take_along_axis(i_top[:, :, 0], best_pos // PEER_TOPK, axis=-1)
    idx2 = jnp.take_along_axis(i_top[:, :, 1], best_pos % PEER_TOPK, axis=-1)
    expert = (idx1 * N_KEYS + idx2).reshape(N, PEER_HEADS * PEER_TOPK)
    g = jax.nn.softmax(best, axis=-1).astype(x.dtype).reshape(N, PEER_HEADS * PEER_TOPK)
    nc = N // PEER_CHUNK

    def one_chunk(args):
        x_c, e_c, g_c = args
        act = jax.nn.gelu(jnp.einsum('cd,ced->ce', x_c, u[e_c]), approximate=False)
        return jnp.einsum('ce,ced->cd', g_c * act, v[e_c])

    out = lax.map(one_chunk, (x.reshape(nc, PEER_CHUNK, D_MODEL),
                              expert.reshape(nc, PEER_CHUNK, -1), g.reshape(nc, PEER_CHUNK, -1)))
    return out.reshape(N, D_MODEL)


def ffn_sublayer(x, mod, norm_w, w_q, sub_keys, u, v):
    B, T, D = x.shape
    h = modulate(rmsnorm(x, norm_w), mod[:, 3], mod[:, 4])
    out = peer_ffn(h.reshape(B * T, D), w_q, sub_keys, u, v).reshape(B, T, D)
    return x + mod[:, 5][:, None, :] * out


def setup_inputs(seed: int = 0) -> dict:
    key = jax.random.key(seed)
    ks = jax.random.split(key, 26)
    nrm = lambda k, shape, s: jax.random.normal(k, shape, jnp.float32) * s
    lam_u = jax.random.uniform(ks[18], (DEPTH, 2, D_RG), jnp.float32, minval=0.9, maxval=0.999)
    return {
        "x_prompt": nrm(ks[0], (BATCH, SEQ, D_MODEL), 1.0),
        "x_sample": nrm(ks[1], (DEC_BATCH, DEC_SEQ, D_MODEL), 1.0),
        "c": nrm(ks[2], (DEC_BATCH, D_MODEL), 1.0),
        "cache_k": nrm(ks[3], (DEC_BATCH, DEPTH, PAST_LEN, N_HEADS_ATT, HEAD_DIM), 1.0),
        "cache_v": nrm(ks[4], (DEC_BATCH, DEPTH, PAST_LEN, N_HEADS_ATT, HEAD_DIM), 1.0),
        "state_rglru": nrm(ks[5], (DEC_BATCH, DEPTH, 2, D_RG), 0.5),
        "c_ctx": nrm(ks[6], (D_MODEL,), 1.0),
        "w_mod": nrm(ks[7], (DEPTH, D_MODEL, 6 * D_MODEL), 0.5 * D_MODEL ** -0.5),
        "b_mod": nrm(ks[8], (DEPTH, 6 * D_MODEL), 0.02),
        "norm_mix_w": 1.0 + nrm(ks[9], (DEPTH, D_MODEL), 0.02),
        "w_in": nrm(ks[10], (DEPTH, D_MODEL, D_IN), D_MODEL ** -0.5),
        "rpb": nrm(ks[11], (DEPTH, N_HEADS_ATT, 2 * WIN_H - 1, 2 * WIN_W - 1), 0.1),
        "conv_w": nrm(ks[12], (DEPTH, CONV_W, D_RG), CONV_W ** -0.5),
        "conv_b": nrm(ks[13], (DEPTH, D_RG), 0.02),
        "rg_w_a": nrm(ks[14], (DEPTH, 2, N_RG_BLOCKS, RG_BLOCK, RG_BLOCK), RG_BLOCK ** -0.5),
        "rg_b_a": nrm(ks[15], (DEPTH, 2, D_RG), 0.02),
        "rg_w_i": nrm(ks[16], (DEPTH, 2, N_RG_BLOCKS, RG_BLOCK, RG_BLOCK), RG_BLOCK ** -0.5),
        "rg_b_i": nrm(ks[17], (DEPTH, 2, D_RG), 0.02),
        "rg_lambda": jnp.log(lam_u) - jnp.log1p(-lam_u),
        "w_out": nrm(ks[19], (DEPTH, D_MIX, D_MODEL), D_MIX ** -0.5),
        "norm_ffn_w": 1.0 + nrm(ks[20], (DEPTH, D_MODEL), 0.02),
        "peer_w_q": nrm(ks[21], (DEPTH, D_MODEL, PEER_HEADS * PEER_D_KEY), D_MODEL ** -0.5),
        "peer_sub_keys": nrm(ks[22], (DEPTH, 2, N_KEYS, PEER_D_HALF), PEER_D_HALF ** -0.5),
        "peer_u": nrm(ks[23], (DEPTH, N_EXPERTS, D_MODEL), D_MODEL ** -0.5),
        "peer_v": nrm(ks[24], (DEPTH, N_EXPERTS, D_MODEL), 0.5),
        "norm_f_w": 1.0 + nrm(ks[25], (D_MODEL,), 0.02),
    }


def reference(x_prompt, x_sample, c, cache_k, cache_v, state_rglru, c_ctx, w_mod, b_mod, norm_mix_w, w_in,
              rpb, conv_w, conv_b, rg_w_a, rg_b_a, rg_w_i, rg_b_i, rg_lambda, w_out, norm_ffn_w,
              peer_w_q, peer_sub_keys, peer_u, peer_v, norm_f_w):
    xp = x_prompt
    xs = x_sample
    new_k, new_v, new_h = [], [], []
    for l in range(DEPTH):
        rg = (conv_w[l], conv_b[l], rg_w_a[l], rg_b_a[l], rg_w_i[l], rg_b_i[l], rg_lambda[l])
        peer = (peer_w_q[l], peer_sub_keys[l], peer_u[l], peer_v[l])
        mod_p = adaln_params(c_ctx[None, :], w_mod[l], b_mod[l])
        q, k, v, xr, yr = mixing_inputs(xp, mod_p, norm_mix_w[l], w_in[l])
        att = context_attention(q, k, v)
        h0 = jnp.zeros((xp.shape[0], 2, D_RG), jnp.float32)
        rg_out, h_f, h_b = bidir_rglru(xr, yr, h0, *rg)
        xp = mixer_residual(xp, mod_p, att, rg_out, w_out[l])
        xp = ffn_sublayer(xp, mod_p, norm_ffn_w[l], *peer)
        new_k.append(k)
        new_v.append(v)
        new_h.append(jnp.stack([h_f[:, -1], h_b[:, 0]], axis=1).astype(x_prompt.dtype))
        mod_s = adaln_params(c, w_mod[l], b_mod[l])
        q, k, v, xr, yr = mixing_inputs(xs, mod_s, norm_mix_w[l], w_in[l])
        att = neighbourhood_attention(q, k, v, cache_k[:, l], cache_v[:, l], rpb[l])
        rg_out, _, _ = bidir_rglru(xr, yr, state_rglru[:, l], *rg)
        xs = mixer_residual(xs, mod_s, att, rg_out, w_out[l])
        xs = ffn_sublayer(xs, mod_s, norm_ffn_w[l], *peer)
    y_prompt = rmsnorm(xp, norm_f_w)
    y_sample = rmsnorm(xs, norm_f_w)
    new_cache_k = jnp.stack(new_k, axis=1)
    new_cache_v = jnp.stack(new_v, axis=1)
    new_state_rglru = jnp.stack(new_h, axis=1)
    return (y_prompt, y_sample, new_cache_k, new_cache_v, new_state_rglru)
```

```python
import functools

import jax
import jax.numpy as jnp
from jax import lax
from jax.experimental import pallas as pl
from jax.experimental.pallas import tpu as pltpu

F32 = jnp.float32
BF16 = jnp.bfloat16

D = 1024
D_ATT = 512
N_HEADS = 8
HEAD_DIM = 64
GRID_W = 64
WIN_H = 8
WIN_W = 16
D_RG = 512
D_IN = 3 * D_ATT + 2 * D_RG
N_KEYS = 128
P_HEADS = 8
TOPK = 16
N_EXPERTS = N_KEYS * N_KEYS
RMS_EPS = 1e-6
NEG_INF = -1e30
RG_C = 8.0
ATT_SCALE = HEAD_DIM ** -0.5

VMEM_LIMIT = 56 * 1024 * 1024

_NT = (((1,), (1,)), ((), ()))


def _cparams(*sem):
    return pltpu.CompilerParams(dimension_semantics=sem, vmem_limit_bytes=VMEM_LIMIT)


def _rmsnorm(x, w):
    ms = jnp.mean(x * x, axis=-1, keepdims=True)
    return x * lax.rsqrt(ms + RMS_EPS) * w


def _mod_kernel(c_ref, w_ref, b_ref, o_ref):
    cv = c_ref[...]
    s = cv * jax.nn.sigmoid(cv)
    o_ref[...] = jnp.dot(s.astype(BF16), w_ref[...].astype(BF16),
                         preferred_element_type=F32) + b_ref[...]


def _mod_call(cvec, w_mod, b_mod):
    rows = cvec.shape[0]
    tn = 1536
    return pl.pallas_call(
        _mod_kernel,
        grid=(6 * D // tn,),
        in_specs=[pl.BlockSpec((rows, D), lambda j: (0, 0)),
                  pl.BlockSpec((D, tn), lambda j: (0, j)),
                  pl.BlockSpec((1, tn), lambda j: (0, j))],
        out_specs=pl.BlockSpec((rows, tn), lambda j: (0, j)),
        out_shape=jax.ShapeDtypeStruct((rows, 6 * D), F32),
        compiler_params=_cparams("arbitrary"),
        name="mod",
    )(cvec, w_mod, b_mod)


def _inproj_kernel(x_ref, mod_ref, nw_ref, w_ref, q_ref, kb_ref, vb_ref, kf_ref, vf_ref, xr_ref, yr_ref):
    x = x_ref[...]
    h = _rmsnorm(x, nw_ref[...]) * (1.0 + mod_ref[0, 1:2, :]) + mod_ref[0, 0:1, :]
    p = jnp.dot(h.astype(BF16), w_ref[...], preferred_element_type=F32)
    q_ref[...] = (p[:, 0:D_ATT] * ATT_SCALE).astype(BF16)
    k = p[:, D_ATT:2 * D_ATT]
    v = p[:, 2 * D_ATT:3 * D_ATT]
    kb_ref[...] = k.astype(BF16)
    vb_ref[...] = v.astype(BF16)
    kf_ref[...] = k
    vf_ref[...] = v
    xr_ref[...] = p[:, 3 * D_ATT:3 * D_ATT + D_RG]
    yr_ref[...] = p[:, 3 * D_ATT + D_RG:]


def _inproj_call(x2d, mod3, mod_base, tiles_per_batch, norm_w, w_in_bf):
    n = x2d.shape[0]
    tm = 512
    row = lambda t: (mod_base + t // tiles_per_batch, 0, 0)
    tok = lambda t: (t, 0)
    shp = lambda w, dt: jax.ShapeDtypeStruct((n, w), dt)
    return pl.pallas_call(
        _inproj_kernel,
        grid=(n // tm,),
        in_specs=[pl.BlockSpec((tm, D), tok),
                  pl.BlockSpec((1, 6, D), row),
                  pl.BlockSpec((1, D), lambda t: (0, 0)),
                  pl.BlockSpec((D, D_IN), lambda t: (0, 0))],
        out_specs=[pl.BlockSpec((tm, D_ATT), tok)] * 5 + [pl.BlockSpec((tm, D_RG), tok)] * 2,
        out_shape=[shp(D_ATT, BF16), shp(D_ATT, BF16), shp(D_ATT, BF16), shp(D_ATT, F32), shp(D_ATT, F32),
                   shp(D_RG, F32), shp(D_RG, F32)],
        compiler_params=_cparams("arbitrary"),
        name="inproj",
    )(x2d, mod3, norm_w, w_in_bf)


def _ctx_attn_kernel(q_ref, k_ref, v_ref, o_ref):
    for h in range(N_HEADS):
        sl = slice(h * HEAD_DIM, (h + 1) * HEAD_DIM)
        s = lax.dot_general(q_ref[:, sl], k_ref[:, sl], _NT, preferred_element_type=F32)
        m = jnp.max(s, axis=-1, keepdims=True)
        p = jnp.exp(s - m)
        l = jnp.sum(p, axis=-1, keepdims=True)
        o = jnp.dot(p.astype(BF16), v_ref[:, sl], preferred_element_type=F32)
        o_ref[:, sl] = o / l


def _ctx_attn_call(q, kb, vb, seq):
    n = q.shape[0]
    blk = pl.BlockSpec((seq, D_ATT), lambda b: (b, 0))
    return pl.pallas_call(
        _ctx_attn_kernel,
        grid=(n // seq,),
        in_specs=[blk, blk, blk],
        out_specs=blk,
        out_shape=jax.ShapeDtypeStruct((n, D_ATT), F32),
        compiler_params=_cparams("arbitrary"),
        name="ctx_attn",
    )(q, kb, vb)


def _nbr_attn_kernel(q_ref, k_ref, v_ref, ck_ref, cv_ref, bias_ref, o_ref, *, rows):
    r = pl.program_id(1)
    start = jnp.clip(r - WIN_H // 2, 0, rows - WIN_H)
    row0 = pl.multiple_of(start * GRID_W, GRID_W)
    n_loc = WIN_H * GRID_W
    qc = lax.broadcasted_iota(jnp.int32, (GRID_W, n_loc), 0)
    kc = lax.broadcasted_iota(jnp.int32, (GRID_W, n_loc), 1) & (GRID_W - 1)
    cs = jnp.clip(qc - WIN_W // 2, 0, GRID_W - WIN_W)
    valid = (kc >= cs) & (kc < cs + WIN_W)
    for h in range(N_HEADS):
        sl = slice(h * HEAD_DIM, (h + 1) * HEAD_DIM)
        q = q_ref[:, sl]
        kl = k_ref[pl.ds(row0, n_loc), sl]
        vl = v_ref[pl.ds(row0, n_loc), sl]
        s_loc = lax.dot_general(q, kl, _NT, preferred_element_type=F32) + bias_ref[h, 0]
        s_loc = jnp.where(valid, s_loc, NEG_INF)
        s_ctx = lax.dot_general(q, ck_ref[:, sl].astype(BF16), _NT, preferred_element_type=F32)
        m = jnp.maximum(jnp.max(s_loc, axis=-1, keepdims=True), jnp.max(s_ctx, axis=-1, keepdims=True))
        p_loc = jnp.exp(s_loc - m)
        p_ctx = jnp.exp(s_ctx - m)
        l = jnp.sum(p_loc, axis=-1, keepdims=True) + jnp.sum(p_ctx, axis=-1, keepdims=True)
        o = (jnp.dot(p_loc.astype(BF16), vl, preferred_element_type=F32)
             + jnp.dot(p_ctx.astype(BF16), cv_ref[:, sl].astype(BF16), preferred_element_type=F32))
        o_ref[:, sl] = o / l


def _bias_pattern(r, rows):
    lo = WIN_H // 2
    hi = rows - WIN_H + lo
    return jnp.where(r < lo, r, jnp.where(r <= hi, lo, r - (hi - lo)))


def _nbr_attn_call(q, kb, vb, cache_k2d, cache_v2d, bias_tab, batch, seq, past):
    rows = seq // GRID_W
    n = q.shape[0]
    return pl.pallas_call(
        functools.partial(_nbr_attn_kernel, rows=rows),
        grid=(batch, rows),
        in_specs=[pl.BlockSpec((GRID_W, D_ATT), lambda b, r: (b * rows + r, 0)),
                  pl.BlockSpec((seq, D_ATT), lambda b, r: (b, 0)),
                  pl.BlockSpec((seq, D_ATT), lambda b, r: (b, 0)),
                  pl.BlockSpec((past, D_ATT), lambda b, r: (b, 0)),
                  pl.BlockSpec((past, D_ATT), lambda b, r: (b, 0)),
                  pl.BlockSpec((N_HEADS, 1, GRID_W, WIN_H * GRID_W),
                               lambda b, r: (0, _bias_pattern(r, rows), 0, 0))],
        out_specs=pl.BlockSpec((GRID_W, D_ATT), lambda b, r: (b * rows + r, 0)),
        out_shape=jax.ShapeDtypeStruct((n, D_ATT), F32),
        compiler_params=_cparams("arbitrary", "arbitrary"),
        name="nbr_attn",
    )(q, kb, vb, cache_k2d, cache_v2d, bias_tab)


def _bias_table(rpb_l, rows):
    lo = WIN_H // 2
    hi = rows - WIN_H + lo
    rep_rows = list(range(lo)) + [lo] + list(range(hi + 1, rows))
    r = jnp.asarray(rep_rows, jnp.int32)
    start = jnp.clip(r - lo, 0, rows - WIN_H)
    rb = start[:, None] + jnp.arange(WIN_H)[None, :] - r[:, None] + (WIN_H - 1)
    qc = jnp.arange(GRID_W)
    cbi = jnp.clip(qc[None, :] - qc[:, None] + (WIN_W - 1), 0, 2 * WIN_W - 2)
    tab = rpb_l[:, rb[:, None, :, None], cbi[None, :, None, :]]
    return tab.reshape(N_HEADS, len(rep_rows), GRID_W, WIN_H * GRID_W).astype(F32)


def _log_sigmoid(x):
    return jnp.minimum(x, 0.0) - jnp.log(1.0 + jnp.exp(-jnp.abs(x)))


def _gelu_tanh(x):
    return 0.5 * x * (1.0 + jnp.tanh(0.7978845608028654 * (x + 0.044715 * x * x * x)))


def _rglru_kernel(xr_ref, yr_ref, h0_ref, cw_ref, cb_ref, wbd_ref, ba_ref, bi_ref, lam_ref,
                  y_ref, hl_ref, xpad, a_f, b_f, a_b, b_b):
    T = xr_ref.shape[0]
    CH = 256
    xpad[0:8, :] = jnp.zeros((8, D_RG), F32)
    xpad[T + 8:T + 16, :] = jnp.zeros((8, D_RG), F32)
    xpad[8:T + 8, :] = xr_ref[...]
    a_refs = (a_f, a_b)
    b_refs = (b_f, b_b)
    for c0 in range(0, T, CH):
        xc = cb_ref[...] + cw_ref[0:1, :] * xpad[c0 + 6:c0 + 6 + CH, :]
        for i in range(1, 4):
            xc = xc + cw_ref[i:i + 1, :] * xpad[c0 + 6 + i:c0 + 6 + i + CH, :]
        xcb = xc.astype(BF16)
        for d in range(2):
            gates = []
            for g in range(2):
                halves = [jnp.dot(xcb[:, hf * 256:(hf + 1) * 256], wbd_ref[d, g, hf],
                                  preferred_element_type=F32) for hf in range(2)]
                gates.append(jnp.concatenate(halves, axis=-1))
            rg = jax.nn.sigmoid(gates[0] + ba_ref[d:d + 1, :])
            ig = jax.nn.sigmoid(gates[1] + bi_ref[d:d + 1, :])
            log_a = RG_C * rg * _log_sigmoid(lam_ref[d:d + 1, :])
            a = jnp.exp(log_a)
            b = jnp.sqrt(1.0 - a * a) * (ig * xc)
            a_refs[d][c0:c0 + CH, :] = a
            b_refs[d][c0:c0 + CH, :] = b

    sub = lax.broadcasted_iota(jnp.int32, (8, D_RG), 0)
    n_tiles = T // 8

    def scan_tile(a_ref, b_ref, t8, h, order):
        base = pl.multiple_of(t8 * 8, 8)
        a_blk = a_ref[pl.ds(base, 8), :]
        b_blk = b_ref[pl.ds(base, 8), :]
        out = b_blk
        for j in order:
            cand = a_blk * h + b_blk
            h = jnp.broadcast_to(cand[j:j + 1, :], (8, D_RG))
            out = jnp.where(sub == j, cand, out)
        b_ref[pl.ds(base, 8), :] = out
        return h

    def body(i, carry):
        hf, hb = carry
        hf = scan_tile(a_f, b_f, i, hf, range(8))
        hb = scan_tile(a_b, b_b, n_tiles - 1 - i, hb, range(7, -1, -1))
        return hf, hb

    hf0 = jnp.broadcast_to(h0_ref[0, 0:1, :], (8, D_RG))
    hb0 = jnp.broadcast_to(h0_ref[0, 1:2, :], (8, D_RG))
    hf, hb = lax.fori_loop(0, n_tiles, body, (hf0, hb0))
    hl_ref[0, 0:1, :] = hf[0:1, :]
    hl_ref[0, 1:2, :] = hb[0:1, :]
    for c0 in range(0, T, CH):
        sl = slice(c0, c0 + CH)
        y_ref[sl, :] = (b_f[sl, :] + b_b[sl, :]) * _gelu_tanh(yr_ref[sl, :])


def _rglru_call(xr, yr, h0, conv_w, conv_b, wbd, b_a, b_i, lam, seq):
    n = xr.shape[0]
    batch = n // seq
    tok = pl.BlockSpec((seq, D_RG), lambda b: (b, 0))
    full = lambda shape: pl.BlockSpec(shape, lambda b: (0,) * len(shape))
    return pl.pallas_call(
        _rglru_kernel,
        grid=(batch,),
        in_specs=[tok, tok, pl.BlockSpec((1, 2, D_RG), lambda b: (b, 0, 0)),
                  full((4, D_RG)), full((1, D_RG)), full((2, 2, 2, 256, 256)),
                  full((2, D_RG)), full((2, D_RG)), full((2, D_RG))],
        out_specs=[tok, pl.BlockSpec((1, 2, D_RG), lambda b: (b, 0, 0))],
        out_shape=[jax.ShapeDtypeStruct((n, D_RG), F32), jax.ShapeDtypeStruct((batch, 2, D_RG), F32)],
        scratch_shapes=[pltpu.VMEM((seq + 16, D_RG), F32)] + [pltpu.VMEM((seq, D_RG), F32)] * 4,
        compiler_params=_cparams("arbitrary"),
        name="rglru",
    )(xr, yr, h0, conv_w, conv_b, wbd, b_a, b_i, lam)


def _block_diag_halves(w):
    d = w.shape[0]
    w4 = w.reshape(d, 2, 4, 64, 64)
    eye = jnp.eye(4, dtype=w.dtype)
    out = jnp.einsum('dhncf,nm->dhncmf', w4, eye)
    return out.reshape(d, 2, 256, 256)


def _top16(s, iota_f):
    tm = s.shape[1]
    row16 = lax.broadcasted_iota(jnp.int32, (TOPK, tm), 0).astype(F32)

    def body(r, carry):
        w, rank, tvals = carry
        rf = r.astype(F32)
        m = jnp.max(w, axis=0, keepdims=True)
        idx = jnp.min(jnp.where(w == m, iota_f, float(N_KEYS)), axis=0, keepdims=True)
        sel = iota_f == idx
        rank = jnp.where(sel, rf, rank)
        w = jnp.where(sel, -jnp.inf, w)
        tvals = jnp.where(row16 == rf, m, tvals)
        return w, rank, tvals

    init = (s, jnp.full(s.shape, float(TOPK), F32), jnp.zeros((TOPK, tm), F32))
    _, rank, tvals = lax.fori_loop(0, TOPK, body, init)
    return rank, tvals


def _route_head(s1, s2, iota_f):
    tm = s1.shape[1]
    rank1, t1 = _top16(s1, iota_f)
    rank2, t2 = _top16(s2, iota_f)
    sub = lax.broadcasted_iota(jnp.int32, (8, tm), 0)
    subf = sub.astype(F32)
    ninf = -jnp.inf
    lim_a = (8, 8, 5, 4)
    vals, flat = [], []
    for r1 in range(4):
        vals.append(jnp.where(sub < lim_a[r1], t1[r1:r1 + 1, :] + t2[0:8, :], ninf))
        flat.append(subf + float(r1 * TOPK))
    vals.append(t1[0:1, :] + t2[8:16, :])
    flat.append(subf + 8.0)
    for r2 in range(3):
        ok = (sub >= 4) if r2 < 2 else (sub == 4)
        vals.append(jnp.where(ok, t1[0:8, :] + t2[r2:r2 + 1, :], ninf))
        flat.append(subf * float(TOPK) + float(r2))
    vals.append(t1[8:16, :] + t2[0:1, :])
    flat.append((subf + 8.0) * float(TOPK))
    cand = jnp.concatenate(vals, axis=0)
    flat = jnp.concatenate(flat, axis=0)

    def body(r, carry):
        w, sel, z, best0 = carry
        m = jnp.max(w, axis=0, keepdims=True)
        f = jnp.min(jnp.where(w == m, flat, 1e9), axis=0, keepdims=True)
        hit = flat == f
        sel = jnp.where(hit, 1.0, sel)
        w = jnp.where(hit, ninf, w)
        best0 = jnp.where(r == 0, m, best0)
        z = z + jnp.exp(m - best0)
        return w, sel, z, best0

    init = (cand, jnp.zeros(cand.shape, F32), jnp.zeros((1, tm), F32), jnp.zeros((1, tm), F32))
    _, sel, z, _ = lax.fori_loop(0, TOPK, body, init)

    l_rows = []
    for r1 in range(4):
        l = jnp.sum(sel[8 * r1:8 * r1 + 8, :], axis=0, keepdims=True)
        if r1 == 0:
            l = l + jnp.sum(sel[32:40, :], axis=0, keepdims=True)
        l_rows.append(l)
    l_mid = sel[40:48, :] + sel[48:56, :] + sel[56:64, :]
    l_hi = sel[64:72, :]
    for r1 in range(4, 8):
        l_rows.append(l_mid[r1:r1 + 1, :])
    for r1 in range(8, 16):
        l_rows.append(l_hi[r1 - 8:r1 - 7, :])
    a1 = jnp.zeros(s1.shape, F32)
    for r1 in range(TOPK):
        a1 = jnp.where(rank1 == float(r1), l_rows[r1], a1)
    c = jnp.exp(s1 - t1[0:1, :]) / z
    e2 = jnp.exp(s2 - t2[0:1, :])
    return a1, c, rank2, e2


def _mixffn_kernel(x_ref, att_ref, rg_ref, mod_ref, wo_ref, nw_ref, wqt_ref, sk_ref,
                   x1_ref, h2t_ref, a1_ref, c_ref, b2_ref, e2_ref, qt_scr):
    o = (jnp.dot(att_ref[...].astype(BF16), wo_ref[0:D_ATT, :], preferred_element_type=F32)
         + jnp.dot(rg_ref[...].astype(BF16), wo_ref[D_ATT:, :], preferred_element_type=F32))
    x1 = x_ref[...] + mod_ref[0, 2:3, :] * o
    x1_ref[...] = x1
    h2 = _rmsnorm(x1, nw_ref[...]) * (1.0 + mod_ref[0, 4:5, :]) + mod_ref[0, 3:4, :]
    h2t = h2.T.astype(BF16)
    h2t_ref[...] = h2t
    qt_scr[...] = jnp.dot(wqt_ref[...], h2t, preferred_element_type=F32).astype(BF16)
    tm = x_ref.shape[0]
    iota_f = lax.broadcasted_iota(jnp.int32, (N_KEYS, tm), 0).astype(F32)

    def head(h, carry):
        r0 = pl.multiple_of(h * 2 * N_KEYS, 2 * N_KEYS)
        s1 = jnp.dot(sk_ref[0], qt_scr[pl.ds(r0, N_KEYS), :], preferred_element_type=F32)
        s2 = jnp.dot(sk_ref[1], qt_scr[pl.ds(r0 + N_KEYS, N_KEYS), :], preferred_element_type=F32)
        a1, c, b2, e2 = _route_head(s1, s2, iota_f)
        a1_ref[h] = a1
        c_ref[h] = c
        b2_ref[h] = b2
        e2_ref[h] = e2
        return carry

    lax.fori_loop(0, P_HEADS, head, 0)


def _mixffn_call(x2d, att, rg, mod3, mod_base, tiles_per_batch, w_out_bf, norm_w, w_qt_bf, sk_bf, tm):
    n = x2d.shape[0]
    tok = lambda t: (t, 0)
    route = pl.BlockSpec((P_HEADS, N_KEYS, tm), lambda t: (0, 0, t))
    route_shape = jax.ShapeDtypeStruct((P_HEADS, N_KEYS, n), F32)
    return pl.pallas_call(
        _mixffn_kernel,
        grid=(n // tm,),
        in_specs=[pl.BlockSpec((tm, D), tok),
                  pl.BlockSpec((tm, D_ATT), tok),
                  pl.BlockSpec((tm, D_RG), tok),
                  pl.BlockSpec((1, 6, D), lambda t: (mod_base + t // tiles_per_batch, 0, 0)),
                  pl.BlockSpec((D, D), lambda t: (0, 0)),
                  pl.BlockSpec((1, D), lambda t: (0, 0)),
                  pl.BlockSpec((2 * P_HEADS * N_KEYS, D), lambda t: (0, 0)),
                  pl.BlockSpec((2, N_KEYS, N_KEYS), lambda t: (0, 0, 0))],
        out_specs=[pl.BlockSpec((tm, D), tok),
                   pl.BlockSpec((D, tm), lambda t: (0, t)),
                   route, route, route, route],
        out_shape=[jax.ShapeDtypeStruct((n, D), F32), jax.ShapeDtypeStruct((D, n), BF16),
                   route_shape, route_shape, route_shape, route_shape],
        scratch_shapes=[pltpu.VMEM((2 * P_HEADS * N_KEYS, tm), BF16)],
        compiler_params=_cparams("arbitrary"),
        name="mixffn",
    )(x2d, att, rg, mod3, w_out_bf, norm_w, w_qt_bf, sk_bf)


def _expert_prep_kernel(u_ref, v_ref, ub_ref, vt_ref):
    ub_ref[...] = u_ref[...].astype(BF16)
    vt_ref[...] = v_ref[...].T.astype(BF16)


def _expert_prep_call(u, v):
    te = 512
    return pl.pallas_call(
        _expert_prep_kernel,
        grid=(N_EXPERTS // te,),
        in_specs=[pl.BlockSpec((te, D), lambda j: (j, 0)), pl.BlockSpec((te, D), lambda j: (j, 0))],
        out_specs=[pl.BlockSpec((te, D), lambda j: (j, 0)), pl.BlockSpec((D, te), lambda j: (0, j))],
        out_shape=[jax.ShapeDtypeStruct((N_EXPERTS, D), BF16), jax.ShapeDtypeStruct((D, N_EXPERTS), BF16)],
        compiler_params=_cparams("arbitrary"),
        name="expert_prep",
    )(u, v)


def _peer_kernel(h2t_ref, u_ref, vt_ref, a1_ref, c_ref, b2_ref, e2_ref, x1_ref, mod_ref, nw_ref,
                 y_ref, acc, at_scr, *, i1_per_step):
    j = pl.program_id(1)

    @pl.when(j == 0)
    def _():
        acc[...] = jnp.zeros_like(acc)

    st = jnp.dot(u_ref[...], h2t_ref[...], preferred_element_type=F32)
    for i in range(i1_per_step):
        g = None
        for h in range(P_HEADS):
            w = jnp.where(b2_ref[h] < a1_ref[h, i:i + 1, :], c_ref[h, i:i + 1, :] * e2_ref[h], 0.0)
            g = w if g is None else g + w
        s = st[i * N_KEYS:(i + 1) * N_KEYS, :]
        act = 0.5 * s * (1.0 + lax.erf(s * 0.7071067811865476))
        at_scr[i * N_KEYS:(i + 1) * N_KEYS, :] = (act * g).astype(BF16)
    acc[...] += jnp.dot(vt_ref[...], at_scr[...], preferred_element_type=F32)

    @pl.when(j == pl.num_programs(1) - 1)
    def _():
        x2 = x1_ref[...] + mod_ref[0, 5:6, :] * acc[...].T
        y_ref[...] = _rmsnorm(x2, nw_ref[...])


def _peer_call(h2t, u_bf, vt_bf, a1, c, b2, e2, x1, mod3, mod_base, tiles_per_batch, norm_f_w, tm, te):
    n = x1.shape[0]
    ips = te // N_KEYS
    route_lo = pl.BlockSpec((P_HEADS, ips, tm), lambda t, j: (0, j, t))
    route_full = pl.BlockSpec((P_HEADS, N_KEYS, tm), lambda t, j: (0, 0, t))
    return pl.pallas_call(
        functools.partial(_peer_kernel, i1_per_step=ips),
        grid=(n // tm, N_EXPERTS // te),
        in_specs=[pl.BlockSpec((D, tm), lambda t, j: (0, t)),
                  pl.BlockSpec((te, D), lambda t, j: (j, 0)),
                  pl.BlockSpec((D, te), lambda t, j: (0, j)),
                  route_lo, route_lo, route_full, route_full,
                  pl.BlockSpec((tm, D), lambda t, j: (t, 0)),
                  pl.BlockSpec((1, 6, D), lambda t, j: (mod_base + t // tiles_per_batch, 0, 0)),
                  pl.BlockSpec((1, D), lambda t, j: (0, 0))],
        out_specs=pl.BlockSpec((tm, D), lambda t, j: (t, 0)),
        out_shape=jax.ShapeDtypeStruct((n, D), F32),
        scratch_shapes=[pltpu.VMEM((D, tm), F32), pltpu.VMEM((te, tm), BF16)],
        compiler_params=_cparams("arbitrary", "arbitrary"),
        name="peer",
    )(h2t, u_bf, vt_bf, a1, c, b2, e2, x1, mod3, norm_f_w)


def _path(x, mod3, mod_base, per_batch_mod, weights, attn_fn, h0):
    (norm_mix_w, w_in_bf, rg_params, w_out_bf, norm_ffn_w, w_qt_bf, sk_bf, u_bf, vt_bf, norm_f_w) = weights
    bsz, seq, _ = x.shape
    n = bsz * seq
    x2d = x.reshape(n, D)
    tpb = lambda tm: (seq // tm) if per_batch_mod else n
    q, kb, vb, kf, vf, xr, yr = _inproj_call(x2d, mod3, mod_base, tpb(512), norm_mix_w, w_in_bf)
    att = attn_fn(q, kb, vb)
    rg_out, h_last = _rglru_call(xr, yr, h0, *rg_params, seq)
    tm_mix = 256
    x1, h2t, a1, c, b2, e2 = _mixffn_call(x2d, att, rg_out, mod3, mod_base, tpb(tm_mix), w_out_bf,
                                          norm_ffn_w, w_qt_bf, sk_bf, tm_mix)
    tm_peer = 512
    y = _peer_call(h2t, u_bf, vt_bf, a1, c, b2, e2, x1, mod3, mod_base, tpb(tm_peer), norm_f_w,
                   tm_peer, 1024)
    return y.reshape(bsz, seq, D), kf, vf, h_last


def kernel(x_prompt, x_sample, c, cache_k, cache_v, state_rglru, c_ctx, w_mod, b_mod, norm_mix_w, w_in,
           rpb, conv_w, conv_b, rg_w_a, rg_b_a, rg_w_i, rg_b_i, rg_lambda, w_out, norm_ffn_w,
           peer_w_q, peer_sub_keys, peer_u, peer_v, norm_f_w):
    depth = w_mod.shape[0]
    assert depth == 1, "single-layer problem"
    l = 0
    bp, sp, _ = x_prompt.shape
    bs, ss, _ = x_sample.shape
    past = cache_k.shape[2]

    n_rows = 16
    cvec = jnp.concatenate([c_ctx[None, :], c, jnp.zeros((n_rows - 1 - bs, D), F32)], axis=0)
    mod3 = _mod_call(cvec, w_mod[l], b_mod[l][None, :]).reshape(n_rows, 6, D)

    wbd = jnp.stack([_block_diag_halves(rg_w_a[l]), _block_diag_halves(rg_w_i[l])], axis=1).astype(BF16)
    rg_params = (conv_w[l], conv_b[l][None, :], wbd, rg_b_a[l], rg_b_i[l], rg_lambda[l])
    u_bf, vt_bf = _expert_prep_call(peer_u[l], peer_v[l])
    weights = (norm_mix_w[l][None, :], w_in[l].astype(BF16), rg_params, w_out[l].astype(BF16),
               norm_ffn_w[l][None, :], peer_w_q[l].T.astype(BF16), peer_sub_keys[l].astype(BF16),
               u_bf, vt_bf, norm_f_w[None, :])

    ctx_attn = lambda q, kb, vb: _ctx_attn_call(q, kb, vb, sp)
    h0_p = jnp.zeros((bp, 2, D_RG), F32)
    y_prompt, k_p, v_p, h_last = _path(x_prompt, mod3, 0, False, weights, ctx_attn, h0_p)

    bias_tab = _bias_table(rpb[l], ss // GRID_W)
    ck = cache_k[:, l].reshape(bs * past, D_ATT)
    cv = cache_v[:, l].reshape(bs * past, D_ATT)
    nbr_attn = lambda q, kb, vb: _nbr_attn_call(q, kb, vb, ck, cv, bias_tab, bs, ss, past)
    y_sample, _, _, _ = _path(x_sample, mod3, 1, True, weights, nbr_attn, state_rglru[:, l])

    new_k = k_p.reshape(bp, 1, sp, N_HEADS, HEAD_DIM)
    new_v = v_p.reshape(bp, 1, sp, N_HEADS, HEAD_DIM)
    new_h = h_last.reshape(bp, 1, 2, D_RG)
    return (y_prompt, y_sample, new_k, new_v, new_h)
```

```python
import functools

import numpy as np
import jax
import jax.numpy as jnp
from jax import lax
from jax.experimental import pallas as pl
from jax.experimental.pallas import tpu as pltpu

F32 = jnp.float32
BF16 = jnp.bfloat16

D = 1024
D_ATT = 512
N_HEADS = 8
HEAD_DIM = 64
GRID_W = 64
WIN_H = 8
WIN_W = 16
D_RG = 512
D_IN = 3 * D_ATT + 2 * D_RG
N_KEYS = 128
P_HEADS = 8
TOPK = 16
N_EXPERTS = N_KEYS * N_KEYS
RMS_EPS = 1e-6
NEG_INF = -1e30
RG_C = 8.0
ATT_SCALE = HEAD_DIM ** -0.5

VMEM_LIMIT = 56 * 1024 * 1024

_NT = (((1,), (1,)), ((), ()))


def _cparams(*sem, flags=None):
    return pltpu.CompilerParams(dimension_semantics=sem, vmem_limit_bytes=VMEM_LIMIT, flags=flags)


def _rmsnorm(x, w):
    ms = jnp.mean(x * x, axis=-1, keepdims=True)
    return x * lax.rsqrt(ms + RMS_EPS) * w


def _mod_kernel(c_ref, w_ref, b_ref, o_ref):
    cv = c_ref[...]
    s = cv * jax.nn.sigmoid(cv)
    o_ref[...] = jnp.dot(s.astype(BF16), w_ref[...].astype(BF16),
                         preferred_element_type=F32) + b_ref[...]


def _mod_call(cvec, w_mod, b_mod):
    rows = cvec.shape[0]
    tn = 1536
    return pl.pallas_call(
        _mod_kernel,
        grid=(6 * D // tn,),
        in_specs=[pl.BlockSpec((rows, D), lambda j: (0, 0)),
                  pl.BlockSpec((D, tn), lambda j: (0, j)),
                  pl.BlockSpec((1, tn), lambda j: (0, j))],
        out_specs=pl.BlockSpec((rows, tn), lambda j: (0, j)),
        out_shape=jax.ShapeDtypeStruct((rows, 6 * D), F32),
        compiler_params=_cparams("arbitrary"),
        name="mod",
    )(cvec, w_mod, b_mod)


def _inproj_kernel(x_ref, mod_ref, nw_ref, w_ref, q_ref, kb_ref, vb_ref, kf_ref, vf_ref, xr_ref, yr_ref):
    x = x_ref[...]
    h = _rmsnorm(x, nw_ref[...]) * (1.0 + mod_ref[0, 1:2, :]) + mod_ref[0, 0:1, :]
    p = jnp.dot(h.astype(BF16), w_ref[...], preferred_element_type=F32)
    q_ref[...] = (p[:, 0:D_ATT] * ATT_SCALE).astype(BF16)
    k = p[:, D_ATT:2 * D_ATT]
    v = p[:, 2 * D_ATT:3 * D_ATT]
    kb_ref[...] = k.astype(BF16)
    vb_ref[...] = v.astype(BF16)
    kf_ref[...] = k
    vf_ref[...] = v
    xr_ref[...] = p[:, 3 * D_ATT:3 * D_ATT + D_RG]
    yr_ref[...] = p[:, 3 * D_ATT + D_RG:]


def _inproj_call(x2d, mod3, mod_base, tiles_per_batch, norm_w, w_in_bf):
    n = x2d.shape[0]
    tm = 512
    row = lambda t: (mod_base + t // tiles_per_batch, 0, 0)
    tok = lambda t: (t, 0)
    shp = lambda w, dt: jax.ShapeDtypeStruct((n, w), dt)
    return pl.pallas_call(
        _inproj_kernel,
        grid=(n // tm,),
        in_specs=[pl.BlockSpec((tm, D), tok),
                  pl.BlockSpec((1, 6, D), row),
                  pl.BlockSpec((1, D), lambda t: (0, 0)),
                  pl.BlockSpec((D, D_IN), lambda t: (0, 0))],
        out_specs=[pl.BlockSpec((tm, D_ATT), tok)] * 5 + [pl.BlockSpec((tm, D_RG), tok)] * 2,
        out_shape=[shp(D_ATT, BF16), shp(D_ATT, BF16), shp(D_ATT, BF16), shp(D_ATT, F32), shp(D_ATT, F32),
                   shp(D_RG, F32), shp(D_RG, F32)],
        compiler_params=_cparams("arbitrary"),
        name="inproj",
    )(x2d, mod3, norm_w, w_in_bf)


def _ctx_attn_kernel(q_ref, k_ref, v_ref, o_ref):
    for h in range(N_HEADS):
        sl = slice(h * HEAD_DIM, (h + 1) * HEAD_DIM)
        s = lax.dot_general(q_ref[:, sl], k_ref[:, sl], _NT, preferred_element_type=F32)
        m = jnp.max(s, axis=-1, keepdims=True)
        p = jnp.exp(s - m)
        l = jnp.sum(p, axis=-1, keepdims=True)
        o = jnp.dot(p.astype(BF16), v_ref[:, sl], preferred_element_type=F32)
        o_ref[:, sl] = o / l


def _ctx_attn_call(q, kb, vb, seq):
    n = q.shape[0]
    blk = pl.BlockSpec((seq, D_ATT), lambda b: (b, 0))
    return pl.pallas_call(
        _ctx_attn_kernel,
        grid=(n // seq,),
        in_specs=[blk, blk, blk],
        out_specs=blk,
        out_shape=jax.ShapeDtypeStruct((n, D_ATT), F32),
        compiler_params=_cparams("arbitrary"),
        name="ctx_attn",
    )(q, kb, vb)


def _nbr_attn_kernel(q_ref, k_ref, v_ref, ck_ref, cv_ref, bias_ref, o_ref, *, rows):
    r = pl.program_id(1)
    start = jnp.clip(r - WIN_H // 2, 0, rows - WIN_H)
    row0 = pl.multiple_of(start * GRID_W, GRID_W)
    n_loc = WIN_H * GRID_W
    qc = lax.broadcasted_iota(jnp.int32, (GRID_W, n_loc), 0)
    kc = lax.broadcasted_iota(jnp.int32, (GRID_W, n_loc), 1) & (GRID_W - 1)
    cs = jnp.clip(qc - WIN_W // 2, 0, GRID_W - WIN_W)
    valid = (kc >= cs) & (kc < cs + WIN_W)
    for h in range(N_HEADS):
        sl = slice(h * HEAD_DIM, (h + 1) * HEAD_DIM)
        q = q_ref[:, sl]
        kl = k_ref[pl.ds(row0, n_loc), sl]
        vl = v_ref[pl.ds(row0, n_loc), sl]
        s_loc = lax.dot_general(q, kl, _NT, preferred_element_type=F32) + bias_ref[h, 0]
        s_loc = jnp.where(valid, s_loc, NEG_INF)
        s_ctx = lax.dot_general(q, ck_ref[:, sl].astype(BF16), _NT, preferred_element_type=F32)
        m = jnp.maximum(jnp.max(s_loc, axis=-1, keepdims=True), jnp.max(s_ctx, axis=-1, keepdims=True))
        p_loc = jnp.exp(s_loc - m)
        p_ctx = jnp.exp(s_ctx - m)
        l = jnp.sum(p_loc, axis=-1, keepdims=True) + jnp.sum(p_ctx, axis=-1, keepdims=True)
        o = (jnp.dot(p_loc.astype(BF16), vl, preferred_element_type=F32)
             + jnp.dot(p_ctx.astype(BF16), cv_ref[:, sl].astype(BF16), preferred_element_type=F32))
        o_ref[:, sl] = o / l


def _bias_pattern(r, rows):
    lo = WIN_H // 2
    hi = rows - WIN_H + lo
    return jnp.where(r < lo, r, jnp.where(r <= hi, lo, r - (hi - lo)))


def _nbr_attn_call(q, kb, vb, cache_k2d, cache_v2d, bias_tab, batch, seq, past):
    rows = seq // GRID_W
    n = q.shape[0]
    return pl.pallas_call(
        functools.partial(_nbr_attn_kernel, rows=rows),
        grid=(batch, rows),
        in_specs=[pl.BlockSpec((GRID_W, D_ATT), lambda b, r: (b * rows + r, 0)),
                  pl.BlockSpec((seq, D_ATT), lambda b, r: (b, 0)),
                  pl.BlockSpec((seq, D_ATT), lambda b, r: (b, 0)),
                  pl.BlockSpec((past, D_ATT), lambda b, r: (b, 0)),
                  pl.BlockSpec((past, D_ATT), lambda b, r: (b, 0)),
                  pl.BlockSpec((N_HEADS, 1, GRID_W, WIN_H * GRID_W),
                               lambda b, r: (0, _bias_pattern(r, rows), 0, 0))],
        out_specs=pl.BlockSpec((GRID_W, D_ATT), lambda b, r: (b * rows + r, 0)),
        out_shape=jax.ShapeDtypeStruct((n, D_ATT), F32),
        compiler_params=_cparams("arbitrary", "arbitrary"),
        name="nbr_attn",
    )(q, kb, vb, cache_k2d, cache_v2d, bias_tab)


def _bias_table(rpb_l, rows):
    lo = WIN_H // 2
    hi = rows - WIN_H + lo
    rep_rows = list(range(lo)) + [lo] + list(range(hi + 1, rows))
    first = [min(max(r - lo, 0), rows - WIN_H) - r + (WIN_H - 1) for r in rep_rows]
    rsel = jnp.stack([rpb_l[:, f:f + WIN_H, :] for f in first], axis=1)
    qc = np.arange(GRID_W)
    off = qc[None, :] - qc[:, None] + (WIN_W - 1)
    place = (off[None, :, :] == np.arange(2 * WIN_W - 1)[:, None, None]).astype(np.float32)
    tab = jnp.einsum('hpjd,dqk->hpqjk', rsel, jnp.asarray(place), precision=lax.Precision.HIGHEST)
    return tab.reshape(N_HEADS, len(rep_rows), GRID_W, WIN_H * GRID_W)


def _log_sigmoid(x):
    return jnp.minimum(x, 0.0) - jnp.log(1.0 + jnp.exp(-jnp.abs(x)))


def _gelu_tanh(x):
    return 0.5 * x * (1.0 + jnp.tanh(0.7978845608028654 * (x + 0.044715 * x * x * x)))


def _rglru_kernel(xr_ref, yr_ref, h0_ref, cw_ref, cb_ref, wbd_ref, ba_ref, bi_ref, lam_ref,
                  y_ref, hl_ref, xpad, a_f, b_f, a_b, b_b):
    T = xr_ref.shape[0]
    CH = 256
    xpad[0:8, :] = jnp.zeros((8, D_RG), F32)
    xpad[T + 8:T + 16, :] = jnp.zeros((8, D_RG), F32)
    xpad[8:T + 8, :] = xr_ref[...]
    a_refs = (a_f, a_b)
    b_refs = (b_f, b_b)
    for c0 in range(0, T, CH):
        xc = cb_ref[...] + cw_ref[0:1, :] * xpad[c0 + 6:c0 + 6 + CH, :]
        for i in range(1, 4):
            xc = xc + cw_ref[i:i + 1, :] * xpad[c0 + 6 + i:c0 + 6 + i + CH, :]
        xcb = xc.astype(BF16)
        for d in range(2):
            gates = []
            for g in range(2):
                halves = [jnp.dot(xcb[:, hf * 256:(hf + 1) * 256], wbd_ref[d, g, hf],
                                  preferred_element_type=F32) for hf in range(2)]
                gates.append(jnp.concatenate(halves, axis=-1))
            rg = jax.nn.sigmoid(gates[0] + ba_ref[d:d + 1, :])
            ig = jax.nn.sigmoid(gates[1] + bi_ref[d:d + 1, :])
            log_a = RG_C * rg * _log_sigmoid(lam_ref[d:d + 1, :])
            a = jnp.exp(log_a)
            b = jnp.sqrt(1.0 - a * a) * (ig * xc)
            a_refs[d][c0:c0 + CH, :] = a
            b_refs[d][c0:c0 + CH, :] = b

    sub = lax.broadcasted_iota(jnp.int32, (8, D_RG), 0)
    n_tiles = T // 8

    def scan_tile(a_ref, b_ref, t8, h, order):
        base = pl.multiple_of(t8 * 8, 8)
        a_blk = a_ref[pl.ds(base, 8), :]
        b_blk = b_ref[pl.ds(base, 8), :]
        out = b_blk
        for j in order:
            cand = a_blk * h + b_blk
            h = jnp.broadcast_to(cand[j:j + 1, :], (8, D_RG))
            out = jnp.where(sub == j, cand, out)
        b_ref[pl.ds(base, 8), :] = out
        return h

    def body(i, carry):
        hf, hb = carry
        hf = scan_tile(a_f, b_f, i, hf, range(8))
        hb = scan_tile(a_b, b_b, n_tiles - 1 - i, hb, range(7, -1, -1))
        return hf, hb

    hf0 = jnp.broadcast_to(h0_ref[0, 0:1, :], (8, D_RG))
    hb0 = jnp.broadcast_to(h0_ref[0, 1:2, :], (8, D_RG))
    hf, hb = lax.fori_loop(0, n_tiles, body, (hf0, hb0))
    hl_ref[0, 0:1, :] = hf[0:1, :]
    hl_ref[0, 1:2, :] = hb[0:1, :]
    for c0 in range(0, T, CH):
        sl = slice(c0, c0 + CH)
        y_ref[sl, :] = (b_f[sl, :] + b_b[sl, :]) * _gelu_tanh(yr_ref[sl, :])


def _rglru_call(xr, yr, h0, conv_w, conv_b, wbd, b_a, b_i, lam, seq):
    n = xr.shape[0]
    batch = n // seq
    tok = pl.BlockSpec((seq, D_RG), lambda b: (b, 0))
    full = lambda shape: pl.BlockSpec(shape, lambda b: (0,) * len(shape))
    return pl.pallas_call(
        _rglru_kernel,
        grid=(batch,),
        in_specs=[tok, tok, pl.BlockSpec((1, 2, D_RG), lambda b: (b, 0, 0)),
                  full((4, D_RG)), full((1, D_RG)), full((2, 2, 2, 256, 256)),
                  full((2, D_RG)), full((2, D_RG)), full((2, D_RG))],
        out_specs=[tok, pl.BlockSpec((1, 2, D_RG), lambda b: (b, 0, 0))],
        out_shape=[jax.ShapeDtypeStruct((n, D_RG), F32), jax.ShapeDtypeStruct((batch, 2, D_RG), F32)],
        scratch_shapes=[pltpu.VMEM((seq + 16, D_RG), F32)] + [pltpu.VMEM((seq, D_RG), F32)] * 4,
        compiler_params=_cparams("arbitrary"),
        name="rglru",
    )(xr, yr, h0, conv_w, conv_b, wbd, b_a, b_i, lam)


def _block_diag_halves(w):
    d = w.shape[0]
    w4 = w.reshape(d, 2, 4, 64, 64)
    eye = jnp.eye(4, dtype=w.dtype)
    out = jnp.einsum('dhncf,nm->dhncmf', w4, eye)
    return out.reshape(d, 2, 256, 256)


N_HP = 2 * P_HEADS


def _candidate_flat(tm):
    subf = lax.broadcasted_iota(jnp.int32, (8, tm), 0).astype(F32)
    flat = [subf + float(r1 * TOPK) for r1 in range(4)]
    flat.append(subf + 8.0)
    flat += [subf * float(TOPK) + float(r2) for r2 in range(3)]
    flat.append((subf + 8.0) * float(TOPK))
    return jnp.concatenate(flat, axis=0)


def _candidate_sums(t1, t2):
    tm = t1.shape[1]
    sub = lax.broadcasted_iota(jnp.int32, (8, tm), 0)
    ninf = -jnp.inf
    lim_a = (8, 8, 5, 4)
    vals = [jnp.where(sub < lim_a[r1], t1[r1:r1 + 1, :] + t2[0:8, :], ninf) for r1 in range(4)]
    vals.append(t1[0:1, :] + t2[8:16, :])
    for r2 in range(3):
        ok = (sub >= 4) if r2 < 2 else (sub == 4)
        vals.append(jnp.where(ok, t1[0:8, :] + t2[r2:r2 + 1, :], ninf))
    vals.append(t1[8:16, :] + t2[0:1, :])
    return jnp.concatenate(vals, axis=0)


def _staircase_rows(sel):
    rows = []
    for r1 in range(4):
        l = jnp.sum(sel[8 * r1:8 * r1 + 8, :], axis=0, keepdims=True)
        if r1 == 0:
            l = l + jnp.sum(sel[32:40, :], axis=0, keepdims=True)
        rows.append(l)
    l_mid = sel[40:48, :] + sel[48:56, :] + sel[56:64, :]
    l_hi = sel[64:72, :]
    rows += [l_mid[r1:r1 + 1, :] for r1 in range(4, 8)]
    rows += [l_hi[r1 - 8:r1 - 7, :] for r1 in range(8, 16)]
    return rows


def _route_tile(sk_ref, qt_scr, s_scr, w_scr, rank_scr, t_scr, cand_scr, sel_scr, z_scr,
                a1_ref, c_ref, b2_ref, e2_ref):
    tm = qt_scr.shape[1]
    iota_f = lax.broadcasted_iota(jnp.int32, (N_KEYS, tm), 0).astype(F32)
    flat = _candidate_flat(tm)
    ninf = -jnp.inf
    for hp in range(N_HP):
        s_scr[hp] = jnp.dot(sk_ref[hp % 2], qt_scr[hp * N_KEYS:(hp + 1) * N_KEYS, :],
                            preferred_element_type=F32)

    def extract(exact):
        for hp in range(N_HP):
            w_scr[hp] = s_scr[hp]
            rank_scr[hp] = jnp.full((N_KEYS, tm), float(TOPK), F32)

        def round1(r, carry):
            rf = r.astype(F32)
            for hp in range(N_HP):
                w = w_scr[hp]
                m = jnp.max(w, axis=0, keepdims=True)
                if exact:
                    idx = jnp.min(jnp.where(w == m, iota_f, float(N_KEYS)), axis=0, keepdims=True)
                    hit = iota_f == idx
                else:
                    hit = w == m
                rank_scr[hp] = jnp.where(hit, rf, rank_scr[hp])
                w_scr[hp] = jnp.where(hit, ninf, w)
                t_scr[hp, pl.ds(r, 1), :] = m
            return carry

        lax.fori_loop(0, TOPK, round1, 0)

        for h in range(P_HEADS):
            cand_scr[h] = _candidate_sums(t_scr[2 * h], t_scr[2 * h + 1])
            sel_scr[h] = jnp.zeros((72, tm), F32)

        def round2(r, carry):
            zs, b0s = carry
            zs_new, b0s_new = [], []
            for h in range(P_HEADS):
                w = cand_scr[h]
                m = jnp.max(w, axis=0, keepdims=True)
                if exact:
                    f = jnp.min(jnp.where(w == m, flat, 1e9), axis=0, keepdims=True)
                    hit = flat == f
                else:
                    hit = w == m
                sel_scr[h] = jnp.where(hit, 1.0, sel_scr[h])
                cand_scr[h] = jnp.where(hit, ninf, w)
                b0 = jnp.where(r == 0, m, b0s[h])
                zs_new.append(zs[h] + jnp.exp(m - b0))
                b0s_new.append(b0)
            return tuple(zs_new), tuple(b0s_new)

        zero_row = jnp.zeros((1, tm), F32)
        zs, _ = lax.fori_loop(0, TOPK, round2, ((zero_row,) * P_HEADS, (zero_row,) * P_HEADS))
        for h in range(P_HEADS):
            z_scr[h:h + 1, :] = zs[h]

    extract(False)
    excess = jnp.zeros((1, tm), F32)
    for hp in range(N_HP):
        cnt = jnp.sum(jnp.where(rank_scr[hp] < float(TOPK), 1.0, 0.0), axis=0, keepdims=True)
        excess = jnp.maximum(excess, jnp.abs(cnt - float(TOPK)))
    for h in range(P_HEADS):
        cnt = jnp.sum(sel_scr[h], axis=0, keepdims=True)
        excess = jnp.maximum(excess, jnp.abs(cnt - float(TOPK)))

    @pl.when(jnp.max(excess) > 0.0)
    def _():
        extract(True)

    for h in range(P_HEADS):
        l_rows = _staircase_rows(sel_scr[h])
        rank1 = rank_scr[2 * h]
        a1 = jnp.zeros((N_KEYS, tm), F32)
        for r1 in range(TOPK):
            a1 = jnp.where(rank1 == float(r1), l_rows[r1], a1)
        a1_ref[h] = a1
        c_ref[h] = jnp.exp(s_scr[2 * h] - t_scr[2 * h, 0:1, :]) * (1.0 / z_scr[h:h + 1, :])
        b2_ref[h] = rank_scr[2 * h + 1]
        e2_ref[h] = jnp.exp(s_scr[2 * h + 1] - t_scr[2 * h + 1, 0:1, :])


def _mixffn_kernel(x_ref, att_ref, rg_ref, mod_ref, wo_ref, nw_ref, wqt_ref, sk_ref,
                   x1_ref, h2t_ref, a1_ref, c_ref, b2_ref, e2_ref,
                   qt_scr, s_scr, w_scr, rank_scr, t_scr, cand_scr, sel_scr, z_scr):
    o = (jnp.dot(att_ref[...].astype(BF16), wo_ref[0:D_ATT, :], preferred_element_type=F32)
         + jnp.dot(rg_ref[...].astype(BF16), wo_ref[D_ATT:, :], preferred_element_type=F32))
    x1 = x_ref[...] + mod_ref[0, 2:3, :] * o
    x1_ref[...] = x1
    h2 = _rmsnorm(x1, nw_ref[...]) * (1.0 + mod_ref[0, 4:5, :]) + mod_ref[0, 3:4, :]
    h2t = h2.T.astype(BF16)
    h2t_ref[...] = h2t
    qt_scr[...] = jnp.dot(wqt_ref[...], h2t, preferred_element_type=F32).astype(BF16)
    _route_tile(sk_ref, qt_scr, s_scr, w_scr, rank_scr, t_scr, cand_scr, sel_scr, z_scr,
                a1_ref, c_ref, b2_ref, e2_ref)


def _mixffn_call(x2d, att, rg, mod3, mod_base, tiles_per_batch, w_out_bf, norm_w, w_qt_bf, sk_bf, tm):
    n = x2d.shape[0]
    tok = lambda t: (t, 0)
    route = pl.BlockSpec((P_HEADS, N_KEYS, tm), lambda t: (0, 0, t))
    route_shape = jax.ShapeDtypeStruct((P_HEADS, N_KEYS, n), F32)
    return pl.pallas_call(
        _mixffn_kernel,
        grid=(n // tm,),
        in_specs=[pl.BlockSpec((tm, D), tok),
                  pl.BlockSpec((tm, D_ATT), tok),
                  pl.BlockSpec((tm, D_RG), tok),
                  pl.BlockSpec((1, 6, D), lambda t: (mod_base + t // tiles_per_batch, 0, 0)),
                  pl.BlockSpec((D, D), lambda t: (0, 0)),
                  pl.BlockSpec((1, D), lambda t: (0, 0)),
                  pl.BlockSpec((2 * P_HEADS * N_KEYS, D), lambda t: (0, 0)),
                  pl.BlockSpec((2, N_KEYS, N_KEYS), lambda t: (0, 0, 0))],
        out_specs=[pl.BlockSpec((tm, D), tok),
                   pl.BlockSpec((D, tm), lambda t: (0, t)),
                   route, route, route, route],
        out_shape=[jax.ShapeDtypeStruct((n, D), F32), jax.ShapeDtypeStruct((D, n), BF16),
                   route_shape, route_shape, route_shape, route_shape],
        scratch_shapes=[pltpu.VMEM((2 * P_HEADS * N_KEYS, tm), BF16)]
                       + [pltpu.VMEM((N_HP, N_KEYS, tm), F32)] * 3
                       + [pltpu.VMEM((N_HP, TOPK, tm), F32)]
                       + [pltpu.VMEM((P_HEADS, 72, tm), F32)] * 2
                       + [pltpu.VMEM((P_HEADS, tm), F32)],
        compiler_params=_cparams("arbitrary"),
        name="mixffn",
    )(x2d, att, rg, mod3, w_out_bf, norm_w, w_qt_bf, sk_bf)


def _expert_prep_kernel(u_ref, v_ref, ub_ref, vt_ref):
    ub_ref[...] = u_ref[...].astype(BF16)
    vt_ref[...] = v_ref[...].T.astype(BF16)


def _expert_prep_call(u, v):
    te = 512
    return pl.pallas_call(
        _expert_prep_kernel,
        grid=(N_EXPERTS // te,),
        in_specs=[pl.BlockSpec((te, D), lambda j: (j, 0)), pl.BlockSpec((te, D), lambda j: (j, 0))],
        out_specs=[pl.BlockSpec((te, D), lambda j: (j, 0)), pl.BlockSpec((D, te), lambda j: (0, j))],
        out_shape=[jax.ShapeDtypeStruct((N_EXPERTS, D), BF16), jax.ShapeDtypeStruct((D, N_EXPERTS), BF16)],
        compiler_params=_cparams("arbitrary"),
        name="expert_prep",
    )(u, v)


def _peer_kernel(h2t_ref, u_ref, vt_ref, a1_ref, c_ref, b2_ref, e2_ref, x1_ref, mod_ref, nw_ref,
                 y_ref, acc, g_scr, b2_scr, e2_scr, *, i1_per_step):
    j = pl.program_id(1)
    tm = acc.shape[1]

    @pl.when(j == 0)
    def _():
        acc[...] = jnp.zeros_like(acc)
        b2_scr[...] = b2_ref[...].astype(BF16)
        e2_scr[...] = e2_ref[...].astype(BF16)

    grp = 4
    for i0 in range(0, i1_per_step, grp):
        for l in range(tm // 128):
            ls = slice(l * 128, (l + 1) * 128)
            g = [None] * grp
            for h in range(P_HEADS):
                b2 = b2_scr[h, :, ls]
                e2 = e2_scr[h, :, ls]
                for k in range(grp):
                    a1 = a1_ref[h, i0 + k:i0 + k + 1, ls].astype(BF16)
                    ch = (0.5 * c_ref[h, i0 + k:i0 + k + 1, ls]).astype(BF16)
                    w = jnp.where(b2 < a1, ch * e2, jnp.zeros_like(e2))
                    g[k] = w if g[k] is None else g[k] + w
            for k in range(grp):
                g_scr[(i0 + k) * N_KEYS:(i0 + k + 1) * N_KEYS, ls] = g[k]
    st = jnp.dot(u_ref[...], h2t_ref[...], preferred_element_type=F32)
    act = st * (1.0 + lax.erf(st * 0.7071067811865476))
    at = act.astype(BF16) * g_scr[...]
    acc[...] += jnp.dot(vt_ref[...], at, preferred_element_type=F32)

    @pl.when(j == pl.num_programs(1) - 1)
    def _():
        x2 = x1_ref[...] + mod_ref[0, 5:6, :] * acc[...].T
        y_ref[...] = _rmsnorm(x2, nw_ref[...])


def _peer_call(h2t, u_bf, vt_bf, a1, c, b2, e2, x1, mod3, mod_base, tiles_per_batch, norm_f_w, tm, te):
    n = x1.shape[0]
    ips = te // N_KEYS
    route_lo = pl.BlockSpec((P_HEADS, ips, tm), lambda t, j: (0, j, t))
    route_full = pl.BlockSpec((P_HEADS, N_KEYS, tm), lambda t, j: (0, 0, t))
    return pl.pallas_call(
        functools.partial(_peer_kernel, i1_per_step=ips),
        grid=(n // tm, N_EXPERTS // te),
        in_specs=[pl.BlockSpec((D, tm), lambda t, j: (0, t)),
                  pl.BlockSpec((te, D), lambda t, j: (j, 0)),
                  pl.BlockSpec((D, te), lambda t, j: (0, j)),
                  route_lo, route_lo, route_full, route_full,
                  pl.BlockSpec((tm, D), lambda t, j: (t, 0)),
                  pl.BlockSpec((1, 6, D), lambda t, j: (mod_base + t // tiles_per_batch, 0, 0)),
                  pl.BlockSpec((1, D), lambda t, j: (0, 0))],
        out_specs=pl.BlockSpec((tm, D), lambda t, j: (t, 0)),
        out_shape=jax.ShapeDtypeStruct((n, D), F32),
        scratch_shapes=[pltpu.VMEM((D, tm), F32), pltpu.VMEM((te, tm), BF16),
                        pltpu.VMEM((P_HEADS, N_KEYS, tm), BF16), pltpu.VMEM((P_HEADS, N_KEYS, tm), BF16)],
        compiler_params=_cparams("arbitrary", "arbitrary"),
        name="peer",
    )(h2t, u_bf, vt_bf, a1, c, b2, e2, x1, mod3, norm_f_w)


def _path(x, mod3, mod_base, per_batch_mod, weights, attn_fn, h0):
    (norm_mix_w, w_in_bf, rg_params, w_out_bf, norm_ffn_w, w_qt_bf, sk_bf, u_bf, vt_bf, norm_f_w) = weights
    bsz, seq, _ = x.shape
    n = bsz * seq
    x2d = x.reshape(n, D)
    tpb = lambda tm: (seq // tm) if per_batch_mod else n
    q, kb, vb, kf, vf, xr, yr = _inproj_call(x2d, mod3, mod_base, tpb(512), norm_mix_w, w_in_bf)
    att = attn_fn(q, kb, vb)
    rg_out, h_last = _rglru_call(xr, yr, h0, *rg_params, seq)
    tm_mix = 256
    x1, h2t, a1, c, b2, e2 = _mixffn_call(x2d, att, rg_out, mod3, mod_base, tpb(tm_mix), w_out_bf,
                                          norm_ffn_w, w_qt_bf, sk_bf, tm_mix)
    tm_peer = 512
    y = _peer_call(h2t, u_bf, vt_bf, a1, c, b2, e2, x1, mod3, mod_base, tpb(tm_peer), norm_f_w,
                   tm_peer, 1024)
    return y.reshape(bsz, seq, D), kf, vf, h_last


def kernel(x_prompt, x_sample, c, cache_k, cache_v, state_rglru, c_ctx, w_mod, b_mod, norm_mix_w, w_in,
           rpb, conv_w, conv_b, rg_w_a, rg_b_a, rg_w_i, rg_b_i, rg_lambda, w_out, norm_ffn_w,
           peer_w_q, peer_sub_keys, peer_u, peer_v, norm_f_w):
    depth = w_mod.shape[0]
    assert depth == 1, "single-layer problem"
    l = 0
    bp, sp, _ = x_prompt.shape
    bs, ss, _ = x_sample.shape
    past = cache_k.shape[2]

    n_rows = 16
    cvec = jnp.concatenate([c_ctx[None, :], c, jnp.zeros((n_rows - 1 - bs, D), F32)], axis=0)
    mod3 = _mod_call(cvec, w_mod[l], b_mod[l][None, :]).reshape(n_rows, 6, D)

    wbd = jnp.stack([_block_diag_halves(rg_w_a[l]), _block_diag_halves(rg_w_i[l])], axis=1).astype(BF16)
    rg_params = (conv_w[l], conv_b[l][None, :], wbd, rg_b_a[l], rg_b_i[l], rg_lambda[l])
    u_bf, vt_bf = _expert_prep_call(peer_u[l], peer_v[l])
    weights = (norm_mix_w[l][None, :], w_in[l].astype(BF16), rg_params, w_out[l].astype(BF16),
               norm_ffn_w[l][None, :], peer_w_q[l].T.astype(BF16), peer_sub_keys[l].astype(BF16),
               u_bf, vt_bf, norm_f_w[None, :])

    ctx_attn = lambda q, kb, vb: _ctx_attn_call(q, kb, vb, sp)
    h0_p = jnp.zeros((bp, 2, D_RG), F32)
    y_prompt, k_p, v_p, h_last = _path(x_prompt, mod3, 0, False, weights, ctx_attn, h0_p)

    bias_tab = _bias_table(rpb[l], ss // GRID_W)
    ck = cache_k[:, l].reshape(bs * past, D_ATT)
    cv = cache_v[:, l].reshape(bs * past, D_ATT)
    nbr_attn = lambda q, kb, vb: _nbr_attn_call(q, kb, vb, ck, cv, bias_tab, bs, ss, past)
    y_sample, _, _, _ = _path(x_sample, mod3, 1, True, weights, nbr_attn, state_rglru[:, l])

    new_k = k_p.reshape(bp, 1, sp, N_HEADS, HEAD_DIM)
    new_v = v_p.reshape(bp, 1, sp, N_HEADS, HEAD_DIM)
    new_h = h_last.reshape(bp, 1, 2, D_RG)
    return (y_prompt, y_sample, new_k, new_v, new_h)
```

```python
import functools

import numpy as np
import jax
import jax.numpy as jnp
from jax import lax
from jax.experimental import pallas as pl
from jax.experimental.pallas import tpu as pltpu

F32 = jnp.float32
BF16 = jnp.bfloat16

D = 1024
D_ATT = 512
N_HEADS = 8
HEAD_DIM = 64
GRID_W = 64
WIN_H = 8
WIN_W = 16
D_RG = 512
D_IN = 3 * D_ATT + 2 * D_RG
N_KEYS = 128
P_HEADS = 8
TOPK = 16
N_EXPERTS = N_KEYS * N_KEYS
RMS_EPS = 1e-6
NEG_INF = -1e30
RG_C = 8.0
ATT_SCALE = HEAD_DIM ** -0.5

VMEM_LIMIT = 56 * 1024 * 1024

_NT = (((1,), (1,)), ((), ()))


def _cparams(*sem, flags=None):
    return pltpu.CompilerParams(dimension_semantics=sem, vmem_limit_bytes=VMEM_LIMIT, flags=flags)


def _rmsnorm(x, w):
    ms = jnp.mean(x * x, axis=-1, keepdims=True)
    return x * lax.rsqrt(ms + RMS_EPS) * w


def _mod_kernel(c_ref, w_ref, b_ref, o_ref):
    cv = c_ref[...]
    s = cv * jax.nn.sigmoid(cv)
    o_ref[...] = jnp.dot(s.astype(BF16), w_ref[...].astype(BF16),
                         preferred_element_type=F32) + b_ref[...]


def _mod_call(cvec, w_mod, b_mod):
    rows = cvec.shape[0]
    tn = 1536
    return pl.pallas_call(
        _mod_kernel,
        grid=(6 * D // tn,),
        in_specs=[pl.BlockSpec((rows, D), lambda j: (0, 0)),
                  pl.BlockSpec((D, tn), lambda j: (0, j)),
                  pl.BlockSpec((1, tn), lambda j: (0, j))],
        out_specs=pl.BlockSpec((rows, tn), lambda j: (0, j)),
        out_shape=jax.ShapeDtypeStruct((rows, 6 * D), F32),
        compiler_params=_cparams("arbitrary"),
        name="mod",
    )(cvec, w_mod, b_mod)


def _inproj_kernel(x_ref, mod_ref, nw_ref, w_ref, *out_refs, emit_f32_kv):
    if emit_f32_kv:
        q_ref, kb_ref, vb_ref, xr_ref, yr_ref, kf_ref, vf_ref = out_refs
    else:
        q_ref, kb_ref, vb_ref, xr_ref, yr_ref = out_refs
    x = x_ref[...]
    h = _rmsnorm(x, nw_ref[...]) * (1.0 + mod_ref[0, 1:2, :]) + mod_ref[0, 0:1, :]
    p = jnp.dot(h.astype(BF16), w_ref[...], preferred_element_type=F32)
    q_ref[...] = (p[:, 0:D_ATT] * ATT_SCALE).astype(BF16)
    k = p[:, D_ATT:2 * D_ATT]
    v = p[:, 2 * D_ATT:3 * D_ATT]
    kb_ref[...] = k.astype(BF16)
    vb_ref[...] = v.astype(BF16)
    xr_ref[...] = p[:, 3 * D_ATT:3 * D_ATT + D_RG]
    yr_ref[...] = p[:, 3 * D_ATT + D_RG:]
    if emit_f32_kv:
        kf_ref[...] = k
        vf_ref[...] = v


def _inproj_call(x2d, mod3, mod_base, tiles_per_batch, norm_w, w_in_bf, emit_f32_kv):
    n = x2d.shape[0]
    tm = 512
    row = lambda t: (mod_base + t // tiles_per_batch, 0, 0)
    tok = lambda t: (t, 0)
    shp = lambda w, dt: jax.ShapeDtypeStruct((n, w), dt)
    out_specs = [pl.BlockSpec((tm, D_ATT), tok)] * 3 + [pl.BlockSpec((tm, D_RG), tok)] * 2
    out_shape = [shp(D_ATT, BF16)] * 3 + [shp(D_RG, F32)] * 2
    if emit_f32_kv:
        out_specs += [pl.BlockSpec((tm, D_ATT), tok)] * 2
        out_shape += [shp(D_ATT, F32)] * 2
    return pl.pallas_call(
        functools.partial(_inproj_kernel, emit_f32_kv=emit_f32_kv),
        grid=(n // tm,),
        in_specs=[pl.BlockSpec((tm, D), tok),
                  pl.BlockSpec((1, 6, D), row),
                  pl.BlockSpec((1, D), lambda t: (0, 0)),
                  pl.BlockSpec((D, D_IN), lambda t: (0, 0))],
        out_specs=out_specs,
        out_shape=out_shape,
        compiler_params=_cparams("arbitrary"),
        name="inproj",
    )(x2d, mod3, norm_w, w_in_bf)


def _ctx_attn_kernel(q_ref, k_ref, v_ref, o_ref):
    for h in range(N_HEADS):
        sl = slice(h * HEAD_DIM, (h + 1) * HEAD_DIM)
        s = lax.dot_general(q_ref[:, sl], k_ref[:, sl], _NT, preferred_element_type=F32)
        m = jnp.max(s, axis=-1, keepdims=True)
        p = jnp.exp(s - m)
        l = jnp.sum(p, axis=-1, keepdims=True)
        o = jnp.dot(p.astype(BF16), v_ref[:, sl], preferred_element_type=F32)
        o_ref[:, sl] = o / l


def _ctx_attn_call(q, kb, vb, seq):
    n = q.shape[0]
    blk = pl.BlockSpec((seq, D_ATT), lambda b: (b, 0))
    return pl.pallas_call(
        _ctx_attn_kernel,
        grid=(n // seq,),
        in_specs=[blk, blk, blk],
        out_specs=blk,
        out_shape=jax.ShapeDtypeStruct((n, D_ATT), F32),
        compiler_params=_cparams("arbitrary"),
        name="ctx_attn",
    )(q, kb, vb)


ROWS_PER_STEP = 4


def _nbr_attn_kernel(q_ref, k_ref, v_ref, ck_ref, cv_ref, tz_ref, o_ref, *, rows):
    g = pl.program_id(0)
    n_loc = WIN_H * GRID_W
    outs = [[None] * N_HEADS for _ in range(ROWS_PER_STEP)]
    for h in range(N_HEADS):
        sl = slice(h * HEAD_DIM, (h + 1) * HEAD_DIM)
        ckh = ck_ref[:, sl]
        cvh = cv_ref[:, sl]
        qg = q_ref[:, sl]
        s_ctx_all = lax.dot_general(qg, ckh, _NT, preferred_element_type=F32)
        for i in range(ROWS_PER_STEP):
            r = g * ROWS_PER_STEP + i
            start = jnp.clip(r - WIN_H // 2, 0, rows - WIN_H)
            row0 = pl.multiple_of(start * GRID_W, GRID_W)
            d0 = start - r + (WIN_H - 1)
            bias = jnp.concatenate([tz_ref[h, d0 + 2 * p] for p in range(WIN_H // 2)], axis=-1)
            qs = slice(i * GRID_W, (i + 1) * GRID_W)
            s_loc = lax.dot_general(qg[qs], k_ref[pl.ds(row0, n_loc), sl], _NT,
                                    preferred_element_type=F32) + bias
            s_ctx = s_ctx_all[qs]
            m = jnp.maximum(jnp.max(s_loc, axis=-1, keepdims=True), jnp.max(s_ctx, axis=-1, keepdims=True))
            p_loc = jnp.exp(s_loc - m)
            p_ctx = jnp.exp(s_ctx - m)
            l = jnp.sum(p_loc, axis=-1, keepdims=True) + jnp.sum(p_ctx, axis=-1, keepdims=True)
            o = (jnp.dot(p_loc.astype(BF16), v_ref[pl.ds(row0, n_loc), sl], preferred_element_type=F32)
                 + jnp.dot(p_ctx.astype(BF16), cvh, preferred_element_type=F32))
            outs[i][h] = o / l
    for i in range(ROWS_PER_STEP):
        o_ref[i * GRID_W:(i + 1) * GRID_W, :] = jnp.concatenate(outs[i], axis=-1)


def _nbr_attn_call(q, kb, vb, ck, cv, tz, batch, seq):
    rows = seq // GRID_W
    groups = rows // ROWS_PER_STEP
    past = ck.shape[0] // batch
    n = q.shape[0]
    qblk = pl.BlockSpec((GRID_W * ROWS_PER_STEP, D_ATT), lambda g, b: (b * groups + g, 0))
    kvblk = pl.BlockSpec((seq, D_ATT), lambda g, b: (b, 0))
    cblk = pl.BlockSpec((past, D_ATT), lambda g, b: (b, 0))
    return pl.pallas_call(
        functools.partial(_nbr_attn_kernel, rows=rows),
        grid=(groups, batch),
        in_specs=[qblk, kvblk, kvblk, cblk, cblk,
                  pl.BlockSpec(tz.shape, lambda g, b: (0, 0, 0, 0))],
        out_specs=qblk,
        out_shape=jax.ShapeDtypeStruct((n, D_ATT), F32),
        compiler_params=_cparams("arbitrary", "arbitrary"),
        name="nbr_attn",
    )(q, kb, vb, ck, cv, tz)


def _bias_toeplitz(rpb_l):
    qc = np.arange(GRID_W)
    off = qc[None, :] - qc[:, None] + (WIN_W - 1)
    place = (off[None, :, :] == np.arange(2 * WIN_W - 1)[:, None, None]).astype(np.float32)
    tz = jnp.einsum('hdx,xqk->hdqk', rpb_l, jnp.asarray(place), precision=lax.Precision.HIGHEST)
    cs = np.clip(qc - WIN_W // 2, 0, GRID_W - WIN_W)
    valid = (qc[None, :] >= cs[:, None]) & (qc[None, :] < cs[:, None] + WIN_W)
    tz = jnp.where(jnp.asarray(valid)[None, None], tz, NEG_INF)
    return jnp.concatenate([tz[:, :-1], tz[:, 1:]], axis=-1)


def _log_sigmoid(x):
    return jnp.minimum(x, 0.0) - jnp.log(1.0 + jnp.exp(-jnp.abs(x)))


def _gelu_tanh(x):
    return 0.5 * x * (1.0 + jnp.tanh(0.7978845608028654 * (x + 0.044715 * x * x * x)))


def _rglru_kernel(xr_ref, yr_ref, h0_ref, cw_ref, cb_ref, wbd_ref, ba_ref, bi_ref, lam_ref,
                  y_ref, hl_ref, xpad, a_f, b_f, a_b, b_b):
    T = xr_ref.shape[0]
    CH = 256
    xpad[0:8, :] = jnp.zeros((8, D_RG), F32)
    xpad[T + 8:T + 16, :] = jnp.zeros((8, D_RG), F32)
    xpad[8:T + 8, :] = xr_ref[...]
    a_refs = (a_f, a_b)
    b_refs = (b_f, b_b)
    for c0 in range(0, T, CH):
        xc = cb_ref[...] + cw_ref[0:1, :] * xpad[c0 + 6:c0 + 6 + CH, :]
        for i in range(1, 4):
            xc = xc + cw_ref[i:i + 1, :] * xpad[c0 + 6 + i:c0 + 6 + i + CH, :]
        xcb = xc.astype(BF16)
        for d in range(2):
            gates = []
            for g in range(2):
                halves = [jnp.dot(xcb[:, hf * 256:(hf + 1) * 256], wbd_ref[d, g, hf],
                                  preferred_element_type=F32) for hf in range(2)]
                gates.append(jnp.concatenate(halves, axis=-1))
            rg = jax.nn.sigmoid(gates[0] + ba_ref[d:d + 1, :])
            ig = jax.nn.sigmoid(gates[1] + bi_ref[d:d + 1, :])
            log_a = RG_C * rg * _log_sigmoid(lam_ref[d:d + 1, :])
            a = jnp.exp(log_a)
            b = jnp.sqrt(1.0 - a * a) * (ig * xc)
            a_refs[d][c0:c0 + CH, :] = a
            b_refs[d][c0:c0 + CH, :] = b

    sub = lax.broadcasted_iota(jnp.int32, (8, D_RG), 0)
    n_tiles = T // 8

    def scan_tile(a_ref, b_ref, t8, h, order):
        base = pl.multiple_of(t8 * 8, 8)
        a_blk = a_ref[pl.ds(base, 8), :]
        b_blk = b_ref[pl.ds(base, 8), :]
        out = b_blk
        for j in order:
            cand = a_blk * h + b_blk
            h = jnp.broadcast_to(cand[j:j + 1, :], (8, D_RG))
            out = jnp.where(sub == j, cand, out)
        b_ref[pl.ds(base, 8), :] = out
        return h

    def body(i, carry):
        hf, hb = carry
        hf = scan_tile(a_f, b_f, i, hf, range(8))
        hb = scan_tile(a_b, b_b, n_tiles - 1 - i, hb, range(7, -1, -1))
        return hf, hb

    hf0 = jnp.broadcast_to(h0_ref[0, 0:1, :], (8, D_RG))
    hb0 = jnp.broadcast_to(h0_ref[0, 1:2, :], (8, D_RG))
    hf, hb = lax.fori_loop(0, n_tiles, body, (hf0, hb0))
    hl_ref[0, 0:1, :] = hf[0:1, :]
    hl_ref[0, 1:2, :] = hb[0:1, :]
    for c0 in range(0, T, CH):
        sl = slice(c0, c0 + CH)
        y_ref[sl, :] = (b_f[sl, :] + b_b[sl, :]) * _gelu_tanh(yr_ref[sl, :])


def _rglru_call(xr, yr, h0, conv_w, conv_b, wbd, b_a, b_i, lam, seq):
    n = xr.shape[0]
    batch = n // seq
    tok = pl.BlockSpec((seq, D_RG), lambda b: (b, 0))
    full = lambda shape: pl.BlockSpec(shape, lambda b: (0,) * len(shape))
    return pl.pallas_call(
        _rglru_kernel,
        grid=(batch,),
        in_specs=[tok, tok, pl.BlockSpec((1, 2, D_RG), lambda b: (b, 0, 0)),
                  full((4, D_RG)), full((1, D_RG)), full((2, 2, 2, 256, 256)),
                  full((2, D_RG)), full((2, D_RG)), full((2, D_RG))],
        out_specs=[tok, pl.BlockSpec((1, 2, D_RG), lambda b: (b, 0, 0))],
        out_shape=[jax.ShapeDtypeStruct((n, D_RG), F32), jax.ShapeDtypeStruct((batch, 2, D_RG), F32)],
        scratch_shapes=[pltpu.VMEM((seq + 16, D_RG), F32)] + [pltpu.VMEM((seq, D_RG), F32)] * 4,
        compiler_params=_cparams("arbitrary"),
        name="rglru",
    )(xr, yr, h0, conv_w, conv_b, wbd, b_a, b_i, lam)


def _block_diag_halves(w):
    d = w.shape[0]
    w4 = w.reshape(d, 2, 4, 64, 64)
    eye = jnp.eye(4, dtype=w.dtype)
    out = jnp.einsum('dhncf,nm->dhncmf', w4, eye)
    return out.reshape(d, 2, 256, 256)


N_HP = 2 * P_HEADS


def _candidate_flat(tm):
    subf = lax.broadcasted_iota(jnp.int32, (8, tm), 0).astype(F32)
    flat = [subf + float(r1 * TOPK) for r1 in range(4)]
    flat.append(subf + 8.0)
    flat += [subf * float(TOPK) + float(r2) for r2 in range(3)]
    flat.append((subf + 8.0) * float(TOPK))
    return jnp.concatenate(flat, axis=0)


def _candidate_sums(t1, t2):
    tm = t1.shape[1]
    sub = lax.broadcasted_iota(jnp.int32, (8, tm), 0)
    ninf = -jnp.inf
    lim_a = (8, 8, 5, 4)
    vals = [jnp.where(sub < lim_a[r1], t1[r1:r1 + 1, :] + t2[0:8, :], ninf) for r1 in range(4)]
    vals.append(t1[0:1, :] + t2[8:16, :])
    for r2 in range(3):
        ok = (sub >= 4) if r2 < 2 else (sub == 4)
        vals.append(jnp.where(ok, t1[0:8, :] + t2[r2:r2 + 1, :], ninf))
    vals.append(t1[8:16, :] + t2[0:1, :])
    return jnp.concatenate(vals, axis=0)


def _staircase_rows(sel):
    rows = []
    for r1 in range(4):
        l = jnp.sum(sel[8 * r1:8 * r1 + 8, :], axis=0, keepdims=True)
        if r1 == 0:
            l = l + jnp.sum(sel[32:40, :], axis=0, keepdims=True)
        rows.append(l)
    l_mid = sel[40:48, :] + sel[48:56, :] + sel[56:64, :]
    l_hi = sel[64:72, :]
    rows += [l_mid[r1:r1 + 1, :] for r1 in range(4, 8)]
    rows += [l_hi[r1 - 8:r1 - 7, :] for r1 in range(8, 16)]
    return rows


def _route_tile(sk_ref, qt_scr, s_scr, w_scr, rank_scr, t_scr, cand_scr, sel_scr, z_scr,
                a1_ref, c_ref, b2_ref, e2_ref):
    tm = qt_scr.shape[1]
    iota_f = lax.broadcasted_iota(jnp.int32, (N_KEYS, tm), 0).astype(F32)
    flat = _candidate_flat(tm)
    ninf = -jnp.inf
    for hp in range(N_HP):
        s_scr[hp] = jnp.dot(sk_ref[hp % 2], qt_scr[hp * N_KEYS:(hp + 1) * N_KEYS, :],
                            preferred_element_type=F32)

    def extract(exact):
        for hp in range(N_HP):
            w_scr[hp] = s_scr[hp]
            rank_scr[hp] = jnp.full((N_KEYS, tm), float(TOPK), F32)

        def round1(r, carry):
            rf = jnp.asarray(r, jnp.int32).astype(F32)
            for hp in range(N_HP):
                w = w_scr[hp]
                m = jnp.max(w, axis=0, keepdims=True)
                if exact:
                    idx = jnp.min(jnp.where(w == m, iota_f, float(N_KEYS)), axis=0, keepdims=True)
                    hit = iota_f == idx
                else:
                    hit = w == m
                rank_scr[hp] = jnp.where(hit, rf, rank_scr[hp])
                w_scr[hp] = jnp.where(hit, ninf, w)
                t_scr[hp, pl.ds(r, 1), :] = m
            return carry

        lax.fori_loop(0, TOPK, round1, 0)

        for h in range(P_HEADS):
            cand_scr[h] = _candidate_sums(t_scr[2 * h], t_scr[2 * h + 1])
            sel_scr[h] = jnp.zeros((72, tm), F32)

        def round2(r, carry):
            zs, b0s = carry
            zs_new, b0s_new = [], []
            for h in range(P_HEADS):
                w = cand_scr[h]
                m = jnp.max(w, axis=0, keepdims=True)
                if exact:
                    f = jnp.min(jnp.where(w == m, flat, 1e9), axis=0, keepdims=True)
                    hit = flat == f
                else:
                    hit = w == m
                sel_scr[h] = jnp.where(hit, 1.0, sel_scr[h])
                cand_scr[h] = jnp.where(hit, ninf, w)
                b0 = jnp.where(r == 0, m, b0s[h])
                zs_new.append(zs[h] + jnp.exp(m - b0))
                b0s_new.append(b0)
            return tuple(zs_new), tuple(b0s_new)

        zero_row = jnp.zeros((1, tm), F32)
        zs, _ = lax.fori_loop(0, TOPK, round2, ((zero_row,) * P_HEADS, (zero_row,) * P_HEADS))
        for h in range(P_HEADS):
            z_scr[h:h + 1, :] = zs[h]

    extract(False)
    excess = jnp.zeros((1, tm), F32)
    for hp in range(N_HP):
        cnt = jnp.sum(jnp.where(rank_scr[hp] < float(TOPK), 1.0, 0.0), axis=0, keepdims=True)
        excess = jnp.maximum(excess, jnp.abs(cnt - float(TOPK)))
    for h in range(P_HEADS):
        cnt = jnp.sum(sel_scr[h], axis=0, keepdims=True)
        excess = jnp.maximum(excess, jnp.abs(cnt - float(TOPK)))

    @pl.when(jnp.max(excess) > 0.0)
    def _():
        extract(True)

    for h in range(P_HEADS):
        l_rows = _staircase_rows(sel_scr[h])
        rank1 = rank_scr[2 * h]
        a1 = jnp.zeros((N_KEYS, tm), F32)
        for r1 in range(TOPK):
            a1 = jnp.where(rank1 == float(r1), l_rows[r1], a1)
        a1_ref[h] = a1
        c_ref[h] = jnp.exp(s_scr[2 * h] - t_scr[2 * h, 0:1, :]) * (1.0 / z_scr[h:h + 1, :])
        b2_ref[h] = rank_scr[2 * h + 1]
        e2_ref[h] = jnp.exp(s_scr[2 * h + 1] - t_scr[2 * h + 1, 0:1, :])


def _mixffn_kernel(x_ref, att_ref, rg_ref, mod_ref, wo_ref, nw_ref, wqt_ref, sk_ref,
                   x1_ref, h2t_ref, a1_ref, c_ref, b2_ref, e2_ref,
                   qt_scr, s_scr, w_scr, rank_scr, t_scr, cand_scr, sel_scr, z_scr):
    o = (jnp.dot(att_ref[...].astype(BF16), wo_ref[0:D_ATT, :], preferred_element_type=F32)
         + jnp.dot(rg_ref[...].astype(BF16), wo_ref[D_ATT:, :], preferred_element_type=F32))
    x1 = x_ref[...] + mod_ref[0, 2:3, :] * o
    x1_ref[...] = x1
    h2 = _rmsnorm(x1, nw_ref[...]) * (1.0 + mod_ref[0, 4:5, :]) + mod_ref[0, 3:4, :]
    h2t = h2.T.astype(BF16)
    h2t_ref[...] = h2t
    qt_scr[...] = jnp.dot(wqt_ref[...], h2t, preferred_element_type=F32).astype(BF16)
    _route_tile(sk_ref, qt_scr, s_scr, w_scr, rank_scr, t_scr, cand_scr, sel_scr, z_scr,
                a1_ref, c_ref, b2_ref, e2_ref)


def _mixffn_call(x2d, att, rg, mod3, mod_base, tiles_per_batch, w_out_bf, norm_w, w_qt_bf, sk_bf, tm):
    n = x2d.shape[0]
    tok = lambda t: (t, 0)
    route = pl.BlockSpec((P_HEADS, N_KEYS, tm), lambda t: (0, 0, t))
    route_shape = jax.ShapeDtypeStruct((P_HEADS, N_KEYS, n), F32)
    return pl.pallas_call(
        _mixffn_kernel,
        grid=(n // tm,),
        in_specs=[pl.BlockSpec((tm, D), tok),
                  pl.BlockSpec((tm, D_ATT), tok),
                  pl.BlockSpec((tm, D_RG), tok),
                  pl.BlockSpec((1, 6, D), lambda t: (mod_base + t // tiles_per_batch, 0, 0)),
                  pl.BlockSpec((D, D), lambda t: (0, 0)),
                  pl.BlockSpec((1, D), lambda t: (0, 0)),
                  pl.BlockSpec((2 * P_HEADS * N_KEYS, D), lambda t: (0, 0)),
                  pl.BlockSpec((2, N_KEYS, N_KEYS), lambda t: (0, 0, 0))],
        out_specs=[pl.BlockSpec((tm, D), tok),
                   pl.BlockSpec((D, tm), lambda t: (0, t)),
                   route, route, route, route],
        out_shape=[jax.ShapeDtypeStruct((n, D), F32), jax.ShapeDtypeStruct((D, n), BF16),
                   route_shape, route_shape, route_shape, route_shape],
        scratch_shapes=[pltpu.VMEM((2 * P_HEADS * N_KEYS, tm), BF16)]
                       + [pltpu.VMEM((N_HP, N_KEYS, tm), F32)] * 3
                       + [pltpu.VMEM((N_HP, TOPK, tm), F32)]
                       + [pltpu.VMEM((P_HEADS, 72, tm), F32)] * 2
                       + [pltpu.VMEM((P_HEADS, tm), F32)],
        compiler_params=_cparams("arbitrary"),
        name="mixffn",
    )(x2d, att, rg, mod3, w_out_bf, norm_w, w_qt_bf, sk_bf)


def _expert_prep_kernel(u_ref, v_ref, ub_ref, vt_ref):
    ub_ref[...] = u_ref[...].astype(BF16)
    vt_ref[...] = v_ref[...].T.astype(BF16)


def _expert_prep_call(u, v):
    te = 512
    return pl.pallas_call(
        _expert_prep_kernel,
        grid=(N_EXPERTS // te,),
        in_specs=[pl.BlockSpec((te, D), lambda j: (j, 0)), pl.BlockSpec((te, D), lambda j: (j, 0))],
        out_specs=[pl.BlockSpec((te, D), lambda j: (j, 0)), pl.BlockSpec((D, te), lambda j: (0, j))],
        out_shape=[jax.ShapeDtypeStruct((N_EXPERTS, D), BF16), jax.ShapeDtypeStruct((D, N_EXPERTS), BF16)],
        compiler_params=_cparams("arbitrary"),
        name="expert_prep",
    )(u, v)


def _peer_kernel(h2t_ref, u_ref, vt_ref, a1_ref, c_ref, b2_ref, e2_ref, x1_ref, mod_ref, nw_ref,
                 y_ref, acc, g_scr, b2_scr, e2_scr, *, i1_per_step):
    j = pl.program_id(1)
    tm = acc.shape[1]

    @pl.when(j == 0)
    def _():
        acc[...] = jnp.zeros_like(acc)
        b2_scr[...] = b2_ref[...].astype(BF16)
        e2_scr[...] = e2_ref[...].astype(BF16)

    grp = 4
    for i0 in range(0, i1_per_step, grp):
        for l in range(tm // 128):
            ls = slice(l * 128, (l + 1) * 128)
            g = [None] * grp
            for h in range(P_HEADS):
                b2 = b2_scr[h, :, ls]
                e2 = e2_scr[h, :, ls]
                for k in range(grp):
                    a1 = a1_ref[h, i0 + k:i0 + k + 1, ls].astype(BF16)
                    ch = (0.5 * c_ref[h, i0 + k:i0 + k + 1, ls]).astype(BF16)
                    w = jnp.where(b2 < a1, ch * e2, jnp.zeros_like(e2))
                    g[k] = w if g[k] is None else g[k] + w
            for k in range(grp):
                g_scr[(i0 + k) * N_KEYS:(i0 + k + 1) * N_KEYS, ls] = g[k]
    st = jnp.dot(u_ref[...], h2t_ref[...], preferred_element_type=F32)
    act = st * (1.0 + lax.erf(st * 0.7071067811865476))
    at = act.astype(BF16) * g_scr[...]
    acc[...] += jnp.dot(vt_ref[...], at, preferred_element_type=F32)

    @pl.when(j == pl.num_programs(1) - 1)
    def _():
        x2 = x1_ref[...] + mod_ref[0, 5:6, :] * acc[...].T
        y_ref[...] = _rmsnorm(x2, nw_ref[...])


def _peer_call(h2t, u_bf, vt_bf, a1, c, b2, e2, x1, mod3, mod_base, tiles_per_batch, norm_f_w, tm, te):
    n = x1.shape[0]
    ips = te // N_KEYS
    route_lo = pl.BlockSpec((P_HEADS, ips, tm), lambda t, j: (0, j, t))
    route_full = pl.BlockSpec((P_HEADS, N_KEYS, tm), lambda t, j: (0, 0, t))
    return pl.pallas_call(
        functools.partial(_peer_kernel, i1_per_step=ips),
        grid=(n // tm, N_EXPERTS // te),
        in_specs=[pl.BlockSpec((D, tm), lambda t, j: (0, t)),
                  pl.BlockSpec((te, D), lambda t, j: (j, 0)),
                  pl.BlockSpec((D, te), lambda t, j: (0, j)),
                  route_lo, route_lo, route_full, route_full,
                  pl.BlockSpec((tm, D), lambda t, j: (t, 0)),
                  pl.BlockSpec((1, 6, D), lambda t, j: (mod_base + t // tiles_per_batch, 0, 0)),
                  pl.BlockSpec((1, D), lambda t, j: (0, 0))],
        out_specs=pl.BlockSpec((tm, D), lambda t, j: (t, 0)),
        out_shape=jax.ShapeDtypeStruct((n, D), F32),
        scratch_shapes=[pltpu.VMEM((D, tm), F32), pltpu.VMEM((te, tm), BF16),
                        pltpu.VMEM((P_HEADS, N_KEYS, tm), BF16), pltpu.VMEM((P_HEADS, N_KEYS, tm), BF16)],
        compiler_params=_cparams("arbitrary", "arbitrary"),
        name="peer",
    )(h2t, u_bf, vt_bf, a1, c, b2, e2, x1, mod3, norm_f_w)


def _path(x, mod3, mod_base, per_batch_mod, weights, attn_fn, h0, emit_cache):
    (norm_mix_w, w_in_bf, rg_params, w_out_bf, norm_ffn_w, w_qt_bf, sk_bf, u_bf, vt_bf, norm_f_w) = weights
    bsz, seq, _ = x.shape
    n = bsz * seq
    x2d = x.reshape(n, D)
    tpb = lambda tm: (seq // tm) if per_batch_mod else n
    q, kb, vb, xr, yr, *kv_f32 = _inproj_call(x2d, mod3, mod_base, tpb(512), norm_mix_w, w_in_bf, emit_cache)
    att = attn_fn(q, kb, vb)
    rg_out, h_last = _rglru_call(xr, yr, h0, *rg_params, seq)
    tm_mix = 256
    x1, h2t, a1, c, b2, e2 = _mixffn_call(x2d, att, rg_out, mod3, mod_base, tpb(tm_mix), w_out_bf,
                                          norm_ffn_w, w_qt_bf, sk_bf, tm_mix)
    tm_peer = 512
    y = _peer_call(h2t, u_bf, vt_bf, a1, c, b2, e2, x1, mod3, mod_base, tpb(tm_peer), norm_f_w,
                   tm_peer, 1024)
    return y.reshape(bsz, seq, D), kv_f32, h_last


def kernel(x_prompt, x_sample, c, cache_k, cache_v, state_rglru, c_ctx, w_mod, b_mod, norm_mix_w, w_in,
           rpb, conv_w, conv_b, rg_w_a, rg_b_a, rg_w_i, rg_b_i, rg_lambda, w_out, norm_ffn_w,
           peer_w_q, peer_sub_keys, peer_u, peer_v, norm_f_w):
    depth = w_mod.shape[0]
    assert depth == 1, "single-layer problem"
    l = 0
    bp, sp, _ = x_prompt.shape
    bs, ss, _ = x_sample.shape

    n_rows = 16
    cvec = jnp.concatenate([c_ctx[None, :], c, jnp.zeros((n_rows - 1 - bs, D), F32)], axis=0)
    mod3 = _mod_call(cvec, w_mod[l], b_mod[l][None, :]).reshape(n_rows, 6, D)

    wbd = jnp.stack([_block_diag_halves(rg_w_a[l]), _block_diag_halves(rg_w_i[l])], axis=1).astype(BF16)
    rg_params = (conv_w[l], conv_b[l][None, :], wbd, rg_b_a[l], rg_b_i[l], rg_lambda[l])
    u_bf, vt_bf = _expert_prep_call(peer_u[l], peer_v[l])
    weights = (norm_mix_w[l][None, :], w_in[l].astype(BF16), rg_params, w_out[l].astype(BF16),
               norm_ffn_w[l][None, :], peer_w_q[l].T.astype(BF16), peer_sub_keys[l].astype(BF16),
               u_bf, vt_bf, norm_f_w[None, :])

    ctx_attn = lambda q, kb, vb: _ctx_attn_call(q, kb, vb, sp)
    h0_p = jnp.zeros((bp, 2, D_RG), F32)
    y_prompt, (k_p, v_p), h_last = _path(x_prompt, mod3, 0, False, weights, ctx_attn, h0_p, True)

    tz = _bias_toeplitz(rpb[l])
    ck = cache_k[:, l].reshape(-1, D_ATT).astype(BF16)
    cv = cache_v[:, l].reshape(-1, D_ATT).astype(BF16)
    nbr_attn = lambda q, kb, vb: _nbr_attn_call(q, kb, vb, ck, cv, tz, bs, ss)
    y_sample, _, _ = _path(x_sample, mod3, 1, True, weights, nbr_attn, state_rglru[:, l], False)

    new_k = k_p.reshape(bp, 1, sp, N_HEADS, HEAD_DIM)
    new_v = v_p.reshape(bp, 1, sp, N_HEADS, HEAD_DIM)
    new_h = h_last.reshape(bp, 1, 2, D_RG)
    return (y_prompt, y_sample, new_k, new_v, new_h)
```

```python
import functools

import numpy as np
import jax
import jax.numpy as jnp
from jax import lax
from jax.experimental import pallas as pl
from jax.experimental.pallas import tpu as pltpu

F32 = jnp.float32
BF16 = jnp.bfloat16

D = 1024
D_ATT = 512
N_HEADS = 8
HEAD_DIM = 64
GRID_W = 64
WIN_H = 8
WIN_W = 16
D_RG = 512
D_IN = 3 * D_ATT + 2 * D_RG
N_KEYS = 128
P_HEADS = 8
TOPK = 16
N_EXPERTS = N_KEYS * N_KEYS
RMS_EPS = 1e-6
NEG_INF = -1e30
RG_C = 8.0
ATT_SCALE = HEAD_DIM ** -0.5

VMEM_LIMIT = 56 * 1024 * 1024

_NT = (((1,), (1,)), ((), ()))


def _cparams(*sem, flags=None):
    return pltpu.CompilerParams(dimension_semantics=sem, vmem_limit_bytes=VMEM_LIMIT, flags=flags)


def _rmsnorm(x, w):
    ms = jnp.mean(x * x, axis=-1, keepdims=True)
    return x * lax.rsqrt(ms + RMS_EPS) * w


def _mod_kernel(c_ref, w_ref, b_ref, o_ref):
    cv = c_ref[...]
    s = cv * jax.nn.sigmoid(cv)
    o_ref[...] = jnp.dot(s.astype(BF16), w_ref[...].astype(BF16),
                         preferred_element_type=F32) + b_ref[...]


def _mod_call(cvec, w_mod, b_mod):
    rows = cvec.shape[0]
    tn = 1536
    return pl.pallas_call(
        _mod_kernel,
        grid=(6 * D // tn,),
        in_specs=[pl.BlockSpec((rows, D), lambda j: (0, 0)),
                  pl.BlockSpec((D, tn), lambda j: (0, j)),
                  pl.BlockSpec((1, tn), lambda j: (0, j))],
        out_specs=pl.BlockSpec((rows, tn), lambda j: (0, j)),
        out_shape=jax.ShapeDtypeStruct((rows, 6 * D), F32),
        compiler_params=_cparams("arbitrary"),
        name="mod",
    )(cvec, w_mod, b_mod)


def _inproj_kernel(x_ref, mod_ref, nw_ref, w_ref, *out_refs, emit_f32_kv):
    if emit_f32_kv:
        q_ref, kb_ref, vb_ref, xr_ref, yr_ref, kf_ref, vf_ref = out_refs
    else:
        q_ref, kb_ref, vb_ref, xr_ref, yr_ref = out_refs
    x = x_ref[...]
    h = _rmsnorm(x, nw_ref[...]) * (1.0 + mod_ref[0, 1:2, :]) + mod_ref[0, 0:1, :]
    p = jnp.dot(h.astype(BF16), w_ref[...], preferred_element_type=F32)
    q_ref[...] = (p[:, 0:D_ATT] * ATT_SCALE).astype(BF16)
    k = p[:, D_ATT:2 * D_ATT]
    v = p[:, 2 * D_ATT:3 * D_ATT]
    kb_ref[...] = k.astype(BF16)
    vb_ref[...] = v.astype(BF16)
    xr_ref[...] = p[:, 3 * D_ATT:3 * D_ATT + D_RG]
    yr_ref[...] = p[:, 3 * D_ATT + D_RG:]
    if emit_f32_kv:
        kf_ref[...] = k
        vf_ref[...] = v


def _inproj_call(x2d, mod3, mod_base, tiles_per_batch, norm_w, w_in_bf, emit_f32_kv):
    n = x2d.shape[0]
    tm = 512
    row = lambda t: (mod_base + t // tiles_per_batch, 0, 0)
    tok = lambda t: (t, 0)
    shp = lambda w, dt: jax.ShapeDtypeStruct((n, w), dt)
    out_specs = [pl.BlockSpec((tm, D_ATT), tok)] * 3 + [pl.BlockSpec((tm, D_RG), tok)] * 2
    out_shape = [shp(D_ATT, BF16)] * 3 + [shp(D_RG, F32)] * 2
    if emit_f32_kv:
        out_specs += [pl.BlockSpec((tm, D_ATT), tok)] * 2
        out_shape += [shp(D_ATT, F32)] * 2
    return pl.pallas_call(
        functools.partial(_inproj_kernel, emit_f32_kv=emit_f32_kv),
        grid=(n // tm,),
        in_specs=[pl.BlockSpec((tm, D), tok),
                  pl.BlockSpec((1, 6, D), row),
                  pl.BlockSpec((1, D), lambda t: (0, 0)),
                  pl.BlockSpec((D, D_IN), lambda t: (0, 0))],
        out_specs=out_specs,
        out_shape=out_shape,
        compiler_params=_cparams("arbitrary"),
        name="inproj",
    )(x2d, mod3, norm_w, w_in_bf)


def _ctx_attn_kernel(q_ref, k_ref, v_ref, o_ref):
    for h in range(N_HEADS):
        sl = slice(h * HEAD_DIM, (h + 1) * HEAD_DIM)
        s = lax.dot_general(q_ref[:, sl], k_ref[:, sl], _NT, preferred_element_type=F32)
        m = jnp.max(s, axis=-1, keepdims=True)
        p = jnp.exp(s - m)
        l = jnp.sum(p, axis=-1, keepdims=True)
        o = jnp.dot(p.astype(BF16), v_ref[:, sl], preferred_element_type=F32)
        o_ref[:, sl] = o / l


def _ctx_attn_call(q, kb, vb, seq):
    n = q.shape[0]
    blk = pl.BlockSpec((seq, D_ATT), lambda b: (b, 0))
    return pl.pallas_call(
        _ctx_attn_kernel,
        grid=(n // seq,),
        in_specs=[blk, blk, blk],
        out_specs=blk,
        out_shape=jax.ShapeDtypeStruct((n, D_ATT), F32),
        compiler_params=_cparams("arbitrary"),
        name="ctx_attn",
    )(q, kb, vb)


ROWS_PER_STEP = 4


def _nbr_attn_kernel(q_ref, k_ref, v_ref, ck_ref, cv_ref, tz_ref, o_ref, *, rows):
    g = pl.program_id(0)
    n_loc = WIN_H * GRID_W
    outs = [[None] * N_HEADS for _ in range(ROWS_PER_STEP)]
    for h in range(N_HEADS):
        sl = slice(h * HEAD_DIM, (h + 1) * HEAD_DIM)
        ckh = ck_ref[:, sl]
        cvh = cv_ref[:, sl]
        qg = q_ref[:, sl]
        s_ctx_all = lax.dot_general(qg, ckh, _NT, preferred_element_type=F32)
        for i in range(ROWS_PER_STEP):
            r = g * ROWS_PER_STEP + i
            start = jnp.clip(r - WIN_H // 2, 0, rows - WIN_H)
            row0 = pl.multiple_of(start * GRID_W, GRID_W)
            d0 = start - r + (WIN_H - 1)
            bias = jnp.concatenate([tz_ref[h, d0 + 2 * p] for p in range(WIN_H // 2)], axis=-1)
            qs = slice(i * GRID_W, (i + 1) * GRID_W)
            s_loc = lax.dot_general(qg[qs], k_ref[pl.ds(row0, n_loc), sl], _NT,
                                    preferred_element_type=F32) + bias
            s_ctx = s_ctx_all[qs]
            m = jnp.maximum(jnp.max(s_loc, axis=-1, keepdims=True), jnp.max(s_ctx, axis=-1, keepdims=True))
            p_loc = jnp.exp(s_loc - m)
            p_ctx = jnp.exp(s_ctx - m)
            l = jnp.sum(p_loc, axis=-1, keepdims=True) + jnp.sum(p_ctx, axis=-1, keepdims=True)
            o = (jnp.dot(p_loc.astype(BF16), v_ref[pl.ds(row0, n_loc), sl], preferred_element_type=F32)
                 + jnp.dot(p_ctx.astype(BF16), cvh, preferred_element_type=F32))
            outs[i][h] = o / l
    for i in range(ROWS_PER_STEP):
        o_ref[i * GRID_W:(i + 1) * GRID_W, :] = jnp.concatenate(outs[i], axis=-1)


def _nbr_attn_call(q, kb, vb, ck, cv, tz, batch, seq):
    rows = seq // GRID_W
    groups = rows // ROWS_PER_STEP
    past = ck.shape[0] // batch
    n = q.shape[0]
    qblk = pl.BlockSpec((GRID_W * ROWS_PER_STEP, D_ATT), lambda g, b: (b * groups + g, 0))
    kvblk = pl.BlockSpec((seq, D_ATT), lambda g, b: (b, 0))
    cblk = pl.BlockSpec((past, D_ATT), lambda g, b: (b, 0))
    return pl.pallas_call(
        functools.partial(_nbr_attn_kernel, rows=rows),
        grid=(groups, batch),
        in_specs=[qblk, kvblk, kvblk, cblk, cblk,
                  pl.BlockSpec(tz.shape, lambda g, b: (0, 0, 0, 0))],
        out_specs=qblk,
        out_shape=jax.ShapeDtypeStruct((n, D_ATT), F32),
        compiler_params=_cparams("arbitrary", "arbitrary"),
        name="nbr_attn",
    )(q, kb, vb, ck, cv, tz)


def _bias_toeplitz(rpb_l):
    qc = np.arange(GRID_W)
    off = qc[None, :] - qc[:, None] + (WIN_W - 1)
    place = (off[None, :, :] == np.arange(2 * WIN_W - 1)[:, None, None]).astype(np.float32)
    tz = jnp.einsum('hdx,xqk->hdqk', rpb_l, jnp.asarray(place), precision=lax.Precision.HIGHEST)
    cs = np.clip(qc - WIN_W // 2, 0, GRID_W - WIN_W)
    valid = (qc[None, :] >= cs[:, None]) & (qc[None, :] < cs[:, None] + WIN_W)
    tz = jnp.where(jnp.asarray(valid)[None, None], tz, NEG_INF)
    return jnp.concatenate([tz[:, :-1], tz[:, 1:]], axis=-1)


def _log_sigmoid(x):
    return jnp.minimum(x, 0.0) - jnp.log(1.0 + jnp.exp(-jnp.abs(x)))


def _gelu_tanh(x):
    return 0.5 * x * (1.0 + jnp.tanh(0.7978845608028654 * (x + 0.044715 * x * x * x)))


def _rglru_kernel(xr_ref, yr_ref, h0_ref, cw_ref, cb_ref, wbd_ref, ba_ref, bi_ref, lam_ref,
                  y_ref, hl_ref, xpad, a_f, b_f, a_b, b_b):
    T = xr_ref.shape[0]
    CH = 256
    xpad[0:8, :] = jnp.zeros((8, D_RG), F32)
    xpad[T + 8:T + 16, :] = jnp.zeros((8, D_RG), F32)
    xpad[8:T + 8, :] = xr_ref[...]
    a_refs = (a_f, a_b)
    b_refs = (b_f, b_b)
    for c0 in range(0, T, CH):
        xc = cb_ref[...] + cw_ref[0:1, :] * xpad[c0 + 6:c0 + 6 + CH, :]
        for i in range(1, 4):
            xc = xc + cw_ref[i:i + 1, :] * xpad[c0 + 6 + i:c0 + 6 + i + CH, :]
        xcb = xc.astype(BF16)
        for d in range(2):
            gates = []
            for g in range(2):
                halves = [jnp.dot(xcb[:, hf * 256:(hf + 1) * 256], wbd_ref[d, g, hf],
                                  preferred_element_type=F32) for hf in range(2)]
                gates.append(jnp.concatenate(halves, axis=-1))
            rg = jax.nn.sigmoid(gates[0] + ba_ref[d:d + 1, :])
            ig = jax.nn.sigmoid(gates[1] + bi_ref[d:d + 1, :])
            log_a = RG_C * rg * _log_sigmoid(lam_ref[d:d + 1, :])
            a = jnp.exp(log_a)
            b = jnp.sqrt(1.0 - a * a) * (ig * xc)
            a_refs[d][c0:c0 + CH, :] = a
            b_refs[d][c0:c0 + CH, :] = b

    sub = lax.broadcasted_iota(jnp.int32, (8, D_RG), 0)
    n_tiles = T // 8

    def scan_tile(a_ref, b_ref, t8, h, order):
        base = pl.multiple_of(t8 * 8, 8)
        a_blk = a_ref[pl.ds(base, 8), :]
        b_blk = b_ref[pl.ds(base, 8), :]
        out = b_blk
        for j in order:
            cand = a_blk * h + b_blk
            h = jnp.broadcast_to(cand[j:j + 1, :], (8, D_RG))
            out = jnp.where(sub == j, cand, out)
        b_ref[pl.ds(base, 8), :] = out
        return h

    def body(i, carry):
        hf, hb = carry
        hf = scan_tile(a_f, b_f, i, hf, range(8))
        hb = scan_tile(a_b, b_b, n_tiles - 1 - i, hb, range(7, -1, -1))
        return hf, hb

    hf0 = jnp.broadcast_to(h0_ref[0, 0:1, :], (8, D_RG))
    hb0 = jnp.broadcast_to(h0_ref[0, 1:2, :], (8, D_RG))
    hf, hb = lax.fori_loop(0, n_tiles, body, (hf0, hb0))
    hl_ref[0, 0:1, :] = hf[0:1, :]
    hl_ref[0, 1:2, :] = hb[0:1, :]
    for c0 in range(0, T, CH):
        sl = slice(c0, c0 + CH)
        y_ref[sl, :] = (b_f[sl, :] + b_b[sl, :]) * _gelu_tanh(yr_ref[sl, :])


def _rglru_call(xr, yr, h0, conv_w, conv_b, wbd, b_a, b_i, lam, seq):
    n = xr.shape[0]
    batch = n // seq
    tok = pl.BlockSpec((seq, D_RG), lambda b: (b, 0))
    full = lambda shape: pl.BlockSpec(shape, lambda b: (0,) * len(shape))
    return pl.pallas_call(
        _rglru_kernel,
        grid=(batch,),
        in_specs=[tok, tok, pl.BlockSpec((1, 2, D_RG), lambda b: (b, 0, 0)),
                  full((4, D_RG)), full((1, D_RG)), full((2, 2, 2, 256, 256)),
                  full((2, D_RG)), full((2, D_RG)), full((2, D_RG))],
        out_specs=[tok, pl.BlockSpec((1, 2, D_RG), lambda b: (b, 0, 0))],
        out_shape=[jax.ShapeDtypeStruct((n, D_RG), F32), jax.ShapeDtypeStruct((batch, 2, D_RG), F32)],
        scratch_shapes=[pltpu.VMEM((seq + 16, D_RG), F32)] + [pltpu.VMEM((seq, D_RG), F32)] * 4,
        compiler_params=_cparams("arbitrary"),
        name="rglru",
    )(xr, yr, h0, conv_w, conv_b, wbd, b_a, b_i, lam)


def _block_diag_halves(w):
    d = w.shape[0]
    w4 = w.reshape(d, 2, 4, 64, 64)
    eye = jnp.eye(4, dtype=w.dtype)
    out = jnp.einsum('dhncf,nm->dhncmf', w4, eye)
    return out.reshape(d, 2, 256, 256)


N_HP = 2 * P_HEADS


def _candidate_flat(tm):
    subf = lax.broadcasted_iota(jnp.int32, (8, tm), 0).astype(F32)
    flat = [subf + float(r1 * TOPK) for r1 in range(4)]
    flat.append(subf + 8.0)
    flat += [subf * float(TOPK) + float(r2) for r2 in range(3)]
    flat.append((subf + 8.0) * float(TOPK))
    return jnp.concatenate(flat, axis=0)


def _candidate_sums(t1, t2):
    tm = t1.shape[1]
    sub = lax.broadcasted_iota(jnp.int32, (8, tm), 0)
    ninf = -jnp.inf
    lim_a = (8, 8, 5, 4)
    vals = [jnp.where(sub < lim_a[r1], t1[r1:r1 + 1, :] + t2[0:8, :], ninf) for r1 in range(4)]
    vals.append(t1[0:1, :] + t2[8:16, :])
    for r2 in range(3):
        ok = (sub >= 4) if r2 < 2 else (sub == 4)
        vals.append(jnp.where(ok, t1[0:8, :] + t2[r2:r2 + 1, :], ninf))
    vals.append(t1[8:16, :] + t2[0:1, :])
    return jnp.concatenate(vals, axis=0)


def _staircase_rows(sel):
    rows = []
    for r1 in range(4):
        l = jnp.sum(sel[8 * r1:8 * r1 + 8, :], axis=0, keepdims=True)
        if r1 == 0:
            l = l + jnp.sum(sel[32:40, :], axis=0, keepdims=True)
        rows.append(l)
    l_mid = sel[40:48, :] + sel[48:56, :] + sel[56:64, :]
    l_hi = sel[64:72, :]
    rows += [l_mid[r1:r1 + 1, :] for r1 in range(4, 8)]
    rows += [l_hi[r1 - 8:r1 - 7, :] for r1 in range(8, 16)]
    return rows


def _route_tile(sk_ref, qt_scr, s_scr, w_scr, rank_scr, t_scr, cand_scr, sel_scr, z_scr,
                a1_ref, c_ref, b2_ref, e2_ref):
    tm = qt_scr.shape[1]
    iota_f = lax.broadcasted_iota(jnp.int32, (N_KEYS, tm), 0).astype(F32)
    flat = _candidate_flat(tm)
    ninf = -jnp.inf
    for hp in range(N_HP):
        s_scr[hp] = jnp.dot(sk_ref[hp % 2], qt_scr[hp * N_KEYS:(hp + 1) * N_KEYS, :],
                            preferred_element_type=F32)

    def extract(exact):
        for hp in range(N_HP):
            w_scr[hp] = s_scr[hp]
            rank_scr[hp] = jnp.full((N_KEYS, tm), float(TOPK), F32)

        def round1(r, carry):
            rf = jnp.asarray(r, jnp.int32).astype(F32)
            for hp in range(N_HP):
                w = w_scr[hp]
                m = jnp.max(w, axis=0, keepdims=True)
                if exact:
                    idx = jnp.min(jnp.where(w == m, iota_f, float(N_KEYS)), axis=0, keepdims=True)
                    hit = iota_f == idx
                else:
                    hit = w == m
                rank_scr[hp] = jnp.where(hit, rf, rank_scr[hp])
                w_scr[hp] = jnp.where(hit, ninf, w)
                t_scr[hp, pl.ds(r, 1), :] = m
            return carry

        lax.fori_loop(0, TOPK, round1, 0)

        for h in range(P_HEADS):
            cand_scr[h] = _candidate_sums(t_scr[2 * h], t_scr[2 * h + 1])
            sel_scr[h] = jnp.zeros((72, tm), F32)

        def round2(r, carry):
            zs, b0s = carry
            zs_new, b0s_new = [], []
            for h in range(P_HEADS):
                w = cand_scr[h]
                m = jnp.max(w, axis=0, keepdims=True)
                if exact:
                    f = jnp.min(jnp.where(w == m, flat, 1e9), axis=0, keepdims=True)
                    hit = flat == f
                else:
                    hit = w == m
                sel_scr[h] = jnp.where(hit, 1.0, sel_scr[h])
                cand_scr[h] = jnp.where(hit, ninf, w)
                b0 = jnp.where(r == 0, m, b0s[h])
                zs_new.append(zs[h] + jnp.exp(m - b0))
                b0s_new.append(b0)
            return tuple(zs_new), tuple(b0s_new)

        zero_row = jnp.zeros((1, tm), F32)
        zs, _ = lax.fori_loop(0, TOPK, round2, ((zero_row,) * P_HEADS, (zero_row,) * P_HEADS))
        for h in range(P_HEADS):
            z_scr[h:h + 1, :] = zs[h]

    extract(False)
    excess = jnp.zeros((1, tm), F32)
    for hp in range(N_HP):
        cnt = jnp.sum(jnp.where(rank_scr[hp] < float(TOPK), 1.0, 0.0), axis=0, keepdims=True)
        excess = jnp.maximum(excess, jnp.abs(cnt - float(TOPK)))
    for h in range(P_HEADS):
        cnt = jnp.sum(sel_scr[h], axis=0, keepdims=True)
        excess = jnp.maximum(excess, jnp.abs(cnt - float(TOPK)))

    @pl.when(jnp.max(excess) > 0.0)
    def _():
        extract(True)

    for h in range(P_HEADS):
        l_rows = _staircase_rows(sel_scr[h])
        rank1 = rank_scr[2 * h]
        a1 = jnp.zeros((N_KEYS, tm), F32)
        for r1 in range(TOPK):
            a1 = jnp.where(rank1 == float(r1), l_rows[r1], a1)
        a1_ref[h] = a1
        c_ref[h] = jnp.exp(s_scr[2 * h] - t_scr[2 * h, 0:1, :]) * (0.5 / z_scr[h:h + 1, :])
        b2_ref[h] = rank_scr[2 * h + 1].astype(BF16)
        e2_ref[h] = jnp.exp(s_scr[2 * h + 1] - t_scr[2 * h + 1, 0:1, :]).astype(BF16)


def _mixffn_kernel(x_ref, att_ref, rg_ref, mod_ref, wo_ref, nw_ref, wqt_ref, sk_ref,
                   x1_ref, h2t_ref, a1_ref, c_ref, b2_ref, e2_ref,
                   qt_scr, s_scr, w_scr, rank_scr, t_scr, cand_scr, sel_scr, z_scr):
    o = (jnp.dot(att_ref[...].astype(BF16), wo_ref[0:D_ATT, :], preferred_element_type=F32)
         + jnp.dot(rg_ref[...].astype(BF16), wo_ref[D_ATT:, :], preferred_element_type=F32))
    x1 = x_ref[...] + mod_ref[0, 2:3, :] * o
    x1_ref[...] = x1
    h2 = _rmsnorm(x1, nw_ref[...]) * (1.0 + mod_ref[0, 4:5, :]) + mod_ref[0, 3:4, :]
    h2t = h2.T.astype(BF16)
    h2t_ref[...] = h2t
    qt_scr[...] = jnp.dot(wqt_ref[...], h2t, preferred_element_type=F32).astype(BF16)
    _route_tile(sk_ref, qt_scr, s_scr, w_scr, rank_scr, t_scr, cand_scr, sel_scr, z_scr,
                a1_ref, c_ref, b2_ref, e2_ref)


def _mixffn_call(x2d, att, rg, mod3, mod_base, tiles_per_batch, w_out_bf, norm_w, w_qt_bf, sk_bf, tm):
    n = x2d.shape[0]
    tok = lambda t: (t, 0)
    route = pl.BlockSpec((P_HEADS, N_KEYS, tm), lambda t: (0, 0, t))
    route_shape = jax.ShapeDtypeStruct((P_HEADS, N_KEYS, n), F32)
    return pl.pallas_call(
        _mixffn_kernel,
        grid=(n // tm,),
        in_specs=[pl.BlockSpec((tm, D), tok),
                  pl.BlockSpec((tm, D_ATT), tok),
                  pl.BlockSpec((tm, D_RG), tok),
                  pl.BlockSpec((1, 6, D), lambda t: (mod_base + t // tiles_per_batch, 0, 0)),
                  pl.BlockSpec((D, D), lambda t: (0, 0)),
                  pl.BlockSpec((1, D), lambda t: (0, 0)),
                  pl.BlockSpec((2 * P_HEADS * N_KEYS, D), lambda t: (0, 0)),
                  pl.BlockSpec((2, N_KEYS, N_KEYS), lambda t: (0, 0, 0))],
        out_specs=[pl.BlockSpec((tm, D), tok),
                   pl.BlockSpec((D, tm), lambda t: (0, t)),
                   route, route, route, route],
        out_shape=[jax.ShapeDtypeStruct((n, D), F32), jax.ShapeDtypeStruct((D, n), BF16),
                   route_shape, route_shape,
                   jax.ShapeDtypeStruct((P_HEADS, N_KEYS, n), BF16), jax.ShapeDtypeStruct((P_HEADS, N_KEYS, n), BF16)],
        scratch_shapes=[pltpu.VMEM((2 * P_HEADS * N_KEYS, tm), BF16)]
                       + [pltpu.VMEM((N_HP, N_KEYS, tm), F32)] * 3
                       + [pltpu.VMEM((N_HP, TOPK, tm), F32)]
                       + [pltpu.VMEM((P_HEADS, 72, tm), F32)] * 2
                       + [pltpu.VMEM((P_HEADS, tm), F32)],
        compiler_params=_cparams("arbitrary"),
        name="mixffn",
    )(x2d, att, rg, mod3, w_out_bf, norm_w, w_qt_bf, sk_bf)


def _expert_prep_kernel(u_ref, v_ref, ub_ref, vt_ref):
    ub_ref[...] = u_ref[...].astype(BF16)
    vt_ref[...] = v_ref[...].T.astype(BF16)


def _expert_prep_call(u, v):
    te = 512
    return pl.pallas_call(
        _expert_prep_kernel,
        grid=(N_EXPERTS // te,),
        in_specs=[pl.BlockSpec((te, D), lambda j: (j, 0)), pl.BlockSpec((te, D), lambda j: (j, 0))],
        out_specs=[pl.BlockSpec((te, D), lambda j: (j, 0)), pl.BlockSpec((D, te), lambda j: (0, j))],
        out_shape=[jax.ShapeDtypeStruct((N_EXPERTS, D), BF16), jax.ShapeDtypeStruct((D, N_EXPERTS), BF16)],
        compiler_params=_cparams("arbitrary"),
        name="expert_prep",
    )(u, v)


def _peer_kernel(h2t_ref, u_ref, vt_ref, a1_ref, c_ref, b2_ref, e2_ref, x1_ref, mod_ref, nw_ref,
                 y_ref, acc, g_scr, *, i1_per_step):
    j = pl.program_id(1)
    tm = acc.shape[1]

    @pl.when(j == 0)
    def _():
        acc[...] = jnp.zeros_like(acc)

    grp = 4
    one = jnp.ones((), BF16)
    zero = jnp.zeros((), BF16)
    for i0 in range(0, i1_per_step, grp):
        for l in range(tm // 128):
            ls = slice(l * 128, (l + 1) * 128)
            g = [None] * grp
            for h in range(P_HEADS):
                b2 = b2_ref[h, :, ls]
                e2 = e2_ref[h, :, ls]
                for k in range(grp):
                    a1 = a1_ref[h, i0 + k:i0 + k + 1, ls].astype(BF16)
                    ch = c_ref[h, i0 + k:i0 + k + 1, ls].astype(BF16)
                    m = jnp.minimum(jnp.maximum(a1 - b2, zero), one)
                    w = (ch * e2) * m
                    g[k] = w if g[k] is None else g[k] + w
            for k in range(grp):
                g_scr[(i0 + k) * N_KEYS:(i0 + k + 1) * N_KEYS, ls] = g[k]
    st = jnp.dot(u_ref[...], h2t_ref[...], preferred_element_type=F32)
    act = st * (1.0 + lax.erf(st * 0.7071067811865476))
    at = act.astype(BF16) * g_scr[...]
    acc[...] += jnp.dot(vt_ref[...], at, preferred_element_type=F32)

    @pl.when(j == pl.num_programs(1) - 1)
    def _():
        x2 = x1_ref[...] + mod_ref[0, 5:6, :] * acc[...].T
        y_ref[...] = _rmsnorm(x2, nw_ref[...])


def _peer_call(h2t, u_bf, vt_bf, a1, c, b2, e2, x1, mod3, mod_base, tiles_per_batch, norm_f_w, tm, te):
    n = x1.shape[0]
    ips = te // N_KEYS
    route_lo = pl.BlockSpec((P_HEADS, ips, tm), lambda t, j: (0, j, t))
    route_full = pl.BlockSpec((P_HEADS, N_KEYS, tm), lambda t, j: (0, 0, t))
    return pl.pallas_call(
        functools.partial(_peer_kernel, i1_per_step=ips),
        grid=(n // tm, N_EXPERTS // te),
        in_specs=[pl.BlockSpec((D, tm), lambda t, j: (0, t)),
                  pl.BlockSpec((te, D), lambda t, j: (j, 0)),
                  pl.BlockSpec((D, te), lambda t, j: (0, j)),
                  route_lo, route_lo, route_full, route_full,
                  pl.BlockSpec((tm, D), lambda t, j: (t, 0)),
                  pl.BlockSpec((1, 6, D), lambda t, j: (mod_base + t // tiles_per_batch, 0, 0)),
                  pl.BlockSpec((1, D), lambda t, j: (0, 0))],
        out_specs=pl.BlockSpec((tm, D), lambda t, j: (t, 0)),
        out_shape=jax.ShapeDtypeStruct((n, D), F32),
        scratch_shapes=[pltpu.VMEM((D, tm), F32), pltpu.VMEM((te, tm), BF16)],
        compiler_params=_cparams("arbitrary", "arbitrary"),
        name="peer",
    )(h2t, u_bf, vt_bf, a1, c, b2, e2, x1, mod3, norm_f_w)


def _path(x, mod3, mod_base, per_batch_mod, weights, attn_fn, h0, emit_cache):
    (norm_mix_w, w_in_bf, rg_params, w_out_bf, norm_ffn_w, w_qt_bf, sk_bf, u_bf, vt_bf, norm_f_w) = weights
    bsz, seq, _ = x.shape
    n = bsz * seq
    x2d = x.reshape(n, D)
    tpb = lambda tm: (seq // tm) if per_batch_mod else n
    q, kb, vb, xr, yr, *kv_f32 = _inproj_call(x2d, mod3, mod_base, tpb(512), norm_mix_w, w_in_bf, emit_cache)
    att = attn_fn(q, kb, vb)
    rg_out, h_last = _rglru_call(xr, yr, h0, *rg_params, seq)
    tm_mix = 256
    x1, h2t, a1, c, b2, e2 = _mixffn_call(x2d, att, rg_out, mod3, mod_base, tpb(tm_mix), w_out_bf,
                                          norm_ffn_w, w_qt_bf, sk_bf, tm_mix)
    tm_peer = 512
    y = _peer_call(h2t, u_bf, vt_bf, a1, c, b2, e2, x1, mod3, mod_base, tpb(tm_peer), norm_f_w,
                   tm_peer, 1024)
    return y.reshape(bsz, seq, D), kv_f32, h_last


def kernel(x_prompt, x_sample, c, cache_k, cache_v, state_rglru, c_ctx, w_mod, b_mod, norm_mix_w, w_in,
           rpb, conv_w, conv_b, rg_w_a, rg_b_a, rg_w_i, rg_b_i, rg_lambda, w_out, norm_ffn_w,
           peer_w_q, peer_sub_keys, peer_u, peer_v, norm_f_w):
    depth = w_mod.shape[0]
    assert depth == 1, "single-layer problem"
    l = 0
    bp, sp, _ = x_prompt.shape
    bs, ss, _ = x_sample.shape

    n_rows = 16
    cvec = jnp.concatenate([c_ctx[None, :], c, jnp.zeros((n_rows - 1 - bs, D), F32)], axis=0)
    mod3 = _mod_call(cvec, w_mod[l], b_mod[l][None, :]).reshape(n_rows, 6, D)

    wbd = jnp.stack([_block_diag_halves(rg_w_a[l]), _block_diag_halves(rg_w_i[l])], axis=1).astype(BF16)
    rg_params = (conv_w[l], conv_b[l][None, :], wbd, rg_b_a[l], rg_b_i[l], rg_lambda[l])
    u_bf, vt_bf = _expert_prep_call(peer_u[l], peer_v[l])
    weights = (norm_mix_w[l][None, :], w_in[l].astype(BF16), rg_params, w_out[l].astype(BF16),
               norm_ffn_w[l][None, :], peer_w_q[l].T.astype(BF16), peer_sub_keys[l].astype(BF16),
               u_bf, vt_bf, norm_f_w[None, :])

    ctx_attn = lambda q, kb, vb: _ctx_attn_call(q, kb, vb, sp)
    h0_p = jnp.zeros((bp, 2, D_RG), F32)
    y_prompt, (k_p, v_p), h_last = _path(x_prompt, mod3, 0, False, weights, ctx_attn, h0_p, True)

    tz = _bias_toeplitz(rpb[l])
    ck = cache_k[:, l].reshape(-1, D_ATT).astype(BF16)
    cv = cache_v[:, l].reshape(-1, D_ATT).astype(BF16)
    nbr_attn = lambda q, kb, vb: _nbr_attn_call(q, kb, vb, ck, cv, tz, bs, ss)
    y_sample, _, _ = _path(x_sample, mod3, 1, True, weights, nbr_attn, state_rglru[:, l], False)

    new_k = k_p.reshape(bp, 1, sp, N_HEADS, HEAD_DIM)
    new_v = v_p.reshape(bp, 1, sp, N_HEADS, HEAD_DIM)
    new_h = h_last.reshape(bp, 1, 2, D_RG)
    return (y_prompt, y_sample, new_k, new_v, new_h)
```

```python
import functools

import numpy as np
import jax
import jax.numpy as jnp
from jax import lax
from jax.experimental import pallas as pl
from jax.experimental.pallas import tpu as pltpu

F32 = jnp.float32
BF16 = jnp.bfloat16

D = 1024
D_ATT = 512
N_HEADS = 8
HEAD_DIM = 64
GRID_W = 64
WIN_H = 8
WIN_W = 16
D_RG = 512
D_IN = 3 * D_ATT + 2 * D_RG
N_KEYS = 128
P_HEADS = 8
TOPK = 16
N_EXPERTS = N_KEYS * N_KEYS
RMS_EPS = 1e-6
NEG_INF = -1e30
RG_C = 8.0
ATT_SCALE = HEAD_DIM ** -0.5

VMEM_LIMIT = 56 * 1024 * 1024

_NT = (((1,), (1,)), ((), ()))


def _cparams(*sem, flags=None):
    return pltpu.CompilerParams(dimension_semantics=sem, vmem_limit_bytes=VMEM_LIMIT, flags=flags)


def _rmsnorm(x, w):
    ms = jnp.mean(x * x, axis=-1, keepdims=True)
    return x * lax.rsqrt(ms + RMS_EPS) * w


def _mod_kernel(c_ref, w_ref, b_ref, o_ref):
    cv = c_ref[...]
    s = cv * jax.nn.sigmoid(cv)
    o_ref[...] = jnp.dot(s.astype(BF16), w_ref[...].astype(BF16),
                         preferred_element_type=F32) + b_ref[...]


def _mod_call(cvec, w_mod, b_mod):
    rows = cvec.shape[0]
    tn = 1536
    return pl.pallas_call(
        _mod_kernel,
        grid=(6 * D // tn,),
        in_specs=[pl.BlockSpec((rows, D), lambda j: (0, 0)),
                  pl.BlockSpec((D, tn), lambda j: (0, j)),
                  pl.BlockSpec((1, tn), lambda j: (0, j))],
        out_specs=pl.BlockSpec((rows, tn), lambda j: (0, j)),
        out_shape=jax.ShapeDtypeStruct((rows, 6 * D), F32),
        compiler_params=_cparams("arbitrary"),
        name="mod",
    )(cvec, w_mod, b_mod)


def _inproj_kernel(x_ref, mod_ref, nw_ref, w_ref, *out_refs, emit_f32_kv):
    if emit_f32_kv:
        q_ref, kb_ref, vb_ref, xr_ref, yr_ref, kf_ref, vf_ref = out_refs
    else:
        q_ref, kb_ref, vb_ref, xr_ref, yr_ref = out_refs
    x = x_ref[...]
    h = _rmsnorm(x, nw_ref[...]) * (1.0 + mod_ref[0, 1:2, :]) + mod_ref[0, 0:1, :]
    p = jnp.dot(h.astype(BF16), w_ref[...], preferred_element_type=F32)
    q_ref[...] = (p[:, 0:D_ATT] * ATT_SCALE).astype(BF16)
    k = p[:, D_ATT:2 * D_ATT]
    v = p[:, 2 * D_ATT:3 * D_ATT]
    kb_ref[...] = k.astype(BF16)
    vb_ref[...] = v.astype(BF16)
    xr_ref[...] = p[:, 3 * D_ATT:3 * D_ATT + D_RG]
    yr_ref[...] = p[:, 3 * D_ATT + D_RG:]
    if emit_f32_kv:
        kf_ref[...] = k
        vf_ref[...] = v


def _inproj_call(x2d, mod3, mod_base, tiles_per_batch, norm_w, w_in_bf, emit_f32_kv):
    n = x2d.shape[0]
    tm = 512
    row = lambda t: (mod_base + t // tiles_per_batch, 0, 0)
    tok = lambda t: (t, 0)
    shp = lambda w, dt: jax.ShapeDtypeStruct((n, w), dt)
    out_specs = [pl.BlockSpec((tm, D_ATT), tok)] * 3 + [pl.BlockSpec((tm, D_RG), tok)] * 2
    out_shape = [shp(D_ATT, BF16)] * 3 + [shp(D_RG, F32)] * 2
    if emit_f32_kv:
        out_specs += [pl.BlockSpec((tm, D_ATT), tok)] * 2
        out_shape += [shp(D_ATT, F32)] * 2
    return pl.pallas_call(
        functools.partial(_inproj_kernel, emit_f32_kv=emit_f32_kv),
        grid=(n // tm,),
        in_specs=[pl.BlockSpec((tm, D), tok),
                  pl.BlockSpec((1, 6, D), row),
                  pl.BlockSpec((1, D), lambda t: (0, 0)),
                  pl.BlockSpec((D, D_IN), lambda t: (0, 0))],
        out_specs=out_specs,
        out_shape=out_shape,
        compiler_params=_cparams("arbitrary"),
        name="inproj",
    )(x2d, mod3, norm_w, w_in_bf)


def _ctx_attn_kernel(q_ref, k_ref, v_ref, o_ref):
    for h in range(N_HEADS):
        sl = slice(h * HEAD_DIM, (h + 1) * HEAD_DIM)
        s = lax.dot_general(q_ref[:, sl], k_ref[:, sl], _NT, preferred_element_type=F32)
        m = jnp.max(s, axis=-1, keepdims=True)
        p = jnp.exp(s - m)
        l = jnp.sum(p, axis=-1, keepdims=True)
        o = jnp.dot(p.astype(BF16), v_ref[:, sl], preferred_element_type=F32)
        o_ref[:, sl] = o / l


def _ctx_attn_call(q, kb, vb, seq):
    n = q.shape[0]
    blk = pl.BlockSpec((seq, D_ATT), lambda b: (b, 0))
    return pl.pallas_call(
        _ctx_attn_kernel,
        grid=(n // seq,),
        in_specs=[blk, blk, blk],
        out_specs=blk,
        out_shape=jax.ShapeDtypeStruct((n, D_ATT), F32),
        compiler_params=_cparams("arbitrary"),
        name="ctx_attn",
    )(q, kb, vb)


ROWS_PER_STEP = 4


def _nbr_attn_kernel(q_ref, k_ref, v_ref, ck_ref, cv_ref, tz_ref, o_ref, *, rows):
    g = pl.program_id(0)
    n_loc = WIN_H * GRID_W
    outs = [[None] * N_HEADS for _ in range(ROWS_PER_STEP)]
    for h in range(N_HEADS):
        sl = slice(h * HEAD_DIM, (h + 1) * HEAD_DIM)
        ckh = ck_ref[:, sl]
        cvh = cv_ref[:, sl]
        qg = q_ref[:, sl]
        s_ctx_all = lax.dot_general(qg, ckh, _NT, preferred_element_type=F32)
        for i in range(ROWS_PER_STEP):
            r = g * ROWS_PER_STEP + i
            start = jnp.clip(r - WIN_H // 2, 0, rows - WIN_H)
            row0 = pl.multiple_of(start * GRID_W, GRID_W)
            d0 = start - r + (WIN_H - 1)
            bias = jnp.concatenate([tz_ref[h, d0 + 2 * p] for p in range(WIN_H // 2)], axis=-1)
            qs = slice(i * GRID_W, (i + 1) * GRID_W)
            s_loc = lax.dot_general(qg[qs], k_ref[pl.ds(row0, n_loc), sl], _NT,
                                    preferred_element_type=F32) + bias
            s_ctx = s_ctx_all[qs]
            m = jnp.maximum(jnp.max(s_loc, axis=-1, keepdims=True), jnp.max(s_ctx, axis=-1, keepdims=True))
            p_loc = jnp.exp(s_loc - m)
            p_ctx = jnp.exp(s_ctx - m)
            l = jnp.sum(p_loc, axis=-1, keepdims=True) + jnp.sum(p_ctx, axis=-1, keepdims=True)
            o = (jnp.dot(p_loc.astype(BF16), v_ref[pl.ds(row0, n_loc), sl], preferred_element_type=F32)
                 + jnp.dot(p_ctx.astype(BF16), cvh, preferred_element_type=F32))
            outs[i][h] = o / l
    for i in range(ROWS_PER_STEP):
        o_ref[i * GRID_W:(i + 1) * GRID_W, :] = jnp.concatenate(outs[i], axis=-1)


def _nbr_attn_call(q, kb, vb, ck, cv, tz, batch, seq):
    rows = seq // GRID_W
    groups = rows // ROWS_PER_STEP
    past = ck.shape[0] // batch
    n = q.shape[0]
    qblk = pl.BlockSpec((GRID_W * ROWS_PER_STEP, D_ATT), lambda g, b: (b * groups + g, 0))
    kvblk = pl.BlockSpec((seq, D_ATT), lambda g, b: (b, 0))
    cblk = pl.BlockSpec((past, D_ATT), lambda g, b: (b, 0))
    return pl.pallas_call(
        functools.partial(_nbr_attn_kernel, rows=rows),
        grid=(groups, batch),
        in_specs=[qblk, kvblk, kvblk, cblk, cblk,
                  pl.BlockSpec(tz.shape, lambda g, b: (0, 0, 0, 0))],
        out_specs=qblk,
        out_shape=jax.ShapeDtypeStruct((n, D_ATT), F32),
        compiler_params=_cparams("arbitrary", "arbitrary"),
        name="nbr_attn",
    )(q, kb, vb, ck, cv, tz)


def _bias_toeplitz(rpb_l):
    qc = np.arange(GRID_W)
    off = qc[None, :] - qc[:, None] + (WIN_W - 1)
    place = (off[None, :, :] == np.arange(2 * WIN_W - 1)[:, None, None]).astype(np.float32)
    tz = jnp.einsum('hdx,xqk->hdqk', rpb_l, jnp.asarray(place), precision=lax.Precision.HIGHEST)
    cs = np.clip(qc - WIN_W // 2, 0, GRID_W - WIN_W)
    valid = (qc[None, :] >= cs[:, None]) & (qc[None, :] < cs[:, None] + WIN_W)
    tz = jnp.where(jnp.asarray(valid)[None, None], tz, NEG_INF)
    return jnp.concatenate([tz[:, :-1], tz[:, 1:]], axis=-1)


def _log_sigmoid(x):
    return jnp.minimum(x, 0.0) - jnp.log(1.0 + jnp.exp(-jnp.abs(x)))


def _gelu_tanh(x):
    return 0.5 * x * (1.0 + jnp.tanh(0.7978845608028654 * (x + 0.044715 * x * x * x)))


def _rglru_kernel(xr_ref, yr_ref, h0_ref, cw_ref, cb_ref, wbd_ref, ba_ref, bi_ref, lam_ref,
                  y_ref, hl_ref, xpad, a_f, b_f, a_b, b_b):
    T = xr_ref.shape[0]
    CH = 256
    xpad[0:8, :] = jnp.zeros((8, D_RG), F32)
    xpad[T + 8:T + 16, :] = jnp.zeros((8, D_RG), F32)
    xpad[8:T + 8, :] = xr_ref[...]
    a_refs = (a_f, a_b)
    b_refs = (b_f, b_b)
    for c0 in range(0, T, CH):
        xc = cb_ref[...] + cw_ref[0:1, :] * xpad[c0 + 6:c0 + 6 + CH, :]
        for i in range(1, 4):
            xc = xc + cw_ref[i:i + 1, :] * xpad[c0 + 6 + i:c0 + 6 + i + CH, :]
        xcb = xc.astype(BF16)
        for d in range(2):
            gates = []
            for g in range(2):
                halves = [jnp.dot(xcb[:, hf * 256:(hf + 1) * 256], wbd_ref[d, g, hf],
                                  preferred_element_type=F32) for hf in range(2)]
                gates.append(jnp.concatenate(halves, axis=-1))
            rg = jax.nn.sigmoid(gates[0] + ba_ref[d:d + 1, :])
            ig = jax.nn.sigmoid(gates[1] + bi_ref[d:d + 1, :])
            log_a = RG_C * rg * _log_sigmoid(lam_ref[d:d + 1, :])
            a = jnp.exp(log_a)
            b = jnp.sqrt(1.0 - a * a) * (ig * xc)
            a_refs[d][c0:c0 + CH, :] = a
            b_refs[d][c0:c0 + CH, :] = b

    sub = lax.broadcasted_iota(jnp.int32, (8, D_RG), 0)
    n_tiles = T // 8

    def scan_tile(a_ref, b_ref, t8, h, order):
        base = pl.multiple_of(t8 * 8, 8)
        a_blk = a_ref[pl.ds(base, 8), :]
        b_blk = b_ref[pl.ds(base, 8), :]
        out = b_blk
        for j in order:
            cand = a_blk * h + b_blk
            h = jnp.broadcast_to(cand[j:j + 1, :], (8, D_RG))
            out = jnp.where(sub == j, cand, out)
        b_ref[pl.ds(base, 8), :] = out
        return h

    def body(i, carry):
        hf, hb = carry
        hf = scan_tile(a_f, b_f, i, hf, range(8))
        hb = scan_tile(a_b, b_b, n_tiles - 1 - i, hb, range(7, -1, -1))
        return hf, hb

    hf0 = jnp.broadcast_to(h0_ref[0, 0:1, :], (8, D_RG))
    hb0 = jnp.broadcast_to(h0_ref[0, 1:2, :], (8, D_RG))
    hf, hb = lax.fori_loop(0, n_tiles, body, (hf0, hb0))
    hl_ref[0, 0:1, :] = hf[0:1, :]
    hl_ref[0, 1:2, :] = hb[0:1, :]
    for c0 in range(0, T, CH):
        sl = slice(c0, c0 + CH)
        y_ref[sl, :] = (b_f[sl, :] + b_b[sl, :]) * _gelu_tanh(yr_ref[sl, :])


def _rglru_call(xr, yr, h0, conv_w, conv_b, wbd, b_a, b_i, lam, seq):
    n = xr.shape[0]
    batch = n // seq
    tok = pl.BlockSpec((seq, D_RG), lambda b: (b, 0))
    full = lambda shape: pl.BlockSpec(shape, lambda b: (0,) * len(shape))
    return pl.pallas_call(
        _rglru_kernel,
        grid=(batch,),
        in_specs=[tok, tok, pl.BlockSpec((1, 2, D_RG), lambda b: (b, 0, 0)),
                  full((4, D_RG)), full((1, D_RG)), full((2, 2, 2, 256, 256)),
                  full((2, D_RG)), full((2, D_RG)), full((2, D_RG))],
        out_specs=[tok, pl.BlockSpec((1, 2, D_RG), lambda b: (b, 0, 0))],
        out_shape=[jax.ShapeDtypeStruct((n, D_RG), F32), jax.ShapeDtypeStruct((batch, 2, D_RG), F32)],
        scratch_shapes=[pltpu.VMEM((seq + 16, D_RG), F32)] + [pltpu.VMEM((seq, D_RG), F32)] * 4,
        compiler_params=_cparams("arbitrary"),
        name="rglru",
    )(xr, yr, h0, conv_w, conv_b, wbd, b_a, b_i, lam)


def _block_diag_halves(w):
    d = w.shape[0]
    w4 = w.reshape(d, 2, 4, 64, 64)
    eye = jnp.eye(4, dtype=w.dtype)
    out = jnp.einsum('dhncf,nm->dhncmf', w4, eye)
    return out.reshape(d, 2, 256, 256)


N_HP = 2 * P_HEADS


def _candidate_flat(tm):
    subf = lax.broadcasted_iota(jnp.int32, (8, tm), 0).astype(F32)
    flat = [subf + float(r1 * TOPK) for r1 in range(4)]
    flat.append(subf + 8.0)
    flat += [subf * float(TOPK) + float(r2) for r2 in range(3)]
    flat.append((subf + 8.0) * float(TOPK))
    return jnp.concatenate(flat, axis=0)


def _candidate_sums(t1, t2):
    tm = t1.shape[1]
    sub = lax.broadcasted_iota(jnp.int32, (8, tm), 0)
    ninf = -jnp.inf
    lim_a = (8, 8, 5, 4)
    vals = [jnp.where(sub < lim_a[r1], t1[r1:r1 + 1, :] + t2[0:8, :], ninf) for r1 in range(4)]
    vals.append(t1[0:1, :] + t2[8:16, :])
    for r2 in range(3):
        ok = (sub >= 4) if r2 < 2 else (sub == 4)
        vals.append(jnp.where(ok, t1[0:8, :] + t2[r2:r2 + 1, :], ninf))
    vals.append(t1[8:16, :] + t2[0:1, :])
    return jnp.concatenate(vals, axis=0)


def _staircase_rows(sel):
    rows = []
    for r1 in range(4):
        l = jnp.sum(sel[8 * r1:8 * r1 + 8, :], axis=0, keepdims=True)
        if r1 == 0:
            l = l + jnp.sum(sel[32:40, :], axis=0, keepdims=True)
        rows.append(l)
    l_mid = sel[40:48, :] + sel[48:56, :] + sel[56:64, :]
    l_hi = sel[64:72, :]
    rows += [l_mid[r1:r1 + 1, :] for r1 in range(4, 8)]
    rows += [l_hi[r1 - 8:r1 - 7, :] for r1 in range(8, 16)]
    return rows


def _batcher_pairs(n):
    pairs = []

    def merge(lo, cnt, r):
        step = r * 2
        if step < cnt:
            merge(lo, cnt, step)
            merge(lo + r, cnt, step)
            for i in range(lo + r, lo + cnt - r, step):
                pairs.append((i, i + r))
        else:
            pairs.append((lo, lo + r))

    def sort(lo, cnt):
        if cnt > 1:
            m = cnt // 2
            sort(lo, m)
            sort(lo + m, m)
            merge(lo, cnt, 1)

    sort(0, n)
    return pairs


_SORT16 = _batcher_pairs(TOPK)


def _top16_network(s):
    v = list(s)
    for i, j in _SORT16:
        v[i], v[j] = jnp.maximum(v[i], v[j]), jnp.minimum(v[i], v[j])
    for shift in (4, 2, 1):
        rolled = [pltpu.roll(x, shift, 0) for x in v]
        c = [jnp.maximum(v[k], rolled[TOPK - 1 - k]) for k in range(TOPK)]
        for d in (8, 4, 2, 1):
            for k in range(TOPK):
                if k & d == 0:
                    c[k], c[k + d] = jnp.maximum(c[k], c[k + d]), jnp.minimum(c[k], c[k + d])
        v = c
    return v


def _route_tile(sk_ref, qt_scr, s_scr, w_scr, rank_scr, t_scr, cand_scr, sel_scr, z_scr,
                a1_ref, c_ref, b2_ref, e2_ref):
    tm = qt_scr.shape[1]
    iota_f = lax.broadcasted_iota(jnp.int32, (N_KEYS, tm), 0).astype(F32)
    flat = _candidate_flat(tm)
    ninf = -jnp.inf
    for hp in range(N_HP):
        s_scr[hp] = jnp.dot(sk_ref[hp % 2], qt_scr[hp * N_KEYS:(hp + 1) * N_KEYS, :],
                            preferred_element_type=F32)

    def stage1_extract():
        for hp in range(N_HP):
            w_scr[hp] = s_scr[hp]
            rank_scr[hp] = jnp.full((N_KEYS, tm), float(TOPK), F32)

        def round1(r, carry):
            rf = jnp.asarray(r, jnp.int32).astype(F32)
            for hp in range(N_HP):
                w = w_scr[hp]
                m = jnp.max(w, axis=0, keepdims=True)
                idx = jnp.min(jnp.where(w == m, iota_f, float(N_KEYS)), axis=0, keepdims=True)
                hit = iota_f == idx
                rank_scr[hp] = jnp.where(hit, rf, rank_scr[hp])
                w_scr[hp] = jnp.where(hit, ninf, w)
                t_scr[hp, pl.ds(r, 1), :] = m
            return carry

        lax.fori_loop(0, TOPK, round1, 0)

    def stage1_network():
        lane_blocks = tm // 128
        sub = lax.broadcasted_iota(jnp.int32, (8, 128), 0)

        def body(hp, flags):
            out = []
            for lb in range(lane_blocks):
                ls = slice(lb * 128, (lb + 1) * 128)
                s = [s_scr[hp, 8 * v:8 * v + 8, ls] for v in range(N_KEYS // 8)]
                t = _top16_network(s)
                lo, hi = t[0], t[8]
                for r in range(1, 8):
                    lo = jnp.where(sub == r, t[r], lo)
                    hi = jnp.where(sub == r, t[8 + r], hi)
                t_scr[hp, 0:8, ls] = lo
                t_scr[hp, 8:16, ls] = hi
                for v in range(N_KEYS // 8):
                    rank = jnp.zeros((8, 128), F32)
                    for r in range(TOPK):
                        rank = jnp.where(t[r] > s[v], float(r + 1), rank)
                    rank_scr[hp, 8 * v:8 * v + 8, ls] = rank
                tie = flags[lb]
                for r in range(TOPK - 1):
                    tie = jnp.where(t[r] == t[r + 1], 1.0, tie)
                out.append(tie)
            return tuple(out)

        flags = lax.fori_loop(0, N_HP, body, (jnp.zeros((8, 128), F32),) * lane_blocks)
        return jnp.concatenate([f[0:1, :] for f in flags], axis=1)

    def stage2(exact):
        for h in range(P_HEADS):
            cand_scr[h] = _candidate_sums(t_scr[2 * h], t_scr[2 * h + 1])
            sel_scr[h] = jnp.zeros((72, tm), F32)

        def round2(r, carry):
            zs, b0s = carry
            zs_new, b0s_new = [], []
            for h in range(P_HEADS):
                w = cand_scr[h]
                m = jnp.max(w, axis=0, keepdims=True)
                if exact:
                    f = jnp.min(jnp.where(w == m, flat, 1e9), axis=0, keepdims=True)
                    hit = flat == f
                else:
                    hit = w == m
                sel_scr[h] = jnp.where(hit, 1.0, sel_scr[h])
                cand_scr[h] = jnp.where(hit, ninf, w)
                b0 = jnp.where(r == 0, m, b0s[h])
                zs_new.append(zs[h] + jnp.exp(m - b0))
                b0s_new.append(b0)
            return tuple(zs_new), tuple(b0s_new)

        zero_row = jnp.zeros((1, tm), F32)
        zs, _ = lax.fori_loop(0, TOPK, round2, ((zero_row,) * P_HEADS, (zero_row,) * P_HEADS))
        for h in range(P_HEADS):
            z_scr[h:h + 1, :] = zs[h]

    excess = stage1_network()
    stage2(False)
    for hp in range(N_HP):
        cnt = jnp.sum(jnp.where(rank_scr[hp] < float(TOPK), 1.0, 0.0), axis=0, keepdims=True)
        excess = jnp.maximum(excess, jnp.abs(cnt - float(TOPK)))
    for h in range(P_HEADS):
        cnt = jnp.sum(sel_scr[h], axis=0, keepdims=True)
        excess = jnp.maximum(excess, jnp.abs(cnt - float(TOPK)))

    @pl.when(jnp.max(excess) > 0.0)
    def _():
        stage1_extract()
        stage2(True)

    for h in range(P_HEADS):
        l_rows = _staircase_rows(sel_scr[h])
        rank1 = rank_scr[2 * h]
        a1 = jnp.zeros((N_KEYS, tm), F32)
        for r1 in range(TOPK):
            a1 = jnp.where(rank1 == float(r1), l_rows[r1], a1)
        a1_ref[h] = a1
        c_ref[h] = jnp.exp(s_scr[2 * h] - t_scr[2 * h, 0:1, :]) * (0.5 / z_scr[h:h + 1, :])
        b2_ref[h] = rank_scr[2 * h + 1].astype(BF16)
        e2_ref[h] = jnp.exp(s_scr[2 * h + 1] - t_scr[2 * h + 1, 0:1, :]).astype(BF16)


def _mixffn_kernel(x_ref, att_ref, rg_ref, mod_ref, wo_ref, nw_ref, wqt_ref, sk_ref,
                   x1_ref, h2t_ref, a1_ref, c_ref, b2_ref, e2_ref,
                   qt_scr, s_scr, w_scr, rank_scr, t_scr, cand_scr, sel_scr, z_scr):
    o = (jnp.dot(att_ref[...].astype(BF16), wo_ref[0:D_ATT, :], preferred_element_type=F32)
         + jnp.dot(rg_ref[...].astype(BF16), wo_ref[D_ATT:, :], preferred_element_type=F32))
    x1 = x_ref[...] + mod_ref[0, 2:3, :] * o
    x1_ref[...] = x1
    h2 = _rmsnorm(x1, nw_ref[...]) * (1.0 + mod_ref[0, 4:5, :]) + mod_ref[0, 3:4, :]
    h2t = h2.T.astype(BF16)
    h2t_ref[...] = h2t
    qt_scr[...] = jnp.dot(wqt_ref[...], h2t, preferred_element_type=F32).astype(BF16)
    _route_tile(sk_ref, qt_scr, s_scr, w_scr, rank_scr, t_scr, cand_scr, sel_scr, z_scr,
                a1_ref, c_ref, b2_ref, e2_ref)


def _mixffn_call(x2d, att, rg, mod3, mod_base, tiles_per_batch, w_out_bf, norm_w, w_qt_bf, sk_bf, tm):
    n = x2d.shape[0]
    tok = lambda t: (t, 0)
    route = pl.BlockSpec((P_HEADS, N_KEYS, tm), lambda t: (0, 0, t))
    route_shape = jax.ShapeDtypeStruct((P_HEADS, N_KEYS, n), F32)
    return pl.pallas_call(
        _mixffn_kernel,
        grid=(n // tm,),
        in_specs=[pl.BlockSpec((tm, D), tok),
                  pl.BlockSpec((tm, D_ATT), tok),
                  pl.BlockSpec((tm, D_RG), tok),
                  pl.BlockSpec((1, 6, D), lambda t: (mod_base + t // tiles_per_batch, 0, 0)),
                  pl.BlockSpec((D, D), lambda t: (0, 0)),
                  pl.BlockSpec((1, D), lambda t: (0, 0)),
                  pl.BlockSpec((2 * P_HEADS * N_KEYS, D), lambda t: (0, 0)),
                  pl.BlockSpec((2, N_KEYS, N_KEYS), lambda t: (0, 0, 0))],
        out_specs=[pl.BlockSpec((tm, D), tok),
                   pl.BlockSpec((D, tm), lambda t: (0, t)),
                   route, route, route, route],
        out_shape=[jax.ShapeDtypeStruct((n, D), F32), jax.ShapeDtypeStruct((D, n), BF16),
                   route_shape, route_shape,
                   jax.ShapeDtypeStruct((P_HEADS, N_KEYS, n), BF16), jax.ShapeDtypeStruct((P_HEADS, N_KEYS, n), BF16)],
        scratch_shapes=[pltpu.VMEM((2 * P_HEADS * N_KEYS, tm), BF16)]
                       + [pltpu.VMEM((N_HP, N_KEYS, tm), F32)] * 3
                       + [pltpu.VMEM((N_HP, TOPK, tm), F32)]
                       + [pltpu.VMEM((P_HEADS, 72, tm), F32)] * 2
                       + [pltpu.VMEM((P_HEADS, tm), F32)],
        compiler_params=_cparams("arbitrary"),
        name="mixffn",
    )(x2d, att, rg, mod3, w_out_bf, norm_w, w_qt_bf, sk_bf)


def _expert_prep_kernel(u_ref, v_ref, ub_ref, vt_ref):
    ub_ref[...] = u_ref[...].astype(BF16)
    vt_ref[...] = v_ref[...].T.astype(BF16)


def _expert_prep_call(u, v):
    te = 512
    return pl.pallas_call(
        _expert_prep_kernel,
        grid=(N_EXPERTS // te,),
        in_specs=[pl.BlockSpec((te, D), lambda j: (j, 0)), pl.BlockSpec((te, D), lambda j: (j, 0))],
        out_specs=[pl.BlockSpec((te, D), lambda j: (j, 0)), pl.BlockSpec((D, te), lambda j: (0, j))],
        out_shape=[jax.ShapeDtypeStruct((N_EXPERTS, D), BF16), jax.ShapeDtypeStruct((D, N_EXPERTS), BF16)],
        compiler_params=_cparams("arbitrary"),
        name="expert_prep",
    )(u, v)


def _peer_kernel(h2t_ref, u_ref, vt_ref, a1_ref, c_ref, b2_ref, e2_ref, x1_ref, mod_ref, nw_ref,
                 y_ref, acc, g_scr, *, i1_per_step):
    j = pl.program_id(1)
    tm = acc.shape[1]

    @pl.when(j == 0)
    def _():
        acc[...] = jnp.zeros_like(acc)

    grp = 4
    one = jnp.ones((), BF16)
    zero = jnp.zeros((), BF16)
    for i0 in range(0, i1_per_step, grp):
        for l in range(tm // 128):
            ls = slice(l * 128, (l + 1) * 128)
            g = [None] * grp
            for h in range(P_HEADS):
                b2 = b2_ref[h, :, ls]
                e2 = e2_ref[h, :, ls]
                for k in range(grp):
                    a1 = a1_ref[h, i0 + k:i0 + k + 1, ls].astype(BF16)
                    ch = c_ref[h, i0 + k:i0 + k + 1, ls].astype(BF16)
                    m = jnp.minimum(jnp.maximum(a1 - b2, zero), one)
                    w = (ch * e2) * m
                    g[k] = w if g[k] is None else g[k] + w
            for k in range(grp):
                g_scr[(i0 + k) * N_KEYS:(i0 + k + 1) * N_KEYS, ls] = g[k]
    st = jnp.dot(u_ref[...], h2t_ref[...], preferred_element_type=F32)
    act = st * (1.0 + lax.erf(st * 0.7071067811865476))
    at = act.astype(BF16) * g_scr[...]
    acc[...] += jnp.dot(vt_ref[...], at, preferred_element_type=F32)

    @pl.when(j == pl.num_programs(1) - 1)
    def _():
        x2 = x1_ref[...] + mod_ref[0, 5:6, :] * acc[...].T
        y_ref[...] = _rmsnorm(x2, nw_ref[...])


def _peer_call(h2t, u_bf, vt_bf, a1, c, b2, e2, x1, mod3, mod_base, tiles_per_batch, norm_f_w, tm, te):
    n = x1.shape[0]
    ips = te // N_KEYS
    route_lo = pl.BlockSpec((P_HEADS, ips, tm), lambda t, j: (0, j, t))
    route_full = pl.BlockSpec((P_HEADS, N_KEYS, tm), lambda t, j: (0, 0, t))
    return pl.pallas_call(
        functools.partial(_peer_kernel, i1_per_step=ips),
        grid=(n // tm, N_EXPERTS // te),
        in_specs=[pl.BlockSpec((D, tm), lambda t, j: (0, t)),
                  pl.BlockSpec((te, D), lambda t, j: (j, 0)),
                  pl.BlockSpec((D, te), lambda t, j: (0, j)),
                  route_lo, route_lo, route_full, route_full,
                  pl.BlockSpec((tm, D), lambda t, j: (t, 0)),
                  pl.BlockSpec((1, 6, D), lambda t, j: (mod_base + t // tiles_per_batch, 0, 0)),
                  pl.BlockSpec((1, D), lambda t, j: (0, 0))],
        out_specs=pl.BlockSpec((tm, D), lambda t, j: (t, 0)),
        out_shape=jax.ShapeDtypeStruct((n, D), F32),
        scratch_shapes=[pltpu.VMEM((D, tm), F32), pltpu.VMEM((te, tm), BF16)],
        compiler_params=_cparams("arbitrary", "arbitrary"),
        name="peer",
    )(h2t, u_bf, vt_bf, a1, c, b2, e2, x1, mod3, norm_f_w)


def _path(x, mod3, mod_base, per_batch_mod, weights, attn_fn, h0, emit_cache):
    (norm_mix_w, w_in_bf, rg_params, w_out_bf, norm_ffn_w, w_qt_bf, sk_bf, u_bf, vt_bf, norm_f_w) = weights
    bsz, seq, _ = x.shape
    n = bsz * seq
    x2d = x.reshape(n, D)
    tpb = lambda tm: (seq // tm) if per_batch_mod else n
    q, kb, vb, xr, yr, *kv_f32 = _inproj_call(x2d, mod3, mod_base, tpb(512), norm_mix_w, w_in_bf, emit_cache)
    att = attn_fn(q, kb, vb)
    rg_out, h_last = _rglru_call(xr, yr, h0, *rg_params, seq)
    tm_mix = 256
    x1, h2t, a1, c, b2, e2 = _mixffn_call(x2d, att, rg_out, mod3, mod_base, tpb(tm_mix), w_out_bf,
                                          norm_ffn_w, w_qt_bf, sk_bf, tm_mix)
    tm_peer = 512
    y = _peer_call(h2t, u_bf, vt_bf, a1, c, b2, e2, x1, mod3, mod_base, tpb(tm_peer), norm_f_w,
                   tm_peer, 1024)
    return y.reshape(bsz, seq, D), kv_f32, h_last


def kernel(x_prompt, x_sample, c, cache_k, cache_v, state_rglru, c_ctx, w_mod, b_mod, norm_mix_w, w_in,
           rpb, conv_w, conv_b, rg_w_a, rg_b_a, rg_w_i, rg_b_i, rg_lambda, w_out, norm_ffn_w,
           peer_w_q, peer_sub_keys, peer_u, peer_v, norm_f_w):
    depth = w_mod.shape[0]
    assert depth == 1, "single-layer problem"
    l = 0
    bp, sp, _ = x_prompt.shape
    bs, ss, _ = x_sample.shape

    n_rows = 16
    cvec = jnp.concatenate([c_ctx[None, :], c, jnp.zeros((n_rows - 1 - bs, D), F32)], axis=0)
    mod3 = _mod_call(cvec, w_mod[l], b_mod[l][None, :]).reshape(n_rows, 6, D)

    wbd = jnp.stack([_block_diag_halves(rg_w_a[l]), _block_diag_halves(rg_w_i[l])], axis=1).astype(BF16)
    rg_params = (conv_w[l], conv_b[l][None, :], wbd, rg_b_a[l], rg_b_i[l], rg_lambda[l])
    u_bf, vt_bf = _expert_prep_call(peer_u[l], peer_v[l])
    weights = (norm_mix_w[l][None, :], w_in[l].astype(BF16), rg_params, w_out[l].astype(BF16),
               norm_ffn_w[l][None, :], peer_w_q[l].T.astype(BF16), peer_sub_keys[l].astype(BF16),
               u_bf, vt_bf, norm_f_w[None, :])

    ctx_attn = lambda q, kb, vb: _ctx_attn_call(q, kb, vb, sp)
    h0_p = jnp.zeros((bp, 2, D_RG), F32)
    y_prompt, (k_p, v_p), h_last = _path(x_prompt, mod3, 0, False, weights, ctx_attn, h0_p, True)

    tz = _bias_toeplitz(rpb[l])
    ck = cache_k[:, l].reshape(-1, D_ATT).astype(BF16)
    cv = cache_v[:, l].reshape(-1, D_ATT).astype(BF16)
    nbr_attn = lambda q, kb, vb: _nbr_attn_call(q, kb, vb, ck, cv, tz, bs, ss)
    y_sample, _, _ = _path(x_sample, mod3, 1, True, weights, nbr_attn, state_rglru[:, l], False)

    new_k = k_p.reshape(bp, 1, sp, N_HEADS, HEAD_DIM)
    new_v = v_p.reshape(bp, 1, sp, N_HEADS, HEAD_DIM)
    new_h = h_last.reshape(bp, 1, 2, D_RG)
    return (y_prompt, y_sample, new_k, new_v, new_h)
```

```python
import functools

import numpy as np
import jax
import jax.numpy as jnp
from jax import lax
from jax.experimental import pallas as pl
from jax.experimental.pallas import tpu as pltpu

F32 = jnp.float32
BF16 = jnp.bfloat16

D = 1024
D_ATT = 512
N_HEADS = 8
HEAD_DIM = 64
GRID_W = 64
WIN_H = 8
WIN_W = 16
D_RG = 512
D_IN = 3 * D_ATT + 2 * D_RG
N_KEYS = 128
P_HEADS = 8
TOPK = 16
N_EXPERTS = N_KEYS * N_KEYS
RMS_EPS = 1e-6
NEG_INF = -1e30
RG_C = 8.0
ATT_SCALE = HEAD_DIM ** -0.5

VMEM_LIMIT = 56 * 1024 * 1024

_NT = (((1,), (1,)), ((), ()))


def _cparams(*sem, flags=None):
    return pltpu.CompilerParams(dimension_semantics=sem, vmem_limit_bytes=VMEM_LIMIT, flags=flags)


def _rmsnorm(x, w):
    ms = jnp.mean(x * x, axis=-1, keepdims=True)
    return x * lax.rsqrt(ms + RMS_EPS) * w


def _mod_kernel(c_ref, w_ref, b_ref, o_ref):
    cv = c_ref[...]
    s = cv * jax.nn.sigmoid(cv)
    o_ref[...] = jnp.dot(s.astype(BF16), w_ref[...].astype(BF16),
                         preferred_element_type=F32) + b_ref[...]


def _mod_call(cvec, w_mod, b_mod):
    rows = cvec.shape[0]
    tn = 1536
    return pl.pallas_call(
        _mod_kernel,
        grid=(6 * D // tn,),
        in_specs=[pl.BlockSpec((rows, D), lambda j: (0, 0)),
                  pl.BlockSpec((D, tn), lambda j: (0, j)),
                  pl.BlockSpec((1, tn), lambda j: (0, j))],
        out_specs=pl.BlockSpec((rows, tn), lambda j: (0, j)),
        out_shape=jax.ShapeDtypeStruct((rows, 6 * D), F32),
        compiler_params=_cparams("arbitrary"),
        name="mod",
    )(cvec, w_mod, b_mod)


def _inproj_kernel(x_ref, mod_ref, nw_ref, w_ref, *out_refs, emit_f32_kv):
    if emit_f32_kv:
        q_ref, kb_ref, vb_ref, xr_ref, yr_ref, kf_ref, vf_ref = out_refs
    else:
        q_ref, kb_ref, vb_ref, xr_ref, yr_ref = out_refs
    x = x_ref[...]
    h = _rmsnorm(x, nw_ref[...]) * (1.0 + mod_ref[0, 1:2, :]) + mod_ref[0, 0:1, :]
    p = jnp.dot(h.astype(BF16), w_ref[...], preferred_element_type=F32)
    q_ref[...] = (p[:, 0:D_ATT] * ATT_SCALE).astype(BF16)
    k = p[:, D_ATT:2 * D_ATT]
    v = p[:, 2 * D_ATT:3 * D_ATT]
    kb_ref[...] = k.astype(BF16)
    vb_ref[...] = v.astype(BF16)
    xr_ref[...] = p[:, 3 * D_ATT:3 * D_ATT + D_RG]
    yr_ref[...] = p[:, 3 * D_ATT + D_RG:]
    if emit_f32_kv:
        kf_ref[...] = k
        vf_ref[...] = v


def _inproj_call(x2d, mod3, mod_base, tiles_per_batch, norm_w, w_in_bf, emit_f32_kv):
    n = x2d.shape[0]
    tm = 512
    row = lambda t: (mod_base + t // tiles_per_batch, 0, 0)
    tok = lambda t: (t, 0)
    shp = lambda w, dt: jax.ShapeDtypeStruct((n, w), dt)
    out_specs = [pl.BlockSpec((tm, D_ATT), tok)] * 3 + [pl.BlockSpec((tm, D_RG), tok)] * 2
    out_shape = [shp(D_ATT, BF16)] * 3 + [shp(D_RG, F32)] * 2
    if emit_f32_kv:
        out_specs += [pl.BlockSpec((tm, D_ATT), tok)] * 2
        out_shape += [shp(D_ATT, F32)] * 2
    return pl.pallas_call(
        functools.partial(_inproj_kernel, emit_f32_kv=emit_f32_kv),
        grid=(n // tm,),
        in_specs=[pl.BlockSpec((tm, D), tok),
                  pl.BlockSpec((1, 6, D), row),
                  pl.BlockSpec((1, D), lambda t: (0, 0)),
                  pl.BlockSpec((D, D_IN), lambda t: (0, 0))],
        out_specs=out_specs,
        out_shape=out_shape,
        compiler_params=_cparams("arbitrary"),
        name="inproj",
    )(x2d, mod3, norm_w, w_in_bf)


def _ctx_attn_kernel(q_ref, k_ref, v_ref, o_ref):
    for h in range(N_HEADS):
        sl = slice(h * HEAD_DIM, (h + 1) * HEAD_DIM)
        s = lax.dot_general(q_ref[:, sl], k_ref[:, sl], _NT, preferred_element_type=F32)
        m = jnp.max(s, axis=-1, keepdims=True)
        p = jnp.exp(s - m)
        l = jnp.sum(p, axis=-1, keepdims=True)
        o = jnp.dot(p.astype(BF16), v_ref[:, sl], preferred_element_type=F32)
        o_ref[:, sl] = o / l


def _ctx_attn_call(q, kb, vb, seq):
    n = q.shape[0]
    blk = pl.BlockSpec((seq, D_ATT), lambda b: (b, 0))
    return pl.pallas_call(
        _ctx_attn_kernel,
        grid=(n // seq,),
        in_specs=[blk, blk, blk],
        out_specs=blk,
        out_shape=jax.ShapeDtypeStruct((n, D_ATT), F32),
        compiler_params=_cparams("arbitrary"),
        name="ctx_attn",
    )(q, kb, vb)


ROWS_PER_STEP = 4


def _nbr_attn_kernel(q_ref, k_ref, v_ref, ck_ref, cv_ref, tz_ref, o_ref, *, rows):
    g = pl.program_id(0)
    n_loc = WIN_H * GRID_W
    outs = [[None] * N_HEADS for _ in range(ROWS_PER_STEP)]
    for h in range(N_HEADS):
        sl = slice(h * HEAD_DIM, (h + 1) * HEAD_DIM)
        ckh = ck_ref[:, sl]
        cvh = cv_ref[:, sl]
        qg = q_ref[:, sl]
        s_ctx_all = lax.dot_general(qg, ckh, _NT, preferred_element_type=F32)
        for i in range(ROWS_PER_STEP):
            r = g * ROWS_PER_STEP + i
            start = jnp.clip(r - WIN_H // 2, 0, rows - WIN_H)
            row0 = pl.multiple_of(start * GRID_W, GRID_W)
            d0 = start - r + (WIN_H - 1)
            bias = jnp.concatenate([tz_ref[h, d0 + 2 * p] for p in range(WIN_H // 2)], axis=-1)
            qs = slice(i * GRID_W, (i + 1) * GRID_W)
            s_loc = lax.dot_general(qg[qs], k_ref[pl.ds(row0, n_loc), sl], _NT,
                                    preferred_element_type=F32) + bias
            s_ctx = s_ctx_all[qs]
            m = jnp.maximum(jnp.max(s_loc, axis=-1, keepdims=True), jnp.max(s_ctx, axis=-1, keepdims=True))
            p_loc = jnp.exp(s_loc - m)
            p_ctx = jnp.exp(s_ctx - m)
            l = jnp.sum(p_loc, axis=-1, keepdims=True) + jnp.sum(p_ctx, axis=-1, keepdims=True)
            o = (jnp.dot(p_loc.astype(BF16), v_ref[pl.ds(row0, n_loc), sl], preferred_element_type=F32)
                 + jnp.dot(p_ctx.astype(BF16), cvh, preferred_element_type=F32))
            outs[i][h] = o / l
    for i in range(ROWS_PER_STEP):
        o_ref[i * GRID_W:(i + 1) * GRID_W, :] = jnp.concatenate(outs[i], axis=-1)


def _nbr_attn_call(q, kb, vb, ck, cv, tz, batch, seq):
    rows = seq // GRID_W
    groups = rows // ROWS_PER_STEP
    past = ck.shape[0] // batch
    n = q.shape[0]
    qblk = pl.BlockSpec((GRID_W * ROWS_PER_STEP, D_ATT), lambda g, b: (b * groups + g, 0))
    kvblk = pl.BlockSpec((seq, D_ATT), lambda g, b: (b, 0))
    cblk = pl.BlockSpec((past, D_ATT), lambda g, b: (b, 0))
    return pl.pallas_call(
        functools.partial(_nbr_attn_kernel, rows=rows),
        grid=(groups, batch),
        in_specs=[qblk, kvblk, kvblk, cblk, cblk,
                  pl.BlockSpec(tz.shape, lambda g, b: (0, 0, 0, 0))],
        out_specs=qblk,
        out_shape=jax.ShapeDtypeStruct((n, D_ATT), F32),
        compiler_params=_cparams("arbitrary", "arbitrary"),
        name="nbr_attn",
    )(q, kb, vb, ck, cv, tz)


def _bias_toeplitz(rpb_l):
    qc = np.arange(GRID_W)
    off = qc[None, :] - qc[:, None] + (WIN_W - 1)
    place = (off[None, :, :] == np.arange(2 * WIN_W - 1)[:, None, None]).astype(np.float32)
    tz = jnp.einsum('hdx,xqk->hdqk', rpb_l, jnp.asarray(place), precision=lax.Precision.HIGHEST)
    cs = np.clip(qc - WIN_W // 2, 0, GRID_W - WIN_W)
    valid = (qc[None, :] >= cs[:, None]) & (qc[None, :] < cs[:, None] + WIN_W)
    tz = jnp.where(jnp.asarray(valid)[None, None], tz, NEG_INF)
    return jnp.concatenate([tz[:, :-1], tz[:, 1:]], axis=-1)


def _log_sigmoid(x):
    return jnp.minimum(x, 0.0) - jnp.log(1.0 + jnp.exp(-jnp.abs(x)))


def _gelu_tanh(x):
    return 0.5 * x * (1.0 + jnp.tanh(0.7978845608028654 * (x + 0.044715 * x * x * x)))


def _rglru_kernel(xr_ref, yr_ref, h0_ref, cw_ref, cb_ref, wbd_ref, ba_ref, bi_ref, lam_ref,
                  y_ref, hl_ref, xpad, a_f, b_f, a_b, b_b):
    T = xr_ref.shape[0]
    CH = 256
    xpad[0:8, :] = jnp.zeros((8, D_RG), F32)
    xpad[T + 8:T + 16, :] = jnp.zeros((8, D_RG), F32)
    xpad[8:T + 8, :] = xr_ref[...]
    a_refs = (a_f, a_b)
    b_refs = (b_f, b_b)
    for c0 in range(0, T, CH):
        xc = cb_ref[...] + cw_ref[0:1, :] * xpad[c0 + 6:c0 + 6 + CH, :]
        for i in range(1, 4):
            xc = xc + cw_ref[i:i + 1, :] * xpad[c0 + 6 + i:c0 + 6 + i + CH, :]
        xcb = xc.astype(BF16)
        for d in range(2):
            gates = []
            for g in range(2):
                halves = [jnp.dot(xcb[:, hf * 256:(hf + 1) * 256], wbd_ref[d, g, hf],
                                  preferred_element_type=F32) for hf in range(2)]
                gates.append(jnp.concatenate(halves, axis=-1))
            rg = jax.nn.sigmoid(gates[0] + ba_ref[d:d + 1, :])
            ig = jax.nn.sigmoid(gates[1] + bi_ref[d:d + 1, :])
            log_a = RG_C * rg * _log_sigmoid(lam_ref[d:d + 1, :])
            a = jnp.exp(log_a)
            b = jnp.sqrt(1.0 - a * a) * (ig * xc)
            a_refs[d][c0:c0 + CH, :] = a
            b_refs[d][c0:c0 + CH, :] = b

    sub = lax.broadcasted_iota(jnp.int32, (8, D_RG), 0)
    n_tiles = T // 8

    def scan_tile(a_ref, b_ref, t8, h, order):
        base = pl.multiple_of(t8 * 8, 8)
        a_blk = a_ref[pl.ds(base, 8), :]
        b_blk = b_ref[pl.ds(base, 8), :]
        out = b_blk
        for j in order:
            cand = a_blk * h + b_blk
            h = jnp.broadcast_to(cand[j:j + 1, :], (8, D_RG))
            out = jnp.where(sub == j, cand, out)
        b_ref[pl.ds(base, 8), :] = out
        return h

    def body(i, carry):
        hf, hb = carry
        hf = scan_tile(a_f, b_f, i, hf, range(8))
        hb = scan_tile(a_b, b_b, n_tiles - 1 - i, hb, range(7, -1, -1))
        return hf, hb

    hf0 = jnp.broadcast_to(h0_ref[0, 0:1, :], (8, D_RG))
    hb0 = jnp.broadcast_to(h0_ref[0, 1:2, :], (8, D_RG))
    hf, hb = lax.fori_loop(0, n_tiles, body, (hf0, hb0))
    hl_ref[0, 0:1, :] = hf[0:1, :]
    hl_ref[0, 1:2, :] = hb[0:1, :]
    for c0 in range(0, T, CH):
        sl = slice(c0, c0 + CH)
        y_ref[sl, :] = (b_f[sl, :] + b_b[sl, :]) * _gelu_tanh(yr_ref[sl, :])


def _rglru_call(xr, yr, h0, conv_w, conv_b, wbd, b_a, b_i, lam, seq):
    n = xr.shape[0]
    batch = n // seq
    tok = pl.BlockSpec((seq, D_RG), lambda b: (b, 0))
    full = lambda shape: pl.BlockSpec(shape, lambda b: (0,) * len(shape))
    return pl.pallas_call(
        _rglru_kernel,
        grid=(batch,),
        in_specs=[tok, tok, pl.BlockSpec((1, 2, D_RG), lambda b: (b, 0, 0)),
                  full((4, D_RG)), full((1, D_RG)), full((2, 2, 2, 256, 256)),
                  full((2, D_RG)), full((2, D_RG)), full((2, D_RG))],
        out_specs=[tok, pl.BlockSpec((1, 2, D_RG), lambda b: (b, 0, 0))],
        out_shape=[jax.ShapeDtypeStruct((n, D_RG), F32), jax.ShapeDtypeStruct((batch, 2, D_RG), F32)],
        scratch_shapes=[pltpu.VMEM((seq + 16, D_RG), F32)] + [pltpu.VMEM((seq, D_RG), F32)] * 4,
        compiler_params=_cparams("arbitrary"),
        name="rglru",
    )(xr, yr, h0, conv_w, conv_b, wbd, b_a, b_i, lam)


def _block_diag_halves(w):
    d = w.shape[0]
    w4 = w.reshape(d, 2, 4, 64, 64)
    eye = jnp.eye(4, dtype=w.dtype)
    out = jnp.einsum('dhncf,nm->dhncmf', w4, eye)
    return out.reshape(d, 2, 256, 256)


N_HP = 2 * P_HEADS


def _candidate_flat(tm):
    subf = lax.broadcasted_iota(jnp.int32, (8, tm), 0).astype(F32)
    flat = [subf + float(r1 * TOPK) for r1 in range(4)]
    flat.append(subf + 8.0)
    flat += [subf * float(TOPK) + float(r2) for r2 in range(3)]
    flat.append((subf + 8.0) * float(TOPK))
    return jnp.concatenate(flat, axis=0)


def _candidate_sums(t1, t2):
    tm = t1.shape[1]
    sub = lax.broadcasted_iota(jnp.int32, (8, tm), 0)
    ninf = -jnp.inf
    lim_a = (8, 8, 5, 4)
    vals = [jnp.where(sub < lim_a[r1], t1[r1:r1 + 1, :] + t2[0:8, :], ninf) for r1 in range(4)]
    vals.append(t1[0:1, :] + t2[8:16, :])
    for r2 in range(3):
        ok = (sub >= 4) if r2 < 2 else (sub == 4)
        vals.append(jnp.where(ok, t1[0:8, :] + t2[r2:r2 + 1, :], ninf))
    vals.append(t1[8:16, :] + t2[0:1, :])
    return jnp.concatenate(vals, axis=0)


def _staircase_rows(sel):
    rows = []
    for r1 in range(4):
        l = jnp.sum(sel[8 * r1:8 * r1 + 8, :], axis=0, keepdims=True)
        if r1 == 0:
            l = l + jnp.sum(sel[32:40, :], axis=0, keepdims=True)
        rows.append(l)
    l_mid = sel[40:48, :] + sel[48:56, :] + sel[56:64, :]
    l_hi = sel[64:72, :]
    rows += [l_mid[r1:r1 + 1, :] for r1 in range(4, 8)]
    rows += [l_hi[r1 - 8:r1 - 7, :] for r1 in range(8, 16)]
    return rows


def _batcher_pairs(n):
    pairs = []

    def merge(lo, cnt, r):
        step = r * 2
        if step < cnt:
            merge(lo, cnt, step)
            merge(lo + r, cnt, step)
            for i in range(lo + r, lo + cnt - r, step):
                pairs.append((i, i + r))
        else:
            pairs.append((lo, lo + r))

    def sort(lo, cnt):
        if cnt > 1:
            m = cnt // 2
            sort(lo, m)
            sort(lo + m, m)
            merge(lo, cnt, 1)

    sort(0, n)
    return pairs


_SORT16 = _batcher_pairs(TOPK)


def _top16_network(s):
    v = list(s)
    for i, j in _SORT16:
        v[i], v[j] = jnp.maximum(v[i], v[j]), jnp.minimum(v[i], v[j])
    for shift in (4, 2, 1):
        rolled = [pltpu.roll(x, shift, 0) for x in v]
        c = [jnp.maximum(v[k], rolled[TOPK - 1 - k]) for k in range(TOPK)]
        for d in (8, 4, 2, 1):
            for k in range(TOPK):
                if k & d == 0:
                    c[k], c[k + d] = jnp.maximum(c[k], c[k + d]), jnp.minimum(c[k], c[k + d])
        v = c
    return v


def _route_tile(sk_ref, qt_scr, s_scr, w_scr, rank_scr, t_scr, cand_scr, sel_scr, z_scr,
                a1_ref, c_ref, b2_ref, e2_ref):
    tm = qt_scr.shape[1]
    iota_f = lax.broadcasted_iota(jnp.int32, (N_KEYS, tm), 0).astype(F32)
    flat = _candidate_flat(tm)
    ninf = -jnp.inf
    for hp in range(N_HP):
        s_scr[hp] = jnp.dot(sk_ref[hp % 2], qt_scr[hp * N_KEYS:(hp + 1) * N_KEYS, :],
                            preferred_element_type=F32)

    def stage1_extract():
        for hp in range(N_HP):
            w_scr[hp] = s_scr[hp]
            rank_scr[hp] = jnp.full((N_KEYS, tm), float(TOPK), F32)

        def round1(r, carry):
            rf = jnp.asarray(r, jnp.int32).astype(F32)
            for hp in range(N_HP):
                w = w_scr[hp]
                m = jnp.max(w, axis=0, keepdims=True)
                idx = jnp.min(jnp.where(w == m, iota_f, float(N_KEYS)), axis=0, keepdims=True)
                hit = iota_f == idx
                rank_scr[hp] = jnp.where(hit, rf, rank_scr[hp])
                w_scr[hp] = jnp.where(hit, ninf, w)
                t_scr[hp, pl.ds(r, 1), :] = m
            return carry

        lax.fori_loop(0, TOPK, round1, 0)

    def stage1_network():
        lane_blocks = tm // 128
        sub = lax.broadcasted_iota(jnp.int32, (8, 128), 0)

        def body(hp, flags):
            out = []
            for lb in range(lane_blocks):
                ls = slice(lb * 128, (lb + 1) * 128)
                s = [s_scr[hp, 8 * v:8 * v + 8, ls] for v in range(N_KEYS // 8)]
                t = _top16_network(s)
                lo, hi = t[0], t[8]
                for r in range(1, 8):
                    lo = jnp.where(sub == r, t[r], lo)
                    hi = jnp.where(sub == r, t[8 + r], hi)
                t_scr[hp, 0:8, ls] = lo
                t_scr[hp, 8:16, ls] = hi
                for v in range(N_KEYS // 8):
                    rank = jnp.zeros((8, 128), F32)
                    for r in range(TOPK):
                        rank = jnp.where(t[r] > s[v], float(r + 1), rank)
                    rank_scr[hp, 8 * v:8 * v + 8, ls] = rank
                tie = flags[lb]
                for r in range(TOPK - 1):
                    tie = jnp.where(t[r] == t[r + 1], 1.0, tie)
                out.append(tie)
            return tuple(out)

        flags = lax.fori_loop(0, N_HP, body, (jnp.zeros((8, 128), F32),) * lane_blocks)
        return jnp.concatenate([f[0:1, :] for f in flags], axis=1)

    def stage2(exact):
        for h in range(P_HEADS):
            cand_scr[h] = _candidate_sums(t_scr[2 * h], t_scr[2 * h + 1])
            sel_scr[h] = jnp.zeros((72, tm), F32)

        def round2(r, carry):
            zs, b0s = carry
            zs_new, b0s_new = [], []
            for h in range(P_HEADS):
                w = cand_scr[h]
                m = jnp.max(w, axis=0, keepdims=True)
                if exact:
                    f = jnp.min(jnp.where(w == m, flat, 1e9), axis=0, keepdims=True)
                    hit = flat == f
                else:
                    hit = w == m
                sel_scr[h] = jnp.where(hit, 1.0, sel_scr[h])
                cand_scr[h] = jnp.where(hit, ninf, w)
                b0 = jnp.where(r == 0, m, b0s[h])
                zs_new.append(zs[h] + jnp.exp(m - b0))
                b0s_new.append(b0)
            return tuple(zs_new), tuple(b0s_new)

        zero_row = jnp.zeros((1, tm), F32)
        zs, _ = lax.fori_loop(0, TOPK, round2, ((zero_row,) * P_HEADS, (zero_row,) * P_HEADS))
        for h in range(P_HEADS):
            z_scr[h:h + 1, :] = zs[h]

    excess = stage1_network()
    stage2(False)
    for hp in range(N_HP):
        cnt = jnp.sum(jnp.where(rank_scr[hp] < float(TOPK), 1.0, 0.0), axis=0, keepdims=True)
        excess = jnp.maximum(excess, jnp.abs(cnt - float(TOPK)))
    for h in range(P_HEADS):
        cnt = jnp.sum(sel_scr[h], axis=0, keepdims=True)
        excess = jnp.maximum(excess, jnp.abs(cnt - float(TOPK)))

    @pl.when(jnp.max(excess) > 0.0)
    def _():
        stage1_extract()
        stage2(True)

    for h in range(P_HEADS):
        l_rows = _staircase_rows(sel_scr[h])
        rank1 = rank_scr[2 * h]
        a1 = jnp.zeros((N_KEYS, tm), F32)
        for r1 in range(TOPK):
            a1 = jnp.where(rank1 == float(r1), l_rows[r1], a1)
        a1_ref[h] = a1
        c_ref[h] = jnp.exp(s_scr[2 * h] - t_scr[2 * h, 0:1, :]) * (0.5 / z_scr[h:h + 1, :])
        b2_ref[h] = rank_scr[2 * h + 1].astype(BF16)
        e2_ref[h] = jnp.exp(s_scr[2 * h + 1] - t_scr[2 * h + 1, 0:1, :]).astype(BF16)


def _mixffn_kernel(x_ref, att_ref, rg_ref, mod_ref, wo_ref, nw_ref, wqt_ref, sk_ref,
                   x1_ref, h2t_ref, a1_ref, c_ref, b2_ref, e2_ref,
                   qt_scr, s_scr, w_scr, rank_scr, t_scr, cand_scr, sel_scr, z_scr):
    o = (jnp.dot(att_ref[...].astype(BF16), wo_ref[0:D_ATT, :], preferred_element_type=F32)
         + jnp.dot(rg_ref[...].astype(BF16), wo_ref[D_ATT:, :], preferred_element_type=F32))
    x1 = x_ref[...] + mod_ref[0, 2:3, :] * o
    x1_ref[...] = x1
    h2 = _rmsnorm(x1, nw_ref[...]) * (1.0 + mod_ref[0, 4:5, :]) + mod_ref[0, 3:4, :]
    h2t = h2.T.astype(BF16)
    h2t_ref[...] = h2t
    qt_scr[...] = jnp.dot(wqt_ref[...], h2t, preferred_element_type=F32).astype(BF16)
    _route_tile(sk_ref, qt_scr, s_scr, w_scr, rank_scr, t_scr, cand_scr, sel_scr, z_scr,
                a1_ref, c_ref, b2_ref, e2_ref)


def _mixffn_call(x2d, att, rg, mod3, mod_base, tiles_per_batch, w_out_bf, norm_w, w_qt_bf, sk_bf, tm):
    n = x2d.shape[0]
    tok = lambda t: (t, 0)
    route = pl.BlockSpec((P_HEADS, N_KEYS, tm), lambda t: (0, 0, t))
    route_shape = jax.ShapeDtypeStruct((P_HEADS, N_KEYS, n), F32)
    return pl.pallas_call(
        _mixffn_kernel,
        grid=(n // tm,),
        in_specs=[pl.BlockSpec((tm, D), tok),
                  pl.BlockSpec((tm, D_ATT), tok),
                  pl.BlockSpec((tm, D_RG), tok),
                  pl.BlockSpec((1, 6, D), lambda t: (mod_base + t // tiles_per_batch, 0, 0)),
                  pl.BlockSpec((D, D), lambda t: (0, 0)),
                  pl.BlockSpec((1, D), lambda t: (0, 0)),
                  pl.BlockSpec((2 * P_HEADS * N_KEYS, D), lambda t: (0, 0)),
                  pl.BlockSpec((2, N_KEYS, N_KEYS), lambda t: (0, 0, 0))],
        out_specs=[pl.BlockSpec((tm, D), tok),
                   pl.BlockSpec((D, tm), lambda t: (0, t)),
                   route, route, route, route],
        out_shape=[jax.ShapeDtypeStruct((n, D), F32), jax.ShapeDtypeStruct((D, n), BF16),
                   route_shape, route_shape,
                   jax.ShapeDtypeStruct((P_HEADS, N_KEYS, n), BF16), jax.ShapeDtypeStruct((P_HEADS, N_KEYS, n), BF16)],
        scratch_shapes=[pltpu.VMEM((2 * P_HEADS * N_KEYS, tm), BF16)]
                       + [pltpu.VMEM((N_HP, N_KEYS, tm), F32)] * 3
                       + [pltpu.VMEM((N_HP, TOPK, tm), F32)]
                       + [pltpu.VMEM((P_HEADS, 72, tm), F32)] * 2
                       + [pltpu.VMEM((P_HEADS, tm), F32)],
        compiler_params=_cparams("arbitrary"),
        name="mixffn",
    )(x2d, att, rg, mod3, w_out_bf, norm_w, w_qt_bf, sk_bf)


def _expert_prep_kernel(u_ref, v_ref, ub_ref, vt_ref):
    ub_ref[...] = u_ref[...].astype(BF16)
    vt_ref[...] = v_ref[...].T.astype(BF16)


def _expert_prep_call(u, v):
    te = 512
    return pl.pallas_call(
        _expert_prep_kernel,
        grid=(N_EXPERTS // te,),
        in_specs=[pl.BlockSpec((te, D), lambda j: (j, 0)), pl.BlockSpec((te, D), lambda j: (j, 0))],
        out_specs=[pl.BlockSpec((te, D), lambda j: (j, 0)), pl.BlockSpec((D, te), lambda j: (0, j))],
        out_shape=[jax.ShapeDtypeStruct((N_EXPERTS, D), BF16), jax.ShapeDtypeStruct((D, N_EXPERTS), BF16)],
        compiler_params=_cparams("arbitrary"),
        name="expert_prep",
    )(u, v)


def _peer_kernel(h2t_ref, u_ref, vt_ref, a1_ref, c_ref, b2_ref, e2_ref, x1_ref, mod_ref, nw_ref,
                 y_ref, acc, g_scr, *, i1_per_step):
    j = pl.program_id(1)
    tm = acc.shape[1]

    @pl.when(j == 0)
    def _():
        acc[...] = jnp.zeros_like(acc)

    grp = 4
    one = jnp.ones((), BF16)
    zero = jnp.zeros((), BF16)
    for i0 in range(0, i1_per_step, grp):
        for l in range(tm // 128):
            ls = slice(l * 128, (l + 1) * 128)
            g = [None] * grp
            for h in range(P_HEADS):
                b2 = b2_ref[h, :, ls]
                e2 = e2_ref[h, :, ls]
                for k in range(grp):
                    a1 = a1_ref[h, i0 + k:i0 + k + 1, ls].astype(BF16)
                    ch = c_ref[h, i0 + k:i0 + k + 1, ls].astype(BF16)
                    m = jnp.minimum(jnp.maximum(a1 - b2, zero), one)
                    w = (ch * e2) * m
                    g[k] = w if g[k] is None else g[k] + w
            for k in range(grp):
                g_scr[(i0 + k) * N_KEYS:(i0 + k + 1) * N_KEYS, ls] = g[k]
    st = jnp.dot(u_ref[...], h2t_ref[...], preferred_element_type=F32)
    act = st * (1.0 + lax.erf(st * 0.7071067811865476))
    at = act.astype(BF16) * g_scr[...]
    acc[...] += jnp.dot(vt_ref[...], at, preferred_element_type=F32)

    @pl.when(j == pl.num_programs(1) - 1)
    def _():
        x2 = x1_ref[...] + mod_ref[0, 5:6, :] * acc[...].T
        y_ref[...] = _rmsnorm(x2, nw_ref[...])


def _peer_call(h2t, u_bf, vt_bf, a1, c, b2, e2, x1, mod3, mod_base, tiles_per_batch, norm_f_w, tm, te):
    n = x1.shape[0]
    ips = te // N_KEYS
    route_lo = pl.BlockSpec((P_HEADS, ips, tm), lambda t, j: (0, j, t))
    route_full = pl.BlockSpec((P_HEADS, N_KEYS, tm), lambda t, j: (0, 0, t))
    return pl.pallas_call(
        functools.partial(_peer_kernel, i1_per_step=ips),
        grid=(n // tm, N_EXPERTS // te),
        in_specs=[pl.BlockSpec((D, tm), lambda t, j: (0, t)),
                  pl.BlockSpec((te, D), lambda t, j: (j, 0)),
                  pl.BlockSpec((D, te), lambda t, j: (0, j)),
                  route_lo, route_lo, route_full, route_full,
                  pl.BlockSpec((tm, D), lambda t, j: (t, 0)),
                  pl.BlockSpec((1, 6, D), lambda t, j: (mod_base + t // tiles_per_batch, 0, 0)),
                  pl.BlockSpec((1, D), lambda t, j: (0, 0))],
        out_specs=pl.BlockSpec((tm, D), lambda t, j: (t, 0)),
        out_shape=jax.ShapeDtypeStruct((n, D), F32),
        scratch_shapes=[pltpu.VMEM((D, tm), F32), pltpu.VMEM((te, tm), BF16)],
        compiler_params=_cparams("arbitrary", "arbitrary"),
        name="peer",
    )(h2t, u_bf, vt_bf, a1, c, b2, e2, x1, mod3, norm_f_w)


def _path(x, mod3, mod_base, per_batch_mod, weights, attn_fn, h0, emit_cache):
    (norm_mix_w, w_in_bf, rg_params, w_out_bf, norm_ffn_w, w_qt_bf, sk_bf, u_bf, vt_bf, norm_f_w) = weights
    bsz, seq, _ = x.shape
    n = bsz * seq
    x2d = x.reshape(n, D)
    tpb = lambda tm: (seq // tm) if per_batch_mod else n
    q, kb, vb, xr, yr, *kv_f32 = _inproj_call(x2d, mod3, mod_base, tpb(512), norm_mix_w, w_in_bf, emit_cache)
    att = attn_fn(q, kb, vb)
    rg_out, h_last = _rglru_call(xr, yr, h0, *rg_params, seq)
    tm_mix = 256
    x1, h2t, a1, c, b2, e2 = _mixffn_call(x2d, att, rg_out, mod3, mod_base, tpb(tm_mix), w_out_bf,
                                          norm_ffn_w, w_qt_bf, sk_bf, tm_mix)
    tm_peer = 1024
    y = _peer_call(h2t, u_bf, vt_bf, a1, c, b2, e2, x1, mod3, mod_base, tpb(tm_peer), norm_f_w,
                   tm_peer, 1024)
    return y.reshape(bsz, seq, D), kv_f32, h_last


def kernel(x_prompt, x_sample, c, cache_k, cache_v, state_rglru, c_ctx, w_mod, b_mod, norm_mix_w, w_in,
           rpb, conv_w, conv_b, rg_w_a, rg_b_a, rg_w_i, rg_b_i, rg_lambda, w_out, norm_ffn_w,
           peer_w_q, peer_sub_keys, peer_u, peer_v, norm_f_w):
    depth = w_mod.shape[0]
    assert depth == 1, "single-layer problem"
    l = 0
    bp, sp, _ = x_prompt.shape
    bs, ss, _ = x_sample.shape

    n_rows = 16
    cvec = jnp.concatenate([c_ctx[None, :], c, jnp.zeros((n_rows - 1 - bs, D), F32)], axis=0)
    mod3 = _mod_call(cvec, w_mod[l], b_mod[l][None, :]).reshape(n_rows, 6, D)

    wbd = jnp.stack([_block_diag_halves(rg_w_a[l]), _block_diag_halves(rg_w_i[l])], axis=1).astype(BF16)
    rg_params = (conv_w[l], conv_b[l][None, :], wbd, rg_b_a[l], rg_b_i[l], rg_lambda[l])
    u_bf, vt_bf = _expert_prep_call(peer_u[l], peer_v[l])
    weights = (norm_mix_w[l][None, :], w_in[l].astype(BF16), rg_params, w_out[l].astype(BF16),
               norm_ffn_w[l][None, :], peer_w_q[l].T.astype(BF16), peer_sub_keys[l].astype(BF16),
               u_bf, vt_bf, norm_f_w[None, :])

    ctx_attn = lambda q, kb, vb: _ctx_attn_call(q, kb, vb, sp)
    h0_p = jnp.zeros((bp, 2, D_RG), F32)
    y_prompt, (k_p, v_p), h_last = _path(x_prompt, mod3, 0, False, weights, ctx_attn, h0_p, True)

    tz = _bias_toeplitz(rpb[l])
    ck = cache_k[:, l].reshape(-1, D_ATT).astype(BF16)
    cv = cache_v[:, l].reshape(-1, D_ATT).astype(BF16)
    nbr_attn = lambda q, kb, vb: _nbr_attn_call(q, kb, vb, ck, cv, tz, bs, ss)
    y_sample, _, _ = _path(x_sample, mod3, 1, True, weights, nbr_attn, state_rglru[:, l], False)

    new_k = k_p.reshape(bp, 1, sp, N_HEADS, HEAD_DIM)
    new_v = v_p.reshape(bp, 1, sp, N_HEADS, HEAD_DIM)
    new_h = h_last.reshape(bp, 1, 2, D_RG)
    return (y_prompt, y_sample, new_k, new_v, new_h)
```

```python
import functools

import numpy as np
import jax
import jax.numpy as jnp
from jax import lax
from jax.experimental import pallas as pl
from jax.experimental.pallas import tpu as pltpu

F32 = jnp.float32
BF16 = jnp.bfloat16

D = 1024
D_ATT = 512
N_HEADS = 8
HEAD_DIM = 64
GRID_W = 64
WIN_H = 8
WIN_W = 16
D_RG = 512
D_IN = 3 * D_ATT + 2 * D_RG
N_KEYS = 128
P_HEADS = 8
TOPK = 16
N_EXPERTS = N_KEYS * N_KEYS
RMS_EPS = 1e-6
NEG_INF = -1e30
RG_C = 8.0
ATT_SCALE = HEAD_DIM ** -0.5

VMEM_LIMIT = 56 * 1024 * 1024

_NT = (((1,), (1,)), ((), ()))


def _cparams(*sem, flags=None):
    return pltpu.CompilerParams(dimension_semantics=sem, vmem_limit_bytes=VMEM_LIMIT, flags=flags)


def _rmsnorm(x, w):
    ms = jnp.mean(x * x, axis=-1, keepdims=True)
    return x * lax.rsqrt(ms + RMS_EPS) * w


def _mod_kernel(c_ref, w_ref, b_ref, o_ref):
    cv = c_ref[...]
    s = cv * jax.nn.sigmoid(cv)
    o_ref[...] = jnp.dot(s.astype(BF16), w_ref[...].astype(BF16),
                         preferred_element_type=F32) + b_ref[...]


def _mod_call(cvec, w_mod, b_mod):
    rows = cvec.shape[0]
    tn = 1536
    return pl.pallas_call(
        _mod_kernel,
        grid=(6 * D // tn,),
        in_specs=[pl.BlockSpec((rows, D), lambda j: (0, 0)),
                  pl.BlockSpec((D, tn), lambda j: (0, j)),
                  pl.BlockSpec((1, tn), lambda j: (0, j))],
        out_specs=pl.BlockSpec((rows, tn), lambda j: (0, j)),
        out_shape=jax.ShapeDtypeStruct((rows, 6 * D), F32),
        compiler_params=_cparams("arbitrary"),
        name="mod",
    )(cvec, w_mod, b_mod)


def _inproj_kernel(x_ref, mod_ref, nw_ref, w_ref, *out_refs, emit_f32_kv):
    if emit_f32_kv:
        q_ref, kb_ref, vb_ref, xr_ref, yr_ref, kf_ref, vf_ref = out_refs
    else:
        q_ref, kb_ref, vb_ref, xr_ref, yr_ref = out_refs
    x = x_ref[...]
    h = _rmsnorm(x, nw_ref[...]) * (1.0 + mod_ref[0, 1:2, :]) + mod_ref[0, 0:1, :]
    p = jnp.dot(h.astype(BF16), w_ref[...], preferred_element_type=F32)
    q_ref[...] = (p[:, 0:D_ATT] * ATT_SCALE).astype(BF16)
    k = p[:, D_ATT:2 * D_ATT]
    v = p[:, 2 * D_ATT:3 * D_ATT]
    kb_ref[...] = k.astype(BF16)
    vb_ref[...] = v.astype(BF16)
    xr_ref[...] = p[:, 3 * D_ATT:3 * D_ATT + D_RG]
    yr_ref[...] = p[:, 3 * D_ATT + D_RG:]
    if emit_f32_kv:
        kf_ref[...] = k
        vf_ref[...] = v


def _inproj_call(x2d, mod3, mod_base, tiles_per_batch, norm_w, w_in_bf, emit_f32_kv):
    n = x2d.shape[0]
    tm = 512
    row = lambda t: (mod_base + t // tiles_per_batch, 0, 0)
    tok = lambda t: (t, 0)
    shp = lambda w, dt: jax.ShapeDtypeStruct((n, w), dt)
    out_specs = [pl.BlockSpec((tm, D_ATT), tok)] * 3 + [pl.BlockSpec((tm, D_RG), tok)] * 2
    out_shape = [shp(D_ATT, BF16)] * 3 + [shp(D_RG, F32)] * 2
    if emit_f32_kv:
        out_specs += [pl.BlockSpec((tm, D_ATT), tok)] * 2
        out_shape += [shp(D_ATT, F32)] * 2
    return pl.pallas_call(
        functools.partial(_inproj_kernel, emit_f32_kv=emit_f32_kv),
        grid=(n // tm,),
        in_specs=[pl.BlockSpec((tm, D), tok),
                  pl.BlockSpec((1, 6, D), row),
                  pl.BlockSpec((1, D), lambda t: (0, 0)),
                  pl.BlockSpec((D, D_IN), lambda t: (0, 0))],
        out_specs=out_specs,
        out_shape=out_shape,
        compiler_params=_cparams("arbitrary"),
        name="inproj",
    )(x2d, mod3, norm_w, w_in_bf)


def _ctx_attn_kernel(q_ref, k_ref, v_ref, o_ref):
    for h in range(N_HEADS):
        sl = slice(h * HEAD_DIM, (h + 1) * HEAD_DIM)
        s = lax.dot_general(q_ref[:, sl], k_ref[:, sl], _NT, preferred_element_type=F32)
        m = jnp.max(s, axis=-1, keepdims=True)
        p = jnp.exp(s - m)
        l = jnp.sum(p, axis=-1, keepdims=True)
        o = jnp.dot(p.astype(BF16), v_ref[:, sl], preferred_element_type=F32)
        o_ref[:, sl] = o / l


def _ctx_attn_call(q, kb, vb, seq):
    n = q.shape[0]
    blk = pl.BlockSpec((seq, D_ATT), lambda b: (b, 0))
    return pl.pallas_call(
        _ctx_attn_kernel,
        grid=(n // seq,),
        in_specs=[blk, blk, blk],
        out_specs=blk,
        out_shape=jax.ShapeDtypeStruct((n, D_ATT), F32),
        compiler_params=_cparams("arbitrary"),
        name="ctx_attn",
    )(q, kb, vb)


ROWS_PER_STEP = 4
WIN_UNION = 12


def _nbr_attn_kernel(q_ref, k_ref, v_ref, ck_ref, cv_ref, tz_ref, o_ref, *, rows):
    g = pl.program_id(0)
    n_loc = WIN_UNION * GRID_W
    first = jnp.clip(g * ROWS_PER_STEP - WIN_H // 2, 0, rows - WIN_H)
    kb = jnp.minimum((first >> 1) << 1, rows - WIN_UNION)
    row0 = pl.multiple_of(kb * GRID_W, 2 * GRID_W)
    lane = lax.broadcasted_iota(jnp.int32, (1, 2 * GRID_W), 1)
    d_idx, pen = [], []
    for i in range(ROWS_PER_STEP):
        r = g * ROWS_PER_STEP + i
        start = jnp.clip(r - WIN_H // 2, 0, rows - WIN_H)
        d_i, pen_i = [], []
        for p in range(WIN_UNION // 2):
            kr = kb + 2 * p
            d_i.append(jnp.clip(kr - r + (WIN_H - 1), 0, 2 * WIN_H - 3))
            in0 = (kr >= start) & (kr < start + WIN_H)
            in1 = (kr + 1 >= start) & (kr + 1 < start + WIN_H)
            pen_i.append(jnp.where(lane < GRID_W, jnp.where(in0, 0.0, NEG_INF), jnp.where(in1, 0.0, NEG_INF)))
        d_idx.append(d_i)
        pen.append(pen_i)
    outs = []
    for h in range(N_HEADS):
        sl = slice(h * HEAD_DIM, (h + 1) * HEAD_DIM)
        qg = q_ref[:, sl]
        bias = jnp.concatenate(
            [jnp.concatenate([tz_ref[h, d_idx[i][p]] + pen[i][p] for p in range(WIN_UNION // 2)], axis=-1)
             for i in range(ROWS_PER_STEP)], axis=0)
        s_loc = lax.dot_general(qg, k_ref[pl.ds(row0, n_loc), sl], _NT, preferred_element_type=F32) + bias
        s_ctx = lax.dot_general(qg, ck_ref[:, sl], _NT, preferred_element_type=F32)
        m = jnp.maximum(jnp.max(s_loc, axis=-1, keepdims=True), jnp.max(s_ctx, axis=-1, keepdims=True))
        p_loc = jnp.exp(s_loc - m)
        p_ctx = jnp.exp(s_ctx - m)
        l = jnp.sum(p_loc, axis=-1, keepdims=True) + jnp.sum(p_ctx, axis=-1, keepdims=True)
        o = (jnp.dot(p_loc.astype(BF16), v_ref[pl.ds(row0, n_loc), sl], preferred_element_type=F32)
             + jnp.dot(p_ctx.astype(BF16), cv_ref[:, sl], preferred_element_type=F32))
        outs.append(o / l)
    o_ref[...] = jnp.concatenate(outs, axis=-1)


def _nbr_attn_call(q, kb, vb, ck, cv, tz, batch, seq):
    rows = seq // GRID_W
    assert rows % 2 == 0 and rows >= WIN_UNION and rows % ROWS_PER_STEP == 0
    groups = rows // ROWS_PER_STEP
    past = ck.shape[0] // batch
    n = q.shape[0]
    qblk = pl.BlockSpec((GRID_W * ROWS_PER_STEP, D_ATT), lambda g, b: (b * groups + g, 0))
    kvblk = pl.BlockSpec((seq, D_ATT), lambda g, b: (b, 0))
    cblk = pl.BlockSpec((past, D_ATT), lambda g, b: (b, 0))
    return pl.pallas_call(
        functools.partial(_nbr_attn_kernel, rows=rows),
        grid=(groups, batch),
        in_specs=[qblk, kvblk, kvblk, cblk, cblk,
                  pl.BlockSpec(tz.shape, lambda g, b: (0, 0, 0, 0))],
        out_specs=qblk,
        out_shape=jax.ShapeDtypeStruct((n, D_ATT), F32),
        compiler_params=_cparams("arbitrary", "arbitrary"),
        name="nbr_attn",
    )(q, kb, vb, ck, cv, tz)


def _bias_toeplitz(rpb_l):
    qc = np.arange(GRID_W)
    off = qc[None, :] - qc[:, None] + (WIN_W - 1)
    place = (off[None, :, :] == np.arange(2 * WIN_W - 1)[:, None, None]).astype(np.float32)
    tz = jnp.einsum('hdx,xqk->hdqk', rpb_l, jnp.asarray(place), precision=lax.Precision.HIGHEST)
    cs = np.clip(qc - WIN_W // 2, 0, GRID_W - WIN_W)
    valid = (qc[None, :] >= cs[:, None]) & (qc[None, :] < cs[:, None] + WIN_W)
    tz = jnp.where(jnp.asarray(valid)[None, None], tz, NEG_INF)
    return jnp.concatenate([tz[:, :-1], tz[:, 1:]], axis=-1)


def _log_sigmoid(x):
    return jnp.minimum(x, 0.0) - jnp.log(1.0 + jnp.exp(-jnp.abs(x)))


def _gelu_tanh(x):
    return 0.5 * x * (1.0 + jnp.tanh(0.7978845608028654 * (x + 0.044715 * x * x * x)))


def _rglru_kernel(xr_ref, yr_ref, h0_ref, cw_ref, cb_ref, wbd_ref, ba_ref, bi_ref, lam_ref,
                  y_ref, hl_ref, xpad, a_f, b_f, a_b, b_b):
    T = xr_ref.shape[0]
    CH = 256
    xpad[0:8, :] = jnp.zeros((8, D_RG), F32)
    xpad[T + 8:T + 16, :] = jnp.zeros((8, D_RG), F32)
    xpad[8:T + 8, :] = xr_ref[...]
    a_refs = (a_f, a_b)
    b_refs = (b_f, b_b)
    for c0 in range(0, T, CH):
        xc = cb_ref[...] + cw_ref[0:1, :] * xpad[c0 + 6:c0 + 6 + CH, :]
        for i in range(1, 4):
            xc = xc + cw_ref[i:i + 1, :] * xpad[c0 + 6 + i:c0 + 6 + i + CH, :]
        xcb = xc.astype(BF16)
        for d in range(2):
            gates = []
            for g in range(2):
                halves = [jnp.dot(xcb[:, hf * 256:(hf + 1) * 256], wbd_ref[d, g, hf],
                                  preferred_element_type=F32) for hf in range(2)]
                gates.append(jnp.concatenate(halves, axis=-1))
            rg = jax.nn.sigmoid(gates[0] + ba_ref[d:d + 1, :])
            ig = jax.nn.sigmoid(gates[1] + bi_ref[d:d + 1, :])
            log_a = RG_C * rg * _log_sigmoid(lam_ref[d:d + 1, :])
            a = jnp.exp(log_a)
            b = jnp.sqrt(1.0 - a * a) * (ig * xc)
            a_refs[d][c0:c0 + CH, :] = a
            b_refs[d][c0:c0 + CH, :] = b

    sub = lax.broadcasted_iota(jnp.int32, (8, D_RG), 0)
    n_tiles = T // 8

    def scan_tile(a_ref, b_ref, t8, h, order):
        base = pl.multiple_of(t8 * 8, 8)
        a_blk = a_ref[pl.ds(base, 8), :]
        b_blk = b_ref[pl.ds(base, 8), :]
        out = b_blk
        for j in order:
            cand = a_blk * h + b_blk
            h = jnp.broadcast_to(cand[j:j + 1, :], (8, D_RG))
            out = jnp.where(sub == j, cand, out)
        b_ref[pl.ds(base, 8), :] = out
        return h

    def body(i, carry):
        hf, hb = carry
        hf = scan_tile(a_f, b_f, i, hf, range(8))
        hb = scan_tile(a_b, b_b, n_tiles - 1 - i, hb, range(7, -1, -1))
        return hf, hb

    hf0 = jnp.broadcast_to(h0_ref[0, 0:1, :], (8, D_RG))
    hb0 = jnp.broadcast_to(h0_ref[0, 1:2, :], (8, D_RG))
    hf, hb = lax.fori_loop(0, n_tiles, body, (hf0, hb0))
    hl_ref[0, 0:1, :] = hf[0:1, :]
    hl_ref[0, 1:2, :] = hb[0:1, :]
    for c0 in range(0, T, CH):
        sl = slice(c0, c0 + CH)
        y_ref[sl, :] = (b_f[sl, :] + b_b[sl, :]) * _gelu_tanh(yr_ref[sl, :])


def _rglru_call(xr, yr, h0, conv_w, conv_b, wbd, b_a, b_i, lam, seq):
    n = xr.shape[0]
    batch = n // seq
    tok = pl.BlockSpec((seq, D_RG), lambda b: (b, 0))
    full = lambda shape: pl.BlockSpec(shape, lambda b: (0,) * len(shape))
    return pl.pallas_call(
        _rglru_kernel,
        grid=(batch,),
        in_specs=[tok, tok, pl.BlockSpec((1, 2, D_RG), lambda b: (b, 0, 0)),
                  full((4, D_RG)), full((1, D_RG)), full((2, 2, 2, 256, 256)),
                  full((2, D_RG)), full((2, D_RG)), full((2, D_RG))],
        out_specs=[tok, pl.BlockSpec((1, 2, D_RG), lambda b: (b, 0, 0))],
        out_shape=[jax.ShapeDtypeStruct((n, D_RG), F32), jax.ShapeDtypeStruct((batch, 2, D_RG), F32)],
        scratch_shapes=[pltpu.VMEM((seq + 16, D_RG), F32)] + [pltpu.VMEM((seq, D_RG), F32)] * 4,
        compiler_params=_cparams("arbitrary"),
        name="rglru",
    )(xr, yr, h0, conv_w, conv_b, wbd, b_a, b_i, lam)


def _block_diag_halves(w):
    d = w.shape[0]
    w4 = w.reshape(d, 2, 4, 64, 64)
    eye = jnp.eye(4, dtype=w.dtype)
    out = jnp.einsum('dhncf,nm->dhncmf', w4, eye)
    return out.reshape(d, 2, 256, 256)


N_HP = 2 * P_HEADS


def _candidate_flat(tm):
    subf = lax.broadcasted_iota(jnp.int32, (8, tm), 0).astype(F32)
    flat = [subf + float(r1 * TOPK) for r1 in range(4)]
    flat.append(subf + 8.0)
    flat += [subf * float(TOPK) + float(r2) for r2 in range(3)]
    flat.append((subf + 8.0) * float(TOPK))
    return jnp.concatenate(flat, axis=0)


def _candidate_sums(t1, t2):
    tm = t1.shape[1]
    sub = lax.broadcasted_iota(jnp.int32, (8, tm), 0)
    ninf = -jnp.inf
    lim_a = (8, 8, 5, 4)
    vals = [jnp.where(sub < lim_a[r1], t1[r1:r1 + 1, :] + t2[0:8, :], ninf) for r1 in range(4)]
    vals.append(t1[0:1, :] + t2[8:16, :])
    for r2 in range(3):
        ok = (sub >= 4) if r2 < 2 else (sub == 4)
        vals.append(jnp.where(ok, t1[0:8, :] + t2[r2:r2 + 1, :], ninf))
    vals.append(t1[8:16, :] + t2[0:1, :])
    return jnp.concatenate(vals, axis=0)


def _staircase_rows(sel):
    rows = []
    for r1 in range(4):
        l = jnp.sum(sel[8 * r1:8 * r1 + 8, :], axis=0, keepdims=True)
        if r1 == 0:
            l = l + jnp.sum(sel[32:40, :], axis=0, keepdims=True)
        rows.append(l)
    l_mid = sel[40:48, :] + sel[48:56, :] + sel[56:64, :]
    l_hi = sel[64:72, :]
    rows += [l_mid[r1:r1 + 1, :] for r1 in range(4, 8)]
    rows += [l_hi[r1 - 8:r1 - 7, :] for r1 in range(8, 16)]
    return rows


def _batcher_pairs(n):
    pairs = []

    def merge(lo, cnt, r):
        step = r * 2
        if step < cnt:
            merge(lo, cnt, step)
            merge(lo + r, cnt, step)
            for i in range(lo + r, lo + cnt - r, step):
                pairs.append((i, i + r))
        else:
            pairs.append((lo, lo + r))

    def sort(lo, cnt):
        if cnt > 1:
            m = cnt // 2
            sort(lo, m)
            sort(lo + m, m)
            merge(lo, cnt, 1)

    sort(0, n)
    return pairs


_SORT16 = _batcher_pairs(TOPK)


def _top16_network(s):
    v = list(s)
    for i, j in _SORT16:
        v[i], v[j] = jnp.maximum(v[i], v[j]), jnp.minimum(v[i], v[j])
    for shift in (4, 2, 1):
        rolled = [pltpu.roll(x, shift, 0) for x in v]
        c = [jnp.maximum(v[k], rolled[TOPK - 1 - k]) for k in range(TOPK)]
        for d in (8, 4, 2, 1):
            for k in range(TOPK):
                if k & d == 0:
                    c[k], c[k + d] = jnp.maximum(c[k], c[k + d]), jnp.minimum(c[k], c[k + d])
        v = c
    return v


def _route_tile(sk_ref, qt_scr, s_scr, w_scr, rank_scr, t_scr, cand_scr, sel_scr, z_scr,
                a1_ref, c_ref, b2_ref, e2_ref):
    tm = qt_scr.shape[1]
    iota_f = lax.broadcasted_iota(jnp.int32, (N_KEYS, tm), 0).astype(F32)
    flat = _candidate_flat(tm)
    ninf = -jnp.inf
    for hp in range(N_HP):
        s_scr[hp] = jnp.dot(sk_ref[hp % 2], qt_scr[hp * N_KEYS:(hp + 1) * N_KEYS, :],
                            preferred_element_type=F32)

    def stage1_extract():
        for hp in range(N_HP):
            w_scr[hp] = s_scr[hp]
            rank_scr[hp] = jnp.full((N_KEYS, tm), float(TOPK), F32)

        def round1(r, carry):
            rf = jnp.asarray(r, jnp.int32).astype(F32)
            for hp in range(N_HP):
                w = w_scr[hp]
                m = jnp.max(w, axis=0, keepdims=True)
                idx = jnp.min(jnp.where(w == m, iota_f, float(N_KEYS)), axis=0, keepdims=True)
                hit = iota_f == idx
                rank_scr[hp] = jnp.where(hit, rf, rank_scr[hp])
                w_scr[hp] = jnp.where(hit, ninf, w)
                t_scr[hp, pl.ds(r, 1), :] = m
            return carry

        lax.fori_loop(0, TOPK, round1, 0)

    def stage1_network():
        lane_blocks = tm // 128
        sub = lax.broadcasted_iota(jnp.int32, (8, 128), 0)

        def body(hp, flags):
            out = []
            for lb in range(lane_blocks):
                ls = slice(lb * 128, (lb + 1) * 128)
                s = [s_scr[hp, 8 * v:8 * v + 8, ls] for v in range(N_KEYS // 8)]
                t = _top16_network(s)
                lo, hi = t[0], t[8]
                for r in range(1, 8):
                    lo = jnp.where(sub == r, t[r], lo)
                    hi = jnp.where(sub == r, t[8 + r], hi)
                t_scr[hp, 0:8, ls] = lo
                t_scr[hp, 8:16, ls] = hi
                for v in range(N_KEYS // 8):
                    rank = jnp.zeros((8, 128), F32)
                    for r in range(TOPK):
                        rank = jnp.where(t[r] > s[v], float(r + 1), rank)
                    rank_scr[hp, 8 * v:8 * v + 8, ls] = rank
                tie = flags[lb]
                for r in range(TOPK - 1):
                    tie = jnp.where(t[r] == t[r + 1], 1.0, tie)
                out.append(tie)
            return tuple(out)

        flags = lax.fori_loop(0, N_HP, body, (jnp.zeros((8, 128), F32),) * lane_blocks)
        return jnp.concatenate([f[0:1, :] for f in flags], axis=1)

    def stage2(exact):
        for h in range(P_HEADS):
            cand_scr[h] = _candidate_sums(t_scr[2 * h], t_scr[2 * h + 1])
            sel_scr[h] = jnp.zeros((72, tm), F32)

        def round2(r, carry):
            zs, b0s = carry
            zs_new, b0s_new = [], []
            for h in range(P_HEADS):
                w = cand_scr[h]
                m = jnp.max(w, axis=0, keepdims=True)
                if exact:
                    f = jnp.min(jnp.where(w == m, flat, 1e9), axis=0, keepdims=True)
                    hit = flat == f
                else:
                    hit = w == m
                sel_scr[h] = jnp.where(hit, 1.0, sel_scr[h])
                cand_scr[h] = jnp.where(hit, ninf, w)
                b0 = jnp.where(r == 0, m, b0s[h])
                zs_new.append(zs[h] + jnp.exp(m - b0))
                b0s_new.append(b0)
            return tuple(zs_new), tuple(b0s_new)

        zero_row = jnp.zeros((1, tm), F32)
        zs, _ = lax.fori_loop(0, TOPK, round2, ((zero_row,) * P_HEADS, (zero_row,) * P_HEADS))
        for h in range(P_HEADS):
            z_scr[h:h + 1, :] = zs[h]

    excess = stage1_network()
    stage2(False)
    for hp in range(N_HP):
        cnt = jnp.sum(jnp.where(rank_scr[hp] < float(TOPK), 1.0, 0.0), axis=0, keepdims=True)
        excess = jnp.maximum(excess, jnp.abs(cnt - float(TOPK)))
    for h in range(P_HEADS):
        cnt = jnp.sum(sel_scr[h], axis=0, keepdims=True)
        excess = jnp.maximum(excess, jnp.abs(cnt - float(TOPK)))

    @pl.when(jnp.max(excess) > 0.0)
    def _():
        stage1_extract()
        stage2(True)

    for h in range(P_HEADS):
        l_rows = _staircase_rows(sel_scr[h])
        rank1 = rank_scr[2 * h]
        a1 = jnp.zeros((N_KEYS, tm), F32)
        for r1 in range(TOPK):
            a1 = jnp.where(rank1 == float(r1), l_rows[r1], a1)
        a1_ref[h] = a1
        c_ref[h] = jnp.exp(s_scr[2 * h] - t_scr[2 * h, 0:1, :]) * (0.5 / z_scr[h:h + 1, :])
        b2_ref[h] = rank_scr[2 * h + 1].astype(BF16)
        e2_ref[h] = jnp.exp(s_scr[2 * h + 1] - t_scr[2 * h + 1, 0:1, :]).astype(BF16)


def _mixffn_kernel(x_ref, att_ref, rg_ref, mod_ref, wo_ref, nw_ref, wqt_ref, sk_ref,
                   x1_ref, h2t_ref, a1_ref, c_ref, b2_ref, e2_ref,
                   qt_scr, s_scr, w_scr, rank_scr, t_scr, cand_scr, sel_scr, z_scr):
    o = (jnp.dot(att_ref[...].astype(BF16), wo_ref[0:D_ATT, :], preferred_element_type=F32)
         + jnp.dot(rg_ref[...].astype(BF16), wo_ref[D_ATT:, :], preferred_element_type=F32))
    x1 = x_ref[...] + mod_ref[0, 2:3, :] * o
    x1_ref[...] = x1
    h2 = _rmsnorm(x1, nw_ref[...]) * (1.0 + mod_ref[0, 4:5, :]) + mod_ref[0, 3:4, :]
    h2t = h2.T.astype(BF16)
    h2t_ref[...] = h2t
    qt_scr[...] = jnp.dot(wqt_ref[...], h2t, preferred_element_type=F32).astype(BF16)
    _route_tile(sk_ref, qt_scr, s_scr, w_scr, rank_scr, t_scr, cand_scr, sel_scr, z_scr,
                a1_ref, c_ref, b2_ref, e2_ref)


def _mixffn_call(x2d, att, rg, mod3, mod_base, tiles_per_batch, w_out_bf, norm_w, w_qt_bf, sk_bf, tm):
    n = x2d.shape[0]
    tok = lambda t: (t, 0)
    route = pl.BlockSpec((P_HEADS, N_KEYS, tm), lambda t: (0, 0, t))
    route_shape = jax.ShapeDtypeStruct((P_HEADS, N_KEYS, n), F32)
    return pl.pallas_call(
        _mixffn_kernel,
        grid=(n // tm,),
        in_specs=[pl.BlockSpec((tm, D), tok),
                  pl.BlockSpec((tm, D_ATT), tok),
                  pl.BlockSpec((tm, D_RG), tok),
                  pl.BlockSpec((1, 6, D), lambda t: (mod_base + t // tiles_per_batch, 0, 0)),
                  pl.BlockSpec((D, D), lambda t: (0, 0)),
                  pl.BlockSpec((1, D), lambda t: (0, 0)),
                  pl.BlockSpec((2 * P_HEADS * N_KEYS, D), lambda t: (0, 0)),
                  pl.BlockSpec((2, N_KEYS, N_KEYS), lambda t: (0, 0, 0))],
        out_specs=[pl.BlockSpec((tm, D), tok),
                   pl.BlockSpec((D, tm), lambda t: (0, t)),
                   route, route, route, route],
        out_shape=[jax.ShapeDtypeStruct((n, D), F32), jax.ShapeDtypeStruct((D, n), BF16),
                   route_shape, route_shape,
                   jax.ShapeDtypeStruct((P_HEADS, N_KEYS, n), BF16), jax.ShapeDtypeStruct((P_HEADS, N_KEYS, n), BF16)],
        scratch_shapes=[pltpu.VMEM((2 * P_HEADS * N_KEYS, tm), BF16)]
                       + [pltpu.VMEM((N_HP, N_KEYS, tm), F32)] * 3
                       + [pltpu.VMEM((N_HP, TOPK, tm), F32)]
                       + [pltpu.VMEM((P_HEADS, 72, tm), F32)] * 2
                       + [pltpu.VMEM((P_HEADS, tm), F32)],
        compiler_params=_cparams("arbitrary"),
        name="mixffn",
    )(x2d, att, rg, mod3, w_out_bf, norm_w, w_qt_bf, sk_bf)


def _expert_prep_kernel(u_ref, v_ref, ub_ref, vt_ref):
    ub_ref[...] = u_ref[...].astype(BF16)
    vt_ref[...] = v_ref[...].T.astype(BF16)


def _expert_prep_call(u, v):
    te = 512
    return pl.pallas_call(
        _expert_prep_kernel,
        grid=(N_EXPERTS // te,),
        in_specs=[pl.BlockSpec((te, D), lambda j: (j, 0)), pl.BlockSpec((te, D), lambda j: (j, 0))],
        out_specs=[pl.BlockSpec((te, D), lambda j: (j, 0)), pl.BlockSpec((D, te), lambda j: (0, j))],
        out_shape=[jax.ShapeDtypeStruct((N_EXPERTS, D), BF16), jax.ShapeDtypeStruct((D, N_EXPERTS), BF16)],
        compiler_params=_cparams("arbitrary"),
        name="expert_prep",
    )(u, v)


def _peer_kernel(h2t_ref, u_ref, vt_ref, a1_ref, c_ref, b2_ref, e2_ref, x1_ref, mod_ref, nw_ref,
                 y_ref, acc, g_scr, *, i1_per_step):
    j = pl.program_id(1)
    tm = acc.shape[1]

    @pl.when(j == 0)
    def _():
        acc[...] = jnp.zeros_like(acc)

    grp = 4
    one = jnp.ones((), BF16)
    zero = jnp.zeros((), BF16)
    for i0 in range(0, i1_per_step, grp):
        for l in range(tm // 128):
            ls = slice(l * 128, (l + 1) * 128)
            g = [None] * grp
            for h in range(P_HEADS):
                b2 = b2_ref[h, :, ls]
                e2 = e2_ref[h, :, ls]
                for k in range(grp):
                    a1 = a1_ref[h, i0 + k:i0 + k + 1, ls].astype(BF16)
                    ch = c_ref[h, i0 + k:i0 + k + 1, ls].astype(BF16)
                    m = jnp.minimum(jnp.maximum(a1 - b2, zero), one)
                    w = (ch * e2) * m
                    g[k] = w if g[k] is None else g[k] + w
            for k in range(grp):
                g_scr[(i0 + k) * N_KEYS:(i0 + k + 1) * N_KEYS, ls] = g[k]
    st = jnp.dot(u_ref[...], h2t_ref[...], preferred_element_type=F32)
    act = st * (1.0 + lax.erf(st * 0.7071067811865476))
    at = act.astype(BF16) * g_scr[...]
    acc[...] += jnp.dot(vt_ref[...], at, preferred_element_type=F32)

    @pl.when(j == pl.num_programs(1) - 1)
    def _():
        x2 = x1_ref[...] + mod_ref[0, 5:6, :] * acc[...].T
        y_ref[...] = _rmsnorm(x2, nw_ref[...])


def _peer_call(h2t, u_bf, vt_bf, a1, c, b2, e2, x1, mod3, mod_base, tiles_per_batch, norm_f_w, tm, te):
    n = x1.shape[0]
    ips = te // N_KEYS
    route_lo = pl.BlockSpec((P_HEADS, ips, tm), lambda t, j: (0, j, t))
    route_full = pl.BlockSpec((P_HEADS, N_KEYS, tm), lambda t, j: (0, 0, t))
    return pl.pallas_call(
        functools.partial(_peer_kernel, i1_per_step=ips),
        grid=(n // tm, N_EXPERTS // te),
        in_specs=[pl.BlockSpec((D, tm), lambda t, j: (0, t)),
                  pl.BlockSpec((te, D), lambda t, j: (j, 0)),
                  pl.BlockSpec((D, te), lambda t, j: (0, j)),
                  route_lo, route_lo, route_full, route_full,
                  pl.BlockSpec((tm, D), lambda t, j: (t, 0)),
                  pl.BlockSpec((1, 6, D), lambda t, j: (mod_base + t // tiles_per_batch, 0, 0)),
                  pl.BlockSpec((1, D), lambda t, j: (0, 0))],
        out_specs=pl.BlockSpec((tm, D), lambda t, j: (t, 0)),
        out_shape=jax.ShapeDtypeStruct((n, D), F32),
        scratch_shapes=[pltpu.VMEM((D, tm), F32), pltpu.VMEM((te, tm), BF16)],
        compiler_params=_cparams("arbitrary", "arbitrary"),
        name="peer",
    )(h2t, u_bf, vt_bf, a1, c, b2, e2, x1, mod3, norm_f_w)


def _path(x, mod3, mod_base, per_batch_mod, weights, attn_fn, h0, emit_cache):
    (norm_mix_w, w_in_bf, rg_params, w_out_bf, norm_ffn_w, w_qt_bf, sk_bf, u_bf, vt_bf, norm_f_w) = weights
    bsz, seq, _ = x.shape
    n = bsz * seq
    x2d = x.reshape(n, D)
    tpb = lambda tm: (seq // tm) if per_batch_mod else n
    q, kb, vb, xr, yr, *kv_f32 = _inproj_call(x2d, mod3, mod_base, tpb(512), norm_mix_w, w_in_bf, emit_cache)
    att = attn_fn(q, kb, vb)
    rg_out, h_last = _rglru_call(xr, yr, h0, *rg_params, seq)
    tm_mix = 256
    x1, h2t, a1, c, b2, e2 = _mixffn_call(x2d, att, rg_out, mod3, mod_base, tpb(tm_mix), w_out_bf,
                                          norm_ffn_w, w_qt_bf, sk_bf, tm_mix)
    tm_peer = 1024
    y = _peer_call(h2t, u_bf, vt_bf, a1, c, b2, e2, x1, mod3, mod_base, tpb(tm_peer), norm_f_w,
                   tm_peer, 1024)
    return y.reshape(bsz, seq, D), kv_f32, h_last


def kernel(x_prompt, x_sample, c, cache_k, cache_v, state_rglru, c_ctx, w_mod, b_mod, norm_mix_w, w_in,
           rpb, conv_w, conv_b, rg_w_a, rg_b_a, rg_w_i, rg_b_i, rg_lambda, w_out, norm_ffn_w,
           peer_w_q, peer_sub_keys, peer_u, peer_v, norm_f_w):
    depth = w_mod.shape[0]
    assert depth == 1, "single-layer problem"
    l = 0
    bp, sp, _ = x_prompt.shape
    bs, ss, _ = x_sample.shape

    n_rows = 16
    cvec = jnp.concatenate([c_ctx[None, :], c, jnp.zeros((n_rows - 1 - bs, D), F32)], axis=0)
    mod3 = _mod_call(cvec, w_mod[l], b_mod[l][None, :]).reshape(n_rows, 6, D)

    wbd = jnp.stack([_block_diag_halves(rg_w_a[l]), _block_diag_halves(rg_w_i[l])], axis=1).astype(BF16)
    rg_params = (conv_w[l], conv_b[l][None, :], wbd, rg_b_a[l], rg_b_i[l], rg_lambda[l])
    u_bf, vt_bf = _expert_prep_call(peer_u[l], peer_v[l])
    weights = (norm_mix_w[l][None, :], w_in[l].astype(BF16), rg_params, w_out[l].astype(BF16),
               norm_ffn_w[l][None, :], peer_w_q[l].T.astype(BF16), peer_sub_keys[l].astype(BF16),
               u_bf, vt_bf, norm_f_w[None, :])

    ctx_attn = lambda q, kb, vb: _ctx_attn_call(q, kb, vb, sp)
    h0_p = jnp.zeros((bp, 2, D_RG), F32)
    y_prompt, (k_p, v_p), h_last = _path(x_prompt, mod3, 0, False, weights, ctx_attn, h0_p, True)

    tz = _bias_toeplitz(rpb[l])
    ck = cache_k[:, l].reshape(-1, D_ATT).astype(BF16)
    cv = cache_v[:, l].reshape(-1, D_ATT).astype(BF16)
    nbr_attn = lambda q, kb, vb: _nbr_attn_call(q, kb, vb, ck, cv, tz, bs, ss)
    y_sample, _, _ = _path(x_sample, mod3, 1, True, weights, nbr_attn, state_rglru[:, l], False)

    new_k = k_p.reshape(bp, 1, sp, N_HEADS, HEAD_DIM)
    new_v = v_p.reshape(bp, 1, sp, N_HEADS, HEAD_DIM)
    new_h = h_last.reshape(bp, 1, 2, D_RG)
    return (y_prompt, y_sample, new_k, new_v, new_h)
```

```python
import functools

import numpy as np
import jax
import jax.numpy as jnp
from jax import lax
from jax.experimental import pallas as pl
from jax.experimental.pallas import tpu as pltpu

F32 = jnp.float32
BF16 = jnp.bfloat16

D = 1024
D_ATT = 512
N_HEADS = 8
HEAD_DIM = 64
GRID_W = 64
WIN_H = 8
WIN_W = 16
D_RG = 512
D_IN = 3 * D_ATT + 2 * D_RG
N_KEYS = 128
P_HEADS = 8
TOPK = 16
N_EXPERTS = N_KEYS * N_KEYS
RMS_EPS = 1e-6
NEG_INF = -1e30
RG_C = 8.0
ATT_SCALE = HEAD_DIM ** -0.5

VMEM_LIMIT = 56 * 1024 * 1024

_NT = (((1,), (1,)), ((), ()))


def _cparams(*sem, flags=None):
    return pltpu.CompilerParams(dimension_semantics=sem, vmem_limit_bytes=VMEM_LIMIT, flags=flags)


def _rmsnorm(x, w):
    ms = jnp.mean(x * x, axis=-1, keepdims=True)
    return x * lax.rsqrt(ms + RMS_EPS) * w


def _mod_kernel(c_ref, w_ref, b_ref, o_ref):
    cv = c_ref[...]
    s = cv * jax.nn.sigmoid(cv)
    o_ref[...] = jnp.dot(s.astype(BF16), w_ref[...].astype(BF16),
                         preferred_element_type=F32) + b_ref[...]


def _mod_call(cvec, w_mod, b_mod):
    rows = cvec.shape[0]
    tn = 1536
    return pl.pallas_call(
        _mod_kernel,
        grid=(6 * D // tn,),
        in_specs=[pl.BlockSpec((rows, D), lambda j: (0, 0)),
                  pl.BlockSpec((D, tn), lambda j: (0, j)),
                  pl.BlockSpec((1, tn), lambda j: (0, j))],
        out_specs=pl.BlockSpec((rows, tn), lambda j: (0, j)),
        out_shape=jax.ShapeDtypeStruct((rows, 6 * D), F32),
        compiler_params=_cparams("arbitrary"),
        name="mod",
    )(cvec, w_mod, b_mod)


def _inproj_kernel(x_ref, mod_ref, nw_ref, w_ref, *out_refs, emit_f32_kv):
    if emit_f32_kv:
        q_ref, kb_ref, vb_ref, xr_ref, yr_ref, kf_ref, vf_ref = out_refs
    else:
        q_ref, kb_ref, vb_ref, xr_ref, yr_ref = out_refs
    x = x_ref[...]
    h = _rmsnorm(x, nw_ref[...]) * (1.0 + mod_ref[0, 1:2, :]) + mod_ref[0, 0:1, :]
    p = jnp.dot(h.astype(BF16), w_ref[...], preferred_element_type=F32)
    q_ref[...] = (p[:, 0:D_ATT] * ATT_SCALE).astype(BF16)
    k = p[:, D_ATT:2 * D_ATT]
    v = p[:, 2 * D_ATT:3 * D_ATT]
    kb_ref[...] = k.astype(BF16)
    vb_ref[...] = v.astype(BF16)
    xr_ref[...] = p[:, 3 * D_ATT:3 * D_ATT + D_RG]
    yr_ref[...] = p[:, 3 * D_ATT + D_RG:]
    if emit_f32_kv:
        kf_ref[...] = k
        vf_ref[...] = v


def _inproj_call(x2d, mod3, mod_base, tiles_per_batch, norm_w, w_in_bf, emit_f32_kv):
    n = x2d.shape[0]
    tm = 512
    row = lambda t: (mod_base + t // tiles_per_batch, 0, 0)
    tok = lambda t: (t, 0)
    shp = lambda w, dt: jax.ShapeDtypeStruct((n, w), dt)
    out_specs = [pl.BlockSpec((tm, D_ATT), tok)] * 3 + [pl.BlockSpec((tm, D_RG), tok)] * 2
    out_shape = [shp(D_ATT, BF16)] * 3 + [shp(D_RG, F32)] * 2
    if emit_f32_kv:
        out_specs += [pl.BlockSpec((tm, D_ATT), tok)] * 2
        out_shape += [shp(D_ATT, F32)] * 2
    return pl.pallas_call(
        functools.partial(_inproj_kernel, emit_f32_kv=emit_f32_kv),
        grid=(n // tm,),
        in_specs=[pl.BlockSpec((tm, D), tok),
                  pl.BlockSpec((1, 6, D), row),
                  pl.BlockSpec((1, D), lambda t: (0, 0)),
                  pl.BlockSpec((D, D_IN), lambda t: (0, 0))],
        out_specs=out_specs,
        out_shape=out_shape,
        compiler_params=_cparams("arbitrary"),
        name="inproj",
    )(x2d, mod3, norm_w, w_in_bf)


def _ctx_attn_kernel(q_ref, k_ref, v_ref, o_ref):
    for h in range(N_HEADS):
        sl = slice(h * HEAD_DIM, (h + 1) * HEAD_DIM)
        s = lax.dot_general(q_ref[:, sl], k_ref[:, sl], _NT, preferred_element_type=F32)
        m = jnp.max(s, axis=-1, keepdims=True)
        p = jnp.exp(s - m)
        l = jnp.sum(p, axis=-1, keepdims=True)
        o = jnp.dot(p.astype(BF16), v_ref[:, sl], preferred_element_type=F32)
        o_ref[:, sl] = o / l


def _ctx_attn_call(q, kb, vb, seq):
    n = q.shape[0]
    blk = pl.BlockSpec((seq, D_ATT), lambda b: (b, 0))
    return pl.pallas_call(
        _ctx_attn_kernel,
        grid=(n // seq,),
        in_specs=[blk, blk, blk],
        out_specs=blk,
        out_shape=jax.ShapeDtypeStruct((n, D_ATT), F32),
        compiler_params=_cparams("arbitrary"),
        name="ctx_attn",
    )(q, kb, vb)


ROWS_PER_STEP = 4
WIN_UNION = 12


def _nbr_attn_kernel(q_ref, k_ref, v_ref, ck_ref, cv_ref, tz_ref, o_ref, *, rows):
    g = pl.program_id(0)
    n_loc = WIN_UNION * GRID_W
    first = jnp.clip(g * ROWS_PER_STEP - WIN_H // 2, 0, rows - WIN_H)
    kb = jnp.minimum((first >> 1) << 1, rows - WIN_UNION)
    row0 = pl.multiple_of(kb * GRID_W, 2 * GRID_W)
    lane = lax.broadcasted_iota(jnp.int32, (1, 2 * GRID_W), 1)
    d_idx, pen = [], []
    for i in range(ROWS_PER_STEP):
        r = g * ROWS_PER_STEP + i
        start = jnp.clip(r - WIN_H // 2, 0, rows - WIN_H)
        d_i, pen_i = [], []
        for p in range(WIN_UNION // 2):
            kr = kb + 2 * p
            d_i.append(jnp.clip(kr - r + (WIN_H - 1), 0, 2 * WIN_H - 3))
            in0 = (kr >= start) & (kr < start + WIN_H)
            in1 = (kr + 1 >= start) & (kr + 1 < start + WIN_H)
            pen_i.append(jnp.where(lane < GRID_W, jnp.where(in0, 0.0, NEG_INF), jnp.where(in1, 0.0, NEG_INF)))
        d_idx.append(d_i)
        pen.append(pen_i)
    outs = []
    for h in range(N_HEADS):
        sl = slice(h * HEAD_DIM, (h + 1) * HEAD_DIM)
        qg = q_ref[:, sl]
        bias = jnp.concatenate(
            [jnp.concatenate([tz_ref[h, d_idx[i][p]] + pen[i][p] for p in range(WIN_UNION // 2)], axis=-1)
             for i in range(ROWS_PER_STEP)], axis=0)
        s_loc = lax.dot_general(qg, k_ref[pl.ds(row0, n_loc), sl], _NT, preferred_element_type=F32) + bias
        s_ctx = lax.dot_general(qg, ck_ref[:, sl], _NT, preferred_element_type=F32)
        m = jnp.maximum(jnp.max(s_loc, axis=-1, keepdims=True), jnp.max(s_ctx, axis=-1, keepdims=True))
        p_loc = jnp.exp(s_loc - m)
        p_ctx = jnp.exp(s_ctx - m)
        l = jnp.sum(p_loc, axis=-1, keepdims=True) + jnp.sum(p_ctx, axis=-1, keepdims=True)
        o = (jnp.dot(p_loc.astype(BF16), v_ref[pl.ds(row0, n_loc), sl], preferred_element_type=F32)
             + jnp.dot(p_ctx.astype(BF16), cv_ref[:, sl], preferred_element_type=F32))
        outs.append(o / l)
    o_ref[...] = jnp.concatenate(outs, axis=-1)


def _nbr_attn_call(q, kb, vb, ck, cv, tz, batch, seq):
    rows = seq // GRID_W
    assert rows % 2 == 0 and rows >= WIN_UNION and rows % ROWS_PER_STEP == 0
    groups = rows // ROWS_PER_STEP
    past = ck.shape[0] // batch
    n = q.shape[0]
    qblk = pl.BlockSpec((GRID_W * ROWS_PER_STEP, D_ATT), lambda g, b: (b * groups + g, 0))
    kvblk = pl.BlockSpec((seq, D_ATT), lambda g, b: (b, 0))
    cblk = pl.BlockSpec((past, D_ATT), lambda g, b: (b, 0))
    return pl.pallas_call(
        functools.partial(_nbr_attn_kernel, rows=rows),
        grid=(groups, batch),
        in_specs=[qblk, kvblk, kvblk, cblk, cblk,
                  pl.BlockSpec(tz.shape, lambda g, b: (0, 0, 0, 0))],
        out_specs=qblk,
        out_shape=jax.ShapeDtypeStruct((n, D_ATT), F32),
        compiler_params=_cparams("arbitrary", "arbitrary"),
        name="nbr_attn",
    )(q, kb, vb, ck, cv, tz)


def _bias_toeplitz(rpb_l):
    qc = np.arange(GRID_W)
    off = qc[None, :] - qc[:, None] + (WIN_W - 1)
    place = (off[None, :, :] == np.arange(2 * WIN_W - 1)[:, None, None]).astype(np.float32)
    tz = jnp.einsum('hdx,xqk->hdqk', rpb_l, jnp.asarray(place), precision=lax.Precision.HIGHEST)
    cs = np.clip(qc - WIN_W // 2, 0, GRID_W - WIN_W)
    valid = (qc[None, :] >= cs[:, None]) & (qc[None, :] < cs[:, None] + WIN_W)
    tz = jnp.where(jnp.asarray(valid)[None, None], tz, NEG_INF)
    return jnp.concatenate([tz[:, :-1], tz[:, 1:]], axis=-1)


def _log_sigmoid(x):
    return jnp.minimum(x, 0.0) - jnp.log(1.0 + jnp.exp(-jnp.abs(x)))


def _gelu_tanh(x):
    return 0.5 * x * (1.0 + jnp.tanh(0.7978845608028654 * (x + 0.044715 * x * x * x)))


def _rglru_kernel(xr_ref, yr_ref, h0_ref, cw_ref, cb_ref, wbd_ref, ba_ref, bi_ref, lam_ref,
                  y_ref, hl_ref, xpad, a_f, b_f, a_b, b_b):
    T = xr_ref.shape[0]
    CH = 256
    xpad[0:8, :] = jnp.zeros((8, D_RG), F32)
    xpad[T + 8:T + 16, :] = jnp.zeros((8, D_RG), F32)
    xpad[8:T + 8, :] = xr_ref[...]
    a_refs = (a_f, a_b)
    b_refs = (b_f, b_b)
    for c0 in range(0, T, CH):
        xc = cb_ref[...] + cw_ref[0:1, :] * xpad[c0 + 6:c0 + 6 + CH, :]
        for i in range(1, 4):
            xc = xc + cw_ref[i:i + 1, :] * xpad[c0 + 6 + i:c0 + 6 + i + CH, :]
        xcb = xc.astype(BF16)
        for d in range(2):
            gates = []
            for g in range(2):
                halves = [jnp.dot(xcb[:, hf * 256:(hf + 1) * 256], wbd_ref[d, g, hf],
                                  preferred_element_type=F32) for hf in range(2)]
                gates.append(jnp.concatenate(halves, axis=-1))
            rg = jax.nn.sigmoid(gates[0] + ba_ref[d:d + 1, :])
            ig = jax.nn.sigmoid(gates[1] + bi_ref[d:d + 1, :])
            log_a = RG_C * rg * _log_sigmoid(lam_ref[d:d + 1, :])
            a = jnp.exp(log_a)
            b = jnp.sqrt(1.0 - a * a) * (ig * xc)
            a_refs[d][c0:c0 + CH, :] = a
            b_refs[d][c0:c0 + CH, :] = b

    sub = lax.broadcasted_iota(jnp.int32, (8, D_RG), 0)
    n_tiles = T // 8

    def scan_tile(a_ref, b_ref, t8, h, order):
        base = pl.multiple_of(t8 * 8, 8)
        a_blk = a_ref[pl.ds(base, 8), :]
        b_blk = b_ref[pl.ds(base, 8), :]
        out = b_blk
        for j in order:
            cand = a_blk * h + b_blk
            h = jnp.broadcast_to(cand[j:j + 1, :], (8, D_RG))
            out = jnp.where(sub == j, cand, out)
        b_ref[pl.ds(base, 8), :] = out
        return h

    def body(i, carry):
        hf, hb = carry
        hf = scan_tile(a_f, b_f, i, hf, range(8))
        hb = scan_tile(a_b, b_b, n_tiles - 1 - i, hb, range(7, -1, -1))
        return hf, hb

    hf0 = jnp.broadcast_to(h0_ref[0, 0:1, :], (8, D_RG))
    hb0 = jnp.broadcast_to(h0_ref[0, 1:2, :], (8, D_RG))
    hf, hb = lax.fori_loop(0, n_tiles, body, (hf0, hb0))
    hl_ref[0, 0:1, :] = hf[0:1, :]
    hl_ref[0, 1:2, :] = hb[0:1, :]
    for c0 in range(0, T, CH):
        sl = slice(c0, c0 + CH)
        y_ref[sl, :] = (b_f[sl, :] + b_b[sl, :]) * _gelu_tanh(yr_ref[sl, :])


def _rglru_call(xr, yr, h0, conv_w, conv_b, wbd, b_a, b_i, lam, seq):
    n = xr.shape[0]
    batch = n // seq
    tok = pl.BlockSpec((seq, D_RG), lambda b: (b, 0))
    full = lambda shape: pl.BlockSpec(shape, lambda b: (0,) * len(shape))
    return pl.pallas_call(
        _rglru_kernel,
        grid=(batch,),
        in_specs=[tok, tok, pl.BlockSpec((1, 2, D_RG), lambda b: (b, 0, 0)),
                  full((4, D_RG)), full((1, D_RG)), full((2, 2, 2, 256, 256)),
                  full((2, D_RG)), full((2, D_RG)), full((2, D_RG))],
        out_specs=[tok, pl.BlockSpec((1, 2, D_RG), lambda b: (b, 0, 0))],
        out_shape=[jax.ShapeDtypeStruct((n, D_RG), F32), jax.ShapeDtypeStruct((batch, 2, D_RG), F32)],
        scratch_shapes=[pltpu.VMEM((seq + 16, D_RG), F32)] + [pltpu.VMEM((seq, D_RG), F32)] * 4,
        compiler_params=_cparams("arbitrary"),
        name="rglru",
    )(xr, yr, h0, conv_w, conv_b, wbd, b_a, b_i, lam)


def _block_diag_halves(w):
    d = w.shape[0]
    w4 = w.reshape(d, 2, 4, 64, 64)
    eye = jnp.eye(4, dtype=w.dtype)
    out = jnp.einsum('dhncf,nm->dhncmf', w4, eye)
    return out.reshape(d, 2, 256, 256)


N_HP = 2 * P_HEADS


def _candidate_flat(tm):
    subf = lax.broadcasted_iota(jnp.int32, (8, tm), 0).astype(F32)
    flat = [subf + float(r1 * TOPK) for r1 in range(4)]
    flat.append(subf + 8.0)
    flat += [subf * float(TOPK) + float(r2) for r2 in range(3)]
    flat.append((subf + 8.0) * float(TOPK))
    return jnp.concatenate(flat, axis=0)


def _candidate_sums(t1, t2):
    tm = t1.shape[1]
    sub = lax.broadcasted_iota(jnp.int32, (8, tm), 0)
    ninf = -jnp.inf
    lim_a = (8, 8, 5, 4)
    vals = [jnp.where(sub < lim_a[r1], t1[r1:r1 + 1, :] + t2[0:8, :], ninf) for r1 in range(4)]
    vals.append(t1[0:1, :] + t2[8:16, :])
    for r2 in range(3):
        ok = (sub >= 4) if r2 < 2 else (sub == 4)
        vals.append(jnp.where(ok, t1[0:8, :] + t2[r2:r2 + 1, :], ninf))
    vals.append(t1[8:16, :] + t2[0:1, :])
    return jnp.concatenate(vals, axis=0)


def _staircase_rows(sel):
    rows = []
    for r1 in range(4):
        l = jnp.sum(sel[8 * r1:8 * r1 + 8, :], axis=0, keepdims=True)
        if r1 == 0:
            l = l + jnp.sum(sel[32:40, :], axis=0, keepdims=True)
        rows.append(l)
    l_mid = sel[40:48, :] + sel[48:56, :] + sel[56:64, :]
    l_hi = sel[64:72, :]
    rows += [l_mid[r1:r1 + 1, :] for r1 in range(4, 8)]
    rows += [l_hi[r1 - 8:r1 - 7, :] for r1 in range(8, 16)]
    return rows


def _batcher_pairs(n):
    pairs = []

    def merge(lo, cnt, r):
        step = r * 2
        if step < cnt:
            merge(lo, cnt, step)
            merge(lo + r, cnt, step)
            for i in range(lo + r, lo + cnt - r, step):
                pairs.append((i, i + r))
        else:
            pairs.append((lo, lo + r))

    def sort(lo, cnt):
        if cnt > 1:
            m = cnt // 2
            sort(lo, m)
            sort(lo + m, m)
            merge(lo, cnt, 1)

    sort(0, n)
    return pairs


_SORT16 = _batcher_pairs(TOPK)


def _top16_network(s):
    v = list(s)
    for i, j in _SORT16:
        v[i], v[j] = jnp.maximum(v[i], v[j]), jnp.minimum(v[i], v[j])
    for shift in (4, 2, 1):
        rolled = [pltpu.roll(x, shift, 0) for x in v]
        c = [jnp.maximum(v[k], rolled[TOPK - 1 - k]) for k in range(TOPK)]
        for d in (8, 4, 2, 1):
            for k in range(TOPK):
                if k & d == 0:
                    c[k], c[k + d] = jnp.maximum(c[k], c[k + d]), jnp.minimum(c[k], c[k + d])
        v = c
    return v


def _route_tile(sk_ref, qt_scr, s_scr, w_scr, rank_scr, t_scr, cand_scr, sel_scr, z_scr,
                a1_ref, c_ref, b2_ref, e2_ref):
    tm = qt_scr.shape[1]
    iota_f = lax.broadcasted_iota(jnp.int32, (N_KEYS, tm), 0).astype(F32)
    flat = _candidate_flat(tm)
    ninf = -jnp.inf
    for hp in range(N_HP):
        s_scr[hp] = jnp.dot(sk_ref[hp % 2], qt_scr[hp * N_KEYS:(hp + 1) * N_KEYS, :],
                            preferred_element_type=F32)

    def stage1_extract():
        for hp in range(N_HP):
            w_scr[hp] = s_scr[hp]
            rank_scr[hp] = jnp.full((N_KEYS, tm), float(TOPK), F32)

        def round1(r, carry):
            rf = jnp.asarray(r, jnp.int32).astype(F32)
            for hp in range(N_HP):
                w = w_scr[hp]
                m = jnp.max(w, axis=0, keepdims=True)
                idx = jnp.min(jnp.where(w == m, iota_f, float(N_KEYS)), axis=0, keepdims=True)
                hit = iota_f == idx
                rank_scr[hp] = jnp.where(hit, rf, rank_scr[hp])
                w_scr[hp] = jnp.where(hit, ninf, w)
                t_scr[hp, pl.ds(r, 1), :] = m
            return carry

        lax.fori_loop(0, TOPK, round1, 0)

    def stage1_network():
        lane_blocks = tm // 128
        sub = lax.broadcasted_iota(jnp.int32, (8, 128), 0)

        def body(h, flags):
            out = list(flags)
            for p in range(2):
                hp = 2 * h + p
                for lb in range(lane_blocks):
                    ls = slice(lb * 128, (lb + 1) * 128)
                    s = [s_scr[hp, 8 * v:8 * v + 8, ls] for v in range(N_KEYS // 8)]
                    t = _top16_network(s)
                    lo, hi = t[0], t[8]
                    for r in range(1, 8):
                        lo = jnp.where(sub == r, t[r], lo)
                        hi = jnp.where(sub == r, t[8 + r], hi)
                    t_scr[hp, 0:8, ls] = lo
                    t_scr[hp, 8:16, ls] = hi
                    if p == 1:
                        for v in range(N_KEYS // 8):
                            rank = jnp.zeros((8, 128), F32)
                            for r in range(TOPK):
                                rank = jnp.where(t[r] > s[v], float(r + 1), rank)
                            rank_scr[hp, 8 * v:8 * v + 8, ls] = rank
                    tie = out[lb]
                    for r in range(TOPK - 1):
                        tie = jnp.where(t[r] == t[r + 1], 1.0, tie)
                    out[lb] = tie
            return tuple(out)

        flags = lax.fori_loop(0, P_HEADS, body, (jnp.zeros((8, 128), F32),) * lane_blocks)
        return jnp.concatenate([f[0:1, :] for f in flags], axis=1)

    def stage2(exact):
        for h in range(P_HEADS):
            cand_scr[h] = _candidate_sums(t_scr[2 * h], t_scr[2 * h + 1])
            sel_scr[h] = jnp.zeros((72, tm), F32)

        def round2(r, carry):
            zs, b0s = carry
            zs_new, b0s_new = [], []
            for h in range(P_HEADS):
                w = cand_scr[h]
                m = jnp.max(w, axis=0, keepdims=True)
                if exact:
                    f = jnp.min(jnp.where(w == m, flat, 1e9), axis=0, keepdims=True)
                    hit = flat == f
                else:
                    hit = w == m
                sel_scr[h] = jnp.where(hit, 1.0, sel_scr[h])
                cand_scr[h] = jnp.where(hit, ninf, w)
                b0 = jnp.where(r == 0, m, b0s[h])
                zs_new.append(zs[h] + jnp.exp(m - b0))
                b0s_new.append(b0)
            return tuple(zs_new), tuple(b0s_new)

        zero_row = jnp.zeros((1, tm), F32)
        zs, _ = lax.fori_loop(0, TOPK, round2, ((zero_row,) * P_HEADS, (zero_row,) * P_HEADS))
        for h in range(P_HEADS):
            z_scr[h:h + 1, :] = zs[h]

    excess = stage1_network()
    stage2(False)
    for hp in range(N_HP):
        if hp % 2 == 0:
            marks = jnp.where(s_scr[hp] >= t_scr[hp, TOPK - 1:TOPK, :], 1.0, 0.0)
        else:
            marks = jnp.where(rank_scr[hp] < float(TOPK), 1.0, 0.0)
        cnt = jnp.sum(marks, axis=0, keepdims=True)
        excess = jnp.maximum(excess, jnp.abs(cnt - float(TOPK)))
    for h in range(P_HEADS):
        cnt = jnp.sum(sel_scr[h], axis=0, keepdims=True)
        excess = jnp.maximum(excess, jnp.abs(cnt - float(TOPK)))
    tied = jnp.max(excess) > 0.0

    @pl.when(tied)
    def _():
        stage1_extract()
        stage2(True)
        for h in range(P_HEADS):
            l_rows = _staircase_rows(sel_scr[h])
            rank1 = rank_scr[2 * h]
            a1 = jnp.zeros((N_KEYS, tm), F32)
            for r1 in range(TOPK):
                a1 = jnp.where(rank1 == float(r1), l_rows[r1], a1)
            a1_ref[h] = a1

    @pl.when(jnp.logical_not(tied))
    def finish_from_values():
        for h in range(P_HEADS):
            l_rows = _staircase_rows(sel_scr[h])
            s1 = s_scr[2 * h]
            a1 = jnp.zeros((N_KEYS, tm), F32)
            for r1 in range(TOPK - 1, -1, -1):
                a1 = jnp.where(s1 >= t_scr[2 * h, r1:r1 + 1, :], l_rows[r1], a1)
            a1_ref[h] = a1

    for h in range(P_HEADS):
        c_ref[h] = jnp.exp(s_scr[2 * h] - t_scr[2 * h, 0:1, :]) * (0.5 / z_scr[h:h + 1, :])
        b2_ref[h] = rank_scr[2 * h + 1].astype(BF16)
        e2_ref[h] = jnp.exp(s_scr[2 * h + 1] - t_scr[2 * h + 1, 0:1, :]).astype(BF16)


def _mixffn_kernel(x_ref, att_ref, rg_ref, mod_ref, wo_ref, nw_ref, wqt_ref, sk_ref,
                   x1_ref, h2t_ref, a1_ref, c_ref, b2_ref, e2_ref,
                   qt_scr, s_scr, w_scr, rank_scr, t_scr, cand_scr, sel_scr, z_scr):
    o = (jnp.dot(att_ref[...].astype(BF16), wo_ref[0:D_ATT, :], preferred_element_type=F32)
         + jnp.dot(rg_ref[...].astype(BF16), wo_ref[D_ATT:, :], preferred_element_type=F32))
    x1 = x_ref[...] + mod_ref[0, 2:3, :] * o
    x1_ref[...] = x1
    h2 = _rmsnorm(x1, nw_ref[...]) * (1.0 + mod_ref[0, 4:5, :]) + mod_ref[0, 3:4, :]
    h2t = h2.T.astype(BF16)
    h2t_ref[...] = h2t
    qt_scr[...] = jnp.dot(wqt_ref[...], h2t, preferred_element_type=F32).astype(BF16)
    _route_tile(sk_ref, qt_scr, s_scr, w_scr, rank_scr, t_scr, cand_scr, sel_scr, z_scr,
                a1_ref, c_ref, b2_ref, e2_ref)


def _mixffn_call(x2d, att, rg, mod3, mod_base, tiles_per_batch, w_out_bf, norm_w, w_qt_bf, sk_bf, tm):
    n = x2d.shape[0]
    tok = lambda t: (t, 0)
    route = pl.BlockSpec((P_HEADS, N_KEYS, tm), lambda t: (0, 0, t))
    route_shape = jax.ShapeDtypeStruct((P_HEADS, N_KEYS, n), F32)
    return pl.pallas_call(
        _mixffn_kernel,
        grid=(n // tm,),
        in_specs=[pl.BlockSpec((tm, D), tok),
                  pl.BlockSpec((tm, D_ATT), tok),
                  pl.BlockSpec((tm, D_RG), tok),
                  pl.BlockSpec((1, 6, D), lambda t: (mod_base + t // tiles_per_batch, 0, 0)),
                  pl.BlockSpec((D, D), lambda t: (0, 0)),
                  pl.BlockSpec((1, D), lambda t: (0, 0)),
                  pl.BlockSpec((2 * P_HEADS * N_KEYS, D), lambda t: (0, 0)),
                  pl.BlockSpec((2, N_KEYS, N_KEYS), lambda t: (0, 0, 0))],
        out_specs=[pl.BlockSpec((tm, D), tok),
                   pl.BlockSpec((D, tm), lambda t: (0, t)),
                   route, route, route, route],
        out_shape=[jax.ShapeDtypeStruct((n, D), F32), jax.ShapeDtypeStruct((D, n), BF16),
                   route_shape, route_shape,
                   jax.ShapeDtypeStruct((P_HEADS, N_KEYS, n), BF16), jax.ShapeDtypeStruct((P_HEADS, N_KEYS, n), BF16)],
        scratch_shapes=[pltpu.VMEM((2 * P_HEADS * N_KEYS, tm), BF16)]
                       + [pltpu.VMEM((N_HP, N_KEYS, tm), F32)] * 3
                       + [pltpu.VMEM((N_HP, TOPK, tm), F32)]
                       + [pltpu.VMEM((P_HEADS, 72, tm), F32)] * 2
                       + [pltpu.VMEM((P_HEADS, tm), F32)],
        compiler_params=_cparams("arbitrary"),
        name="mixffn",
    )(x2d, att, rg, mod3, w_out_bf, norm_w, w_qt_bf, sk_bf)


def _expert_prep_kernel(u_ref, v_ref, ub_ref, vt_ref):
    ub_ref[...] = u_ref[...].astype(BF16)
    vt_ref[...] = v_ref[...].T.astype(BF16)


def _expert_prep_call(u, v):
    te = 512
    return pl.pallas_call(
        _expert_prep_kernel,
        grid=(N_EXPERTS // te,),
        in_specs=[pl.BlockSpec((te, D), lambda j: (j, 0)), pl.BlockSpec((te, D), lambda j: (j, 0))],
        out_specs=[pl.BlockSpec((te, D), lambda j: (j, 0)), pl.BlockSpec((D, te), lambda j: (0, j))],
        out_shape=[jax.ShapeDtypeStruct((N_EXPERTS, D), BF16), jax.ShapeDtypeStruct((D, N_EXPERTS), BF16)],
        compiler_params=_cparams("arbitrary"),
        name="expert_prep",
    )(u, v)


def _peer_kernel(h2t_ref, u_ref, vt_ref, a1_ref, c_ref, b2_ref, e2_ref, x1_ref, mod_ref, nw_ref,
                 y_ref, acc, g_scr, *, i1_per_step):
    j = pl.program_id(1)
    tm = acc.shape[1]

    @pl.when(j == 0)
    def _():
        acc[...] = jnp.zeros_like(acc)

    grp = 4
    one = jnp.ones((), BF16)
    zero = jnp.zeros((), BF16)
    for i0 in range(0, i1_per_step, grp):
        for l in range(tm // 128):
            ls = slice(l * 128, (l + 1) * 128)
            g = [None] * grp
            for h in range(P_HEADS):
                b2 = b2_ref[h, :, ls]
                e2 = e2_ref[h, :, ls]
                for k in range(grp):
                    a1 = a1_ref[h, i0 + k:i0 + k + 1, ls].astype(BF16)
                    ch = c_ref[h, i0 + k:i0 + k + 1, ls].astype(BF16)
                    m = jnp.minimum(jnp.maximum(a1 - b2, zero), one)
                    w = (ch * e2) * m
                    g[k] = w if g[k] is None else g[k] + w
            for k in range(grp):
                g_scr[(i0 + k) * N_KEYS:(i0 + k + 1) * N_KEYS, ls] = g[k]
    st = jnp.dot(u_ref[...], h2t_ref[...], preferred_element_type=F32)
    act = st * (1.0 + lax.erf(st * 0.7071067811865476))
    at = act.astype(BF16) * g_scr[...]
    acc[...] += jnp.dot(vt_ref[...], at, preferred_element_type=F32)

    @pl.when(j == pl.num_programs(1) - 1)
    def _():
        x2 = x1_ref[...] + mod_ref[0, 5:6, :] * acc[...].T
        y_ref[...] = _rmsnorm(x2, nw_ref[...])


def _peer_call(h2t, u_bf, vt_bf, a1, c, b2, e2, x1, mod3, mod_base, tiles_per_batch, norm_f_w, tm, te):
    n = x1.shape[0]
    ips = te // N_KEYS
    route_lo = pl.BlockSpec((P_HEADS, ips, tm), lambda t, j: (0, j, t))
    route_full = pl.BlockSpec((P_HEADS, N_KEYS, tm), lambda t, j: (0, 0, t))
    return pl.pallas_call(
        functools.partial(_peer_kernel, i1_per_step=ips),
        grid=(n // tm, N_EXPERTS // te),
        in_specs=[pl.BlockSpec((D, tm), lambda t, j: (0, t)),
                  pl.BlockSpec((te, D), lambda t, j: (j, 0)),
                  pl.BlockSpec((D, te), lambda t, j: (0, j)),
                  route_lo, route_lo, route_full, route_full,
                  pl.BlockSpec((tm, D), lambda t, j: (t, 0)),
                  pl.BlockSpec((1, 6, D), lambda t, j: (mod_base + t // tiles_per_batch, 0, 0)),
                  pl.BlockSpec((1, D), lambda t, j: (0, 0))],
        out_specs=pl.BlockSpec((tm, D), lambda t, j: (t, 0)),
        out_shape=jax.ShapeDtypeStruct((n, D), F32),
        scratch_shapes=[pltpu.VMEM((D, tm), F32), pltpu.VMEM((te, tm), BF16)],
        compiler_params=_cparams("arbitrary", "arbitrary"),
        name="peer",
    )(h2t, u_bf, vt_bf, a1, c, b2, e2, x1, mod3, norm_f_w)


def _path(x, mod3, mod_base, per_batch_mod, weights, attn_fn, h0, emit_cache):
    (norm_mix_w, w_in_bf, rg_params, w_out_bf, norm_ffn_w, w_qt_bf, sk_bf, u_bf, vt_bf, norm_f_w) = weights
    bsz, seq, _ = x.shape
    n = bsz * seq
    x2d = x.reshape(n, D)
    tpb = lambda tm: (seq // tm) if per_batch_mod else n
    q, kb, vb, xr, yr, *kv_f32 = _inproj_call(x2d, mod3, mod_base, tpb(512), norm_mix_w, w_in_bf, emit_cache)
    att = attn_fn(q, kb, vb)
    rg_out, h_last = _rglru_call(xr, yr, h0, *rg_params, seq)
    tm_mix = 256
    x1, h2t, a1, c, b2, e2 = _mixffn_call(x2d, att, rg_out, mod3, mod_base, tpb(tm_mix), w_out_bf,
                                          norm_ffn_w, w_qt_bf, sk_bf, tm_mix)
    tm_peer = 1024
    y = _peer_call(h2t, u_bf, vt_bf, a1, c, b2, e2, x1, mod3, mod_base, tpb(tm_peer), norm_f_w,
                   tm_peer, 1024)
    return y.reshape(bsz, seq, D), kv_f32, h_last


def kernel(x_prompt, x_sample, c, cache_k, cache_v, state_rglru, c_ctx, w_mod, b_mod, norm_mix_w, w_in,
           rpb, conv_w, conv_b, rg_w_a, rg_b_a, rg_w_i, rg_b_i, rg_lambda, w_out, norm_ffn_w,
           peer_w_q, peer_sub_keys, peer_u, peer_v, norm_f_w):
    depth = w_mod.shape[0]
    assert depth == 1, "single-layer problem"
    l = 0
    bp, sp, _ = x_prompt.shape
    bs, ss, _ = x_sample.shape

    n_rows = 16
    cvec = jnp.concatenate([c_ctx[None, :], c, jnp.zeros((n_rows - 1 - bs, D), F32)], axis=0)
    mod3 = _mod_call(cvec, w_mod[l], b_mod[l][None, :]).reshape(n_rows, 6, D)

    wbd = jnp.stack([_block_diag_halves(rg_w_a[l]), _block_diag_halves(rg_w_i[l])], axis=1).astype(BF16)
    rg_params = (conv_w[l], conv_b[l][None, :], wbd, rg_b_a[l], rg_b_i[l], rg_lambda[l])
    u_bf, vt_bf = _expert_prep_call(peer_u[l], peer_v[l])
    weights = (norm_mix_w[l][None, :], w_in[l].astype(BF16), rg_params, w_out[l].astype(BF16),
               norm_ffn_w[l][None, :], peer_w_q[l].T.astype(BF16), peer_sub_keys[l].astype(BF16),
               u_bf, vt_bf, norm_f_w[None, :])

    ctx_attn = lambda q, kb, vb: _ctx_attn_call(q, kb, vb, sp)
    h0_p = jnp.zeros((bp, 2, D_RG), F32)
    y_prompt, (k_p, v_p), h_last = _path(x_prompt, mod3, 0, False, weights, ctx_attn, h0_p, True)

    tz = _bias_toeplitz(rpb[l])
    ck = cache_k[:, l].reshape(-1, D_ATT).astype(BF16)
    cv = cache_v[:, l].reshape(-1, D_ATT).astype(BF16)
    nbr_attn = lambda q, kb, vb: _nbr_attn_call(q, kb, vb, ck, cv, tz, bs, ss)
    y_sample, _, _ = _path(x_sample, mod3, 1, True, weights, nbr_attn, state_rglru[:, l], False)

    new_k = k_p.reshape(bp, 1, sp, N_HEADS, HEAD_DIM)
    new_v = v_p.reshape(bp, 1, sp, N_HEADS, HEAD_DIM)
    new_h = h_last.reshape(bp, 1, 2, D_RG)
    return (y_prompt, y_sample, new_k, new_v, new_h)
```

```python
import functools

import numpy as np
import jax
import jax.numpy as jnp
from jax import lax
from jax.experimental import pallas as pl
from jax.experimental.pallas import tpu as pltpu

F32 = jnp.float32
BF16 = jnp.bfloat16

D = 1024
D_ATT = 512
N_HEADS = 8
HEAD_DIM = 64
GRID_W = 64
WIN_H = 8
WIN_W = 16
D_RG = 512
D_IN = 3 * D_ATT + 2 * D_RG
N_KEYS = 128
P_HEADS = 8
TOPK = 16
N_EXPERTS = N_KEYS * N_KEYS
RMS_EPS = 1e-6
NEG_INF = -1e30
RG_C = 8.0
ATT_SCALE = HEAD_DIM ** -0.5

VMEM_LIMIT = 56 * 1024 * 1024

_NT = (((1,), (1,)), ((), ()))


def _cparams(*sem, flags=None):
    return pltpu.CompilerParams(dimension_semantics=sem, vmem_limit_bytes=VMEM_LIMIT, flags=flags)


def _rmsnorm(x, w):
    ms = jnp.mean(x * x, axis=-1, keepdims=True)
    return x * lax.rsqrt(ms + RMS_EPS) * w


def _mod_kernel(c_ref, w_ref, b_ref, o_ref):
    cv = c_ref[...]
    s = cv * jax.nn.sigmoid(cv)
    o_ref[...] = jnp.dot(s.astype(BF16), w_ref[...].astype(BF16),
                         preferred_element_type=F32) + b_ref[...]


def _mod_call(cvec, w_mod, b_mod):
    rows = cvec.shape[0]
    tn = 1536
    return pl.pallas_call(
        _mod_kernel,
        grid=(6 * D // tn,),
        in_specs=[pl.BlockSpec((rows, D), lambda j: (0, 0)),
                  pl.BlockSpec((D, tn), lambda j: (0, j)),
                  pl.BlockSpec((1, tn), lambda j: (0, j))],
        out_specs=pl.BlockSpec((rows, tn), lambda j: (0, j)),
        out_shape=jax.ShapeDtypeStruct((rows, 6 * D), F32),
        compiler_params=_cparams("arbitrary"),
        name="mod",
    )(cvec, w_mod, b_mod)


def _inproj_kernel(x_ref, mod_ref, nw_ref, w_ref, *out_refs, emit_f32_kv):
    if emit_f32_kv:
        q_ref, kb_ref, vb_ref, xr_ref, yr_ref, kf_ref, vf_ref = out_refs
    else:
        q_ref, kb_ref, vb_ref, xr_ref, yr_ref = out_refs
    x = x_ref[...]
    h = _rmsnorm(x, nw_ref[...]) * (1.0 + mod_ref[0, 1:2, :]) + mod_ref[0, 0:1, :]
    p = jnp.dot(h.astype(BF16), w_ref[...], preferred_element_type=F32)
    q_ref[...] = (p[:, 0:D_ATT] * ATT_SCALE).astype(BF16)
    k = p[:, D_ATT:2 * D_ATT]
    v = p[:, 2 * D_ATT:3 * D_ATT]
    kb_ref[...] = k.astype(BF16)
    vb_ref[...] = v.astype(BF16)
    xr_ref[...] = p[:, 3 * D_ATT:3 * D_ATT + D_RG]
    yr_ref[...] = p[:, 3 * D_ATT + D_RG:]
    if emit_f32_kv:
        kf_ref[...] = k
        vf_ref[...] = v


def _inproj_call(x2d, mod3, mod_base, tiles_per_batch, norm_w, w_in_bf, emit_f32_kv):
    n = x2d.shape[0]
    tm = 512
    row = lambda t: (mod_base + t // tiles_per_batch, 0, 0)
    tok = lambda t: (t, 0)
    shp = lambda w, dt: jax.ShapeDtypeStruct((n, w), dt)
    out_specs = [pl.BlockSpec((tm, D_ATT), tok)] * 3 + [pl.BlockSpec((tm, D_RG), tok)] * 2
    out_shape = [shp(D_ATT, BF16)] * 3 + [shp(D_RG, F32)] * 2
    if emit_f32_kv:
        out_specs += [pl.BlockSpec((tm, D_ATT), tok)] * 2
        out_shape += [shp(D_ATT, F32)] * 2
    return pl.pallas_call(
        functools.partial(_inproj_kernel, emit_f32_kv=emit_f32_kv),
        grid=(n // tm,),
        in_specs=[pl.BlockSpec((tm, D), tok),
                  pl.BlockSpec((1, 6, D), row),
                  pl.BlockSpec((1, D), lambda t: (0, 0)),
                  pl.BlockSpec((D, D_IN), lambda t: (0, 0))],
        out_specs=out_specs,
        out_shape=out_shape,
        compiler_params=_cparams("arbitrary"),
        name="inproj",
    )(x2d, mod3, norm_w, w_in_bf)


def _ctx_attn_kernel(q_ref, k_ref, v_ref, o_ref):
    for h in range(N_HEADS):
        sl = slice(h * HEAD_DIM, (h + 1) * HEAD_DIM)
        s = lax.dot_general(q_ref[:, sl], k_ref[:, sl], _NT, preferred_element_type=F32)
        m = jnp.max(s, axis=-1, keepdims=True)
        p = jnp.exp(s - m)
        l = jnp.sum(p, axis=-1, keepdims=True)
        o = jnp.dot(p.astype(BF16), v_ref[:, sl], preferred_element_type=F32)
        o_ref[:, sl] = o / l


def _ctx_attn_call(q, kb, vb, seq):
    n = q.shape[0]
    blk = pl.BlockSpec((seq, D_ATT), lambda b: (b, 0))
    return pl.pallas_call(
        _ctx_attn_kernel,
        grid=(n // seq,),
        in_specs=[blk, blk, blk],
        out_specs=blk,
        out_shape=jax.ShapeDtypeStruct((n, D_ATT), F32),
        compiler_params=_cparams("arbitrary"),
        name="ctx_attn",
    )(q, kb, vb)


ROWS_PER_STEP = 4
WIN_UNION = 12


def _nbr_attn_kernel(q_ref, k_ref, v_ref, ck_ref, cv_ref, tz_ref, o_ref, *, rows):
    g = pl.program_id(0)
    n_loc = WIN_UNION * GRID_W
    first = jnp.clip(g * ROWS_PER_STEP - WIN_H // 2, 0, rows - WIN_H)
    kb = jnp.minimum((first >> 1) << 1, rows - WIN_UNION)
    row0 = pl.multiple_of(kb * GRID_W, 2 * GRID_W)
    lane = lax.broadcasted_iota(jnp.int32, (1, 2 * GRID_W), 1)
    d_idx, pen = [], []
    for i in range(ROWS_PER_STEP):
        r = g * ROWS_PER_STEP + i
        start = jnp.clip(r - WIN_H // 2, 0, rows - WIN_H)
        d_i, pen_i = [], []
        for p in range(WIN_UNION // 2):
            kr = kb + 2 * p
            d_i.append(jnp.clip(kr - r + (WIN_H - 1), 0, 2 * WIN_H - 3))
            in0 = (kr >= start) & (kr < start + WIN_H)
            in1 = (kr + 1 >= start) & (kr + 1 < start + WIN_H)
            pen_i.append(jnp.where(lane < GRID_W, jnp.where(in0, 0.0, NEG_INF), jnp.where(in1, 0.0, NEG_INF)))
        d_idx.append(d_i)
        pen.append(pen_i)
    outs = []
    for h in range(N_HEADS):
        sl = slice(h * HEAD_DIM, (h + 1) * HEAD_DIM)
        qg = q_ref[:, sl]
        bias = jnp.concatenate(
            [jnp.concatenate([tz_ref[h, d_idx[i][p]] + pen[i][p] for p in range(WIN_UNION // 2)], axis=-1)
             for i in range(ROWS_PER_STEP)], axis=0)
        s_loc = lax.dot_general(qg, k_ref[pl.ds(row0, n_loc), sl], _NT, preferred_element_type=F32) + bias
        s_ctx = lax.dot_general(qg, ck_ref[:, sl], _NT, preferred_element_type=F32)
        m = jnp.maximum(jnp.max(s_loc, axis=-1, keepdims=True), jnp.max(s_ctx, axis=-1, keepdims=True))
        p_loc = jnp.exp(s_loc - m)
        p_ctx = jnp.exp(s_ctx - m)
        l = jnp.sum(p_loc, axis=-1, keepdims=True) + jnp.sum(p_ctx, axis=-1, keepdims=True)
        o = (jnp.dot(p_loc.astype(BF16), v_ref[pl.ds(row0, n_loc), sl], preferred_element_type=F32)
             + jnp.dot(p_ctx.astype(BF16), cv_ref[:, sl], preferred_element_type=F32))
        outs.append(o / l)
    o_ref[...] = jnp.concatenate(outs, axis=-1)


def _nbr_attn_call(q, kb, vb, ck, cv, tz, batch, seq):
    rows = seq // GRID_W
    assert rows % 2 == 0 and rows >= WIN_UNION and rows % ROWS_PER_STEP == 0
    groups = rows // ROWS_PER_STEP
    past = ck.shape[0] // batch
    n = q.shape[0]
    qblk = pl.BlockSpec((GRID_W * ROWS_PER_STEP, D_ATT), lambda g, b: (b * groups + g, 0))
    kvblk = pl.BlockSpec((seq, D_ATT), lambda g, b: (b, 0))
    cblk = pl.BlockSpec((past, D_ATT), lambda g, b: (b, 0))
    return pl.pallas_call(
        functools.partial(_nbr_attn_kernel, rows=rows),
        grid=(groups, batch),
        in_specs=[qblk, kvblk, kvblk, cblk, cblk,
                  pl.BlockSpec(tz.shape, lambda g, b: (0, 0, 0, 0))],
        out_specs=qblk,
        out_shape=jax.ShapeDtypeStruct((n, D_ATT), F32),
        compiler_params=_cparams("arbitrary", "arbitrary"),
        name="nbr_attn",
    )(q, kb, vb, ck, cv, tz)


def _bias_toeplitz(rpb_l):
    qc = np.arange(GRID_W)
    off = qc[None, :] - qc[:, None] + (WIN_W - 1)
    place = (off[None, :, :] == np.arange(2 * WIN_W - 1)[:, None, None]).astype(np.float32)
    tz = jnp.einsum('hdx,xqk->hdqk', rpb_l, jnp.asarray(place), precision=lax.Precision.HIGHEST)
    cs = np.clip(qc - WIN_W // 2, 0, GRID_W - WIN_W)
    valid = (qc[None, :] >= cs[:, None]) & (qc[None, :] < cs[:, None] + WIN_W)
    tz = jnp.where(jnp.asarray(valid)[None, None], tz, NEG_INF)
    return jnp.concatenate([tz[:, :-1], tz[:, 1:]], axis=-1)


def _log_sigmoid(x):
    return jnp.minimum(x, 0.0) - jnp.log(1.0 + jnp.exp(-jnp.abs(x)))


def _gelu_tanh(x):
    return 0.5 * x * (1.0 + jnp.tanh(0.7978845608028654 * (x + 0.044715 * x * x * x)))


def _rglru_kernel(xr_ref, yr_ref, h0_ref, cw_ref, cb_ref, wbd_ref, ba_ref, bi_ref, lam_ref,
                  y_ref, hl_ref, xpad, a_f, b_f, a_b, b_b):
    T = xr_ref.shape[0]
    CH = 256
    xpad[0:8, :] = jnp.zeros((8, D_RG), F32)
    xpad[T + 8:T + 16, :] = jnp.zeros((8, D_RG), F32)
    xpad[8:T + 8, :] = xr_ref[...]
    a_refs = (a_f, a_b)
    b_refs = (b_f, b_b)
    for c0 in range(0, T, CH):
        xc = cb_ref[...] + cw_ref[0:1, :] * xpad[c0 + 6:c0 + 6 + CH, :]
        for i in range(1, 4):
            xc = xc + cw_ref[i:i + 1, :] * xpad[c0 + 6 + i:c0 + 6 + i + CH, :]
        xcb = xc.astype(BF16)
        for d in range(2):
            gates = []
            for g in range(2):
                halves = [jnp.dot(xcb[:, hf * 256:(hf + 1) * 256], wbd_ref[d, g, hf],
                                  preferred_element_type=F32) for hf in range(2)]
                gates.append(jnp.concatenate(halves, axis=-1))
            rg = jax.nn.sigmoid(gates[0] + ba_ref[d:d + 1, :])
            ig = jax.nn.sigmoid(gates[1] + bi_ref[d:d + 1, :])
            log_a = RG_C * rg * _log_sigmoid(lam_ref[d:d + 1, :])
            a = jnp.exp(log_a)
            b = jnp.sqrt(1.0 - a * a) * (ig * xc)
            a_refs[d][c0:c0 + CH, :] = a
            b_refs[d][c0:c0 + CH, :] = b

    sub = lax.broadcasted_iota(jnp.int32, (8, D_RG), 0)
    n_tiles = T // 8

    def scan_tile(a_ref, b_ref, t8, h, order):
        base = pl.multiple_of(t8 * 8, 8)
        a_blk = a_ref[pl.ds(base, 8), :]
        b_blk = b_ref[pl.ds(base, 8), :]
        out = b_blk
        for j in order:
            cand = a_blk * h + b_blk
            h = jnp.broadcast_to(cand[j:j + 1, :], (8, D_RG))
            out = jnp.where(sub == j, cand, out)
        b_ref[pl.ds(base, 8), :] = out
        return h

    def body(i, carry):
        hf, hb = carry
        hf = scan_tile(a_f, b_f, i, hf, range(8))
        hb = scan_tile(a_b, b_b, n_tiles - 1 - i, hb, range(7, -1, -1))
        return hf, hb

    hf0 = jnp.broadcast_to(h0_ref[0, 0:1, :], (8, D_RG))
    hb0 = jnp.broadcast_to(h0_ref[0, 1:2, :], (8, D_RG))
    hf, hb = lax.fori_loop(0, n_tiles, body, (hf0, hb0))
    hl_ref[0, 0:1, :] = hf[0:1, :]
    hl_ref[0, 1:2, :] = hb[0:1, :]
    for c0 in range(0, T, CH):
        sl = slice(c0, c0 + CH)
        y_ref[sl, :] = (b_f[sl, :] + b_b[sl, :]) * _gelu_tanh(yr_ref[sl, :])


def _rglru_call(xr, yr, h0, conv_w, conv_b, wbd, b_a, b_i, lam, seq):
    n = xr.shape[0]
    batch = n // seq
    tok = pl.BlockSpec((seq, D_RG), lambda b: (b, 0))
    full = lambda shape: pl.BlockSpec(shape, lambda b: (0,) * len(shape))
    return pl.pallas_call(
        _rglru_kernel,
        grid=(batch,),
        in_specs=[tok, tok, pl.BlockSpec((1, 2, D_RG), lambda b: (b, 0, 0)),
                  full((4, D_RG)), full((1, D_RG)), full((2, 2, 2, 256, 256)),
                  full((2, D_RG)), full((2, D_RG)), full((2, D_RG))],
        out_specs=[tok, pl.BlockSpec((1, 2, D_RG), lambda b: (b, 0, 0))],
        out_shape=[jax.ShapeDtypeStruct((n, D_RG), F32), jax.ShapeDtypeStruct((batch, 2, D_RG), F32)],
        scratch_shapes=[pltpu.VMEM((seq + 16, D_RG), F32)] + [pltpu.VMEM((seq, D_RG), F32)] * 4,
        compiler_params=_cparams("arbitrary"),
        name="rglru",
    )(xr, yr, h0, conv_w, conv_b, wbd, b_a, b_i, lam)


def _block_diag_halves(w):
    d = w.shape[0]
    w4 = w.reshape(d, 2, 4, 64, 64)
    eye = jnp.eye(4, dtype=w.dtype)
    out = jnp.einsum('dhncf,nm->dhncmf', w4, eye)
    return out.reshape(d, 2, 256, 256)


N_HP = 2 * P_HEADS


def _candidate_flat(tm):
    subf = lax.broadcasted_iota(jnp.int32, (8, tm), 0).astype(F32)
    flat = [subf + float(r1 * TOPK) for r1 in range(4)]
    flat.append(subf + 8.0)
    flat += [subf * float(TOPK) + float(r2) for r2 in range(3)]
    flat.append((subf + 8.0) * float(TOPK))
    return jnp.concatenate(flat, axis=0)


def _candidate_sums(t1, t2):
    tm = t1.shape[1]
    sub = lax.broadcasted_iota(jnp.int32, (8, tm), 0)
    ninf = -jnp.inf
    lim_a = (8, 8, 5, 4)
    vals = [jnp.where(sub < lim_a[r1], t1[r1:r1 + 1, :] + t2[0:8, :], ninf) for r1 in range(4)]
    vals.append(t1[0:1, :] + t2[8:16, :])
    for r2 in range(3):
        ok = (sub >= 4) if r2 < 2 else (sub == 4)
        vals.append(jnp.where(ok, t1[0:8, :] + t2[r2:r2 + 1, :], ninf))
    vals.append(t1[8:16, :] + t2[0:1, :])
    return jnp.concatenate(vals, axis=0)


def _staircase_rows(sel):
    rows = []
    for r1 in range(4):
        l = jnp.sum(sel[8 * r1:8 * r1 + 8, :], axis=0, keepdims=True)
        if r1 == 0:
            l = l + jnp.sum(sel[32:40, :], axis=0, keepdims=True)
        rows.append(l)
    l_mid = sel[40:48, :] + sel[48:56, :] + sel[56:64, :]
    l_hi = sel[64:72, :]
    rows += [l_mid[r1:r1 + 1, :] for r1 in range(4, 8)]
    rows += [l_hi[r1 - 8:r1 - 7, :] for r1 in range(8, 16)]
    return rows


def _batcher_pairs(n):
    pairs = []

    def merge(lo, cnt, r):
        step = r * 2
        if step < cnt:
            merge(lo, cnt, step)
            merge(lo + r, cnt, step)
            for i in range(lo + r, lo + cnt - r, step):
                pairs.append((i, i + r))
        else:
            pairs.append((lo, lo + r))

    def sort(lo, cnt):
        if cnt > 1:
            m = cnt // 2
            sort(lo, m)
            sort(lo + m, m)
            merge(lo, cnt, 1)

    sort(0, n)
    return pairs


_SORT16 = _batcher_pairs(TOPK)


def _top16_network(s):
    v = list(s)
    for i, j in _SORT16:
        v[i], v[j] = jnp.maximum(v[i], v[j]), jnp.minimum(v[i], v[j])
    for shift in (4, 2, 1):
        rolled = [pltpu.roll(x, shift, 0) for x in v]
        c = [jnp.maximum(v[k], rolled[TOPK - 1 - k]) for k in range(TOPK)]
        for d in (8, 4, 2, 1):
            for k in range(TOPK):
                if k & d == 0:
                    c[k], c[k + d] = jnp.maximum(c[k], c[k + d]), jnp.minimum(c[k], c[k + d])
        v = c
    return v


def _route_tile(sk_ref, qt_scr, s_scr, w_scr, rank_scr, t_scr, cand_scr, sel_scr, z_scr,
                a1_ref, c_ref, b2_ref, e2_ref):
    tm = qt_scr.shape[1]
    iota_f = lax.broadcasted_iota(jnp.int32, (N_KEYS, tm), 0).astype(F32)
    flat = _candidate_flat(tm)
    ninf = -jnp.inf
    for hp in range(N_HP):
        s_scr[hp] = jnp.dot(sk_ref[hp % 2], qt_scr[hp * N_KEYS:(hp + 1) * N_KEYS, :],
                            preferred_element_type=F32)

    def stage1_extract():
        for hp in range(N_HP):
            w_scr[hp] = s_scr[hp]
            rank_scr[hp] = jnp.full((N_KEYS, tm), float(TOPK), F32)

        def round1(r, carry):
            rf = jnp.asarray(r, jnp.int32).astype(F32)
            for hp in range(N_HP):
                w = w_scr[hp]
                m = jnp.max(w, axis=0, keepdims=True)
                idx = jnp.min(jnp.where(w == m, iota_f, float(N_KEYS)), axis=0, keepdims=True)
                hit = iota_f == idx
                rank_scr[hp] = jnp.where(hit, rf, rank_scr[hp])
                w_scr[hp] = jnp.where(hit, ninf, w)
                t_scr[hp, pl.ds(r, 1), :] = m
            return carry

        lax.fori_loop(0, TOPK, round1, 0)

    def stage1_network():
        lane_blocks = tm // 128
        sub = lax.broadcasted_iota(jnp.int32, (8, 128), 0)

        def body(h, flags):
            out = list(flags)
            for p in range(2):
                hp = 2 * h + p
                for lb in range(lane_blocks):
                    ls = slice(lb * 128, (lb + 1) * 128)
                    s = [s_scr[hp, 8 * v:8 * v + 8, ls] for v in range(N_KEYS // 8)]
                    t = _top16_network(s)
                    lo, hi = t[0], t[8]
                    for r in range(1, 8):
                        lo = jnp.where(sub == r, t[r], lo)
                        hi = jnp.where(sub == r, t[8 + r], hi)
                    t_scr[hp, 0:8, ls] = lo
                    t_scr[hp, 8:16, ls] = hi
                    if p == 1:
                        for v in range(N_KEYS // 8):
                            rank = jnp.zeros((8, 128), F32)
                            for r in range(TOPK):
                                rank = jnp.where(t[r] > s[v], float(r + 1), rank)
                            rank_scr[hp, 8 * v:8 * v + 8, ls] = rank
                    tie = out[lb]
                    for r in range(TOPK - 1):
                        tie = jnp.where(t[r] == t[r + 1], 1.0, tie)
                    out[lb] = tie
            return tuple(out)

        flags = lax.fori_loop(0, P_HEADS, body, (jnp.zeros((8, 128), F32),) * lane_blocks)
        return jnp.concatenate([f[0:1, :] for f in flags], axis=1)

    def stage2_extract():
        for h in range(P_HEADS):
            cand_scr[h] = _candidate_sums(t_scr[2 * h], t_scr[2 * h + 1])
            sel_scr[h] = jnp.zeros((72, tm), F32)

        def round2(r, carry):
            zs, b0s = carry
            zs_new, b0s_new = [], []
            for h in range(P_HEADS):
                w = cand_scr[h]
                m = jnp.max(w, axis=0, keepdims=True)
                f = jnp.min(jnp.where(w == m, flat, 1e9), axis=0, keepdims=True)
                hit = flat == f
                sel_scr[h] = jnp.where(hit, 1.0, sel_scr[h])
                cand_scr[h] = jnp.where(hit, ninf, w)
                b0 = jnp.where(r == 0, m, b0s[h])
                zs_new.append(zs[h] + jnp.exp(m - b0))
                b0s_new.append(b0)
            return tuple(zs_new), tuple(b0s_new)

        zero_row = jnp.zeros((1, tm), F32)
        zs, _ = lax.fori_loop(0, TOPK, round2, ((zero_row,) * P_HEADS, (zero_row,) * P_HEADS))
        for h in range(P_HEADS):
            z_scr[h] = jnp.broadcast_to(zs[h], (8, tm))

    def stage2_network():
        pad = jnp.full((8, 128), ninf, F32)

        def body(h, carry):
            for lb in range(tm // 128):
                ls = slice(lb * 128, (lb + 1) * 128)
                cand = _candidate_sums(t_scr[2 * h, :, ls], t_scr[2 * h + 1, :, ls])
                c = [cand[8 * v:8 * v + 8, :] for v in range(9)]
                top = _top16_network(c + [pad] * (TOPK - 9))
                for v in range(9):
                    sel_scr[h, 8 * v:8 * v + 8, ls] = jnp.where(c[v] >= top[TOPK - 1], 1.0, 0.0)
                z = jnp.ones((8, 128), F32)
                for r in range(1, TOPK):
                    z = z + jnp.exp(top[r] - top[0])
                z_scr[h, :, ls] = z
            return carry

        lax.fori_loop(0, P_HEADS, body, 0)

    excess = stage1_network()
    stage2_network()
    for hp in range(N_HP):
        if hp % 2 == 0:
            marks = jnp.where(s_scr[hp] >= t_scr[hp, TOPK - 1:TOPK, :], 1.0, 0.0)
        else:
            marks = jnp.where(rank_scr[hp] < float(TOPK), 1.0, 0.0)
        cnt = jnp.sum(marks, axis=0, keepdims=True)
        excess = jnp.maximum(excess, jnp.abs(cnt - float(TOPK)))
    for h in range(P_HEADS):
        cnt = jnp.sum(sel_scr[h], axis=0, keepdims=True)
        excess = jnp.maximum(excess, jnp.abs(cnt - float(TOPK)))
    tied = jnp.max(excess) > 0.0

    @pl.when(tied)
    def _():
        stage1_extract()
        stage2_extract()
        for h in range(P_HEADS):
            l_rows = _staircase_rows(sel_scr[h])
            rank1 = rank_scr[2 * h]
            a1 = jnp.zeros((N_KEYS, tm), F32)
            for r1 in range(TOPK):
                a1 = jnp.where(rank1 == float(r1), l_rows[r1], a1)
            a1_ref[h] = a1

    @pl.when(jnp.logical_not(tied))
    def finish_from_values():
        for h in range(P_HEADS):
            l_rows = _staircase_rows(sel_scr[h])
            s1 = s_scr[2 * h]
            a1 = jnp.zeros((N_KEYS, tm), F32)
            for r1 in range(TOPK - 1, -1, -1):
                a1 = jnp.where(s1 >= t_scr[2 * h, r1:r1 + 1, :], l_rows[r1], a1)
            a1_ref[h] = a1

    for h in range(P_HEADS):
        c_ref[h] = jnp.exp(s_scr[2 * h] - t_scr[2 * h, 0:1, :]) * (0.5 / z_scr[h, 0:1, :])
        b2_ref[h] = rank_scr[2 * h + 1].astype(BF16)
        e2_ref[h] = jnp.exp(s_scr[2 * h + 1] - t_scr[2 * h + 1, 0:1, :]).astype(BF16)


def _mixffn_kernel(x_ref, att_ref, rg_ref, mod_ref, wo_ref, nw_ref, wqt_ref, sk_ref,
                   x1_ref, h2t_ref, a1_ref, c_ref, b2_ref, e2_ref,
                   qt_scr, s_scr, w_scr, rank_scr, t_scr, cand_scr, sel_scr, z_scr):
    o = (jnp.dot(att_ref[...].astype(BF16), wo_ref[0:D_ATT, :], preferred_element_type=F32)
         + jnp.dot(rg_ref[...].astype(BF16), wo_ref[D_ATT:, :], preferred_element_type=F32))
    x1 = x_ref[...] + mod_ref[0, 2:3, :] * o
    x1_ref[...] = x1
    h2 = _rmsnorm(x1, nw_ref[...]) * (1.0 + mod_ref[0, 4:5, :]) + mod_ref[0, 3:4, :]
    h2t = h2.T.astype(BF16)
    h2t_ref[...] = h2t
    qt_scr[...] = jnp.dot(wqt_ref[...], h2t, preferred_element_type=F32).astype(BF16)
    _route_tile(sk_ref, qt_scr, s_scr, w_scr, rank_scr, t_scr, cand_scr, sel_scr, z_scr,
                a1_ref, c_ref, b2_ref, e2_ref)


def _mixffn_call(x2d, att, rg, mod3, mod_base, tiles_per_batch, w_out_bf, norm_w, w_qt_bf, sk_bf, tm):
    n = x2d.shape[0]
    tok = lambda t: (t, 0)
    route = pl.BlockSpec((P_HEADS, N_KEYS, tm), lambda t: (0, 0, t))
    route_shape = jax.ShapeDtypeStruct((P_HEADS, N_KEYS, n), F32)
    return pl.pallas_call(
        _mixffn_kernel,
        grid=(n // tm,),
        in_specs=[pl.BlockSpec((tm, D), tok),
                  pl.BlockSpec((tm, D_ATT), tok),
                  pl.BlockSpec((tm, D_RG), tok),
                  pl.BlockSpec((1, 6, D), lambda t: (mod_base + t // tiles_per_batch, 0, 0)),
                  pl.BlockSpec((D, D), lambda t: (0, 0)),
                  pl.BlockSpec((1, D), lambda t: (0, 0)),
                  pl.BlockSpec((2 * P_HEADS * N_KEYS, D), lambda t: (0, 0)),
                  pl.BlockSpec((2, N_KEYS, N_KEYS), lambda t: (0, 0, 0))],
        out_specs=[pl.BlockSpec((tm, D), tok),
                   pl.BlockSpec((D, tm), lambda t: (0, t)),
                   route, route, route, route],
        out_shape=[jax.ShapeDtypeStruct((n, D), F32), jax.ShapeDtypeStruct((D, n), BF16),
                   route_shape, route_shape,
                   jax.ShapeDtypeStruct((P_HEADS, N_KEYS, n), BF16), jax.ShapeDtypeStruct((P_HEADS, N_KEYS, n), BF16)],
        scratch_shapes=[pltpu.VMEM((2 * P_HEADS * N_KEYS, tm), BF16)]
                       + [pltpu.VMEM((N_HP, N_KEYS, tm), F32)] * 3
                       + [pltpu.VMEM((N_HP, TOPK, tm), F32)]
                       + [pltpu.VMEM((P_HEADS, 72, tm), F32)] * 2
                       + [pltpu.VMEM((P_HEADS, 8, tm), F32)],
        compiler_params=_cparams("arbitrary"),
        name="mixffn",
    )(x2d, att, rg, mod3, w_out_bf, norm_w, w_qt_bf, sk_bf)


def _expert_prep_kernel(u_ref, v_ref, ub_ref, vt_ref):
    ub_ref[...] = u_ref[...].astype(BF16)
    vt_ref[...] = v_ref[...].T.astype(BF16)


def _expert_prep_call(u, v):
    te = 512
    return pl.pallas_call(
        _expert_prep_kernel,
        grid=(N_EXPERTS // te,),
        in_specs=[pl.BlockSpec((te, D), lambda j: (j, 0)), pl.BlockSpec((te, D), lambda j: (j, 0))],
        out_specs=[pl.BlockSpec((te, D), lambda j: (j, 0)), pl.BlockSpec((D, te), lambda j: (0, j))],
        out_shape=[jax.ShapeDtypeStruct((N_EXPERTS, D), BF16), jax.ShapeDtypeStruct((D, N_EXPERTS), BF16)],
        compiler_params=_cparams("arbitrary"),
        name="expert_prep",
    )(u, v)


def _peer_kernel(h2t_ref, u_ref, vt_ref, a1_ref, c_ref, b2_ref, e2_ref, x1_ref, mod_ref, nw_ref,
                 y_ref, acc, g_scr, *, i1_per_step):
    j = pl.program_id(1)
    tm = acc.shape[1]

    @pl.when(j == 0)
    def _():
        acc[...] = jnp.zeros_like(acc)

    grp = 4
    one = jnp.ones((), BF16)
    zero = jnp.zeros((), BF16)
    for i0 in range(0, i1_per_step, grp):
        for l in range(tm // 128):
            ls = slice(l * 128, (l + 1) * 128)
            g = [None] * grp
            for h in range(P_HEADS):
                b2 = b2_ref[h, :, ls]
                e2 = e2_ref[h, :, ls]
                for k in range(grp):
                    a1 = a1_ref[h, i0 + k:i0 + k + 1, ls].astype(BF16)
                    ch = c_ref[h, i0 + k:i0 + k + 1, ls].astype(BF16)
                    m = jnp.minimum(jnp.maximum(a1 - b2, zero), one)
                    w = (ch * e2) * m
                    g[k] = w if g[k] is None else g[k] + w
            for k in range(grp):
                g_scr[(i0 + k) * N_KEYS:(i0 + k + 1) * N_KEYS, ls] = g[k]
    st = jnp.dot(u_ref[...], h2t_ref[...], preferred_element_type=F32)
    act = st * (1.0 + lax.erf(st * 0.7071067811865476))
    at = act.astype(BF16) * g_scr[...]
    acc[...] += jnp.dot(vt_ref[...], at, preferred_element_type=F32)

    @pl.when(j == pl.num_programs(1) - 1)
    def _():
        x2 = x1_ref[...] + mod_ref[0, 5:6, :] * acc[...].T
        y_ref[...] = _rmsnorm(x2, nw_ref[...])


def _peer_call(h2t, u_bf, vt_bf, a1, c, b2, e2, x1, mod3, mod_base, tiles_per_batch, norm_f_w, tm, te):
    n = x1.shape[0]
    ips = te // N_KEYS
    route_lo = pl.BlockSpec((P_HEADS, ips, tm), lambda t, j: (0, j, t))
    route_full = pl.BlockSpec((P_HEADS, N_KEYS, tm), lambda t, j: (0, 0, t))
    return pl.pallas_call(
        functools.partial(_peer_kernel, i1_per_step=ips),
        grid=(n // tm, N_EXPERTS // te),
        in_specs=[pl.BlockSpec((D, tm), lambda t, j: (0, t)),
                  pl.BlockSpec((te, D), lambda t, j: (j, 0)),
                  pl.BlockSpec((D, te), lambda t, j: (0, j)),
                  route_lo, route_lo, route_full, route_full,
                  pl.BlockSpec((tm, D), lambda t, j: (t, 0)),
                  pl.BlockSpec((1, 6, D), lambda t, j: (mod_base + t // tiles_per_batch, 0, 0)),
                  pl.BlockSpec((1, D), lambda t, j: (0, 0))],
        out_specs=pl.BlockSpec((tm, D), lambda t, j: (t, 0)),
        out_shape=jax.ShapeDtypeStruct((n, D), F32),
        scratch_shapes=[pltpu.VMEM((D, tm), F32), pltpu.VMEM((te, tm), BF16)],
        compiler_params=_cparams("arbitrary", "arbitrary"),
        name="peer",
    )(h2t, u_bf, vt_bf, a1, c, b2, e2, x1, mod3, norm_f_w)


def _path(x, mod3, mod_base, per_batch_mod, weights, attn_fn, h0, emit_cache):
    (norm_mix_w, w_in_bf, rg_params, w_out_bf, norm_ffn_w, w_qt_bf, sk_bf, u_bf, vt_bf, norm_f_w) = weights
    bsz, seq, _ = x.shape
    n = bsz * seq
    x2d = x.reshape(n, D)
    tpb = lambda tm: (seq // tm) if per_batch_mod else n
    q, kb, vb, xr, yr, *kv_f32 = _inproj_call(x2d, mod3, mod_base, tpb(512), norm_mix_w, w_in_bf, emit_cache)
    att = attn_fn(q, kb, vb)
    rg_out, h_last = _rglru_call(xr, yr, h0, *rg_params, seq)
    tm_mix = 256
    x1, h2t, a1, c, b2, e2 = _mixffn_call(x2d, att, rg_out, mod3, mod_base, tpb(tm_mix), w_out_bf,
                                          norm_ffn_w, w_qt_bf, sk_bf, tm_mix)
    tm_peer = 1024
    y = _peer_call(h2t, u_bf, vt_bf, a1, c, b2, e2, x1, mod3, mod_base, tpb(tm_peer), norm_f_w,
                   tm_peer, 1024)
    return y.reshape(bsz, seq, D), kv_f32, h_last


def kernel(x_prompt, x_sample, c, cache_k, cache_v, state_rglru, c_ctx, w_mod, b_mod, norm_mix_w, w_in,
           rpb, conv_w, conv_b, rg_w_a, rg_b_a, rg_w_i, rg_b_i, rg_lambda, w_out, norm_ffn_w,
           peer_w_q, peer_sub_keys, peer_u, peer_v, norm_f_w):
    depth = w_mod.shape[0]
    assert depth == 1, "single-layer problem"
    l = 0
    bp, sp, _ = x_prompt.shape
    bs, ss, _ = x_sample.shape

    n_rows = 16
    cvec = jnp.concatenate([c_ctx[None, :], c, jnp.zeros((n_rows - 1 - bs, D), F32)], axis=0)
    mod3 = _mod_call(cvec, w_mod[l], b_mod[l][None, :]).reshape(n_rows, 6, D)

    wbd = jnp.stack([_block_diag_halves(rg_w_a[l]), _block_diag_halves(rg_w_i[l])], axis=1).astype(BF16)
    rg_params = (conv_w[l], conv_b[l][None, :], wbd, rg_b_a[l], rg_b_i[l], rg_lambda[l])
    u_bf, vt_bf = _expert_prep_call(peer_u[l], peer_v[l])
    weights = (norm_mix_w[l][None, :], w_in[l].astype(BF16), rg_params, w_out[l].astype(BF16),
               norm_ffn_w[l][None, :], peer_w_q[l].T.astype(BF16), peer_sub_keys[l].astype(BF16),
               u_bf, vt_bf, norm_f_w[None, :])

    ctx_attn = lambda q, kb, vb: _ctx_attn_call(q, kb, vb, sp)
    h0_p = jnp.zeros((bp, 2, D_RG), F32)
    y_prompt, (k_p, v_p), h_last = _path(x_prompt, mod3, 0, False, weights, ctx_attn, h0_p, True)

    tz = _bias_toeplitz(rpb[l])
    ck = cache_k[:, l].reshape(-1, D_ATT).astype(BF16)
    cv = cache_v[:, l].reshape(-1, D_ATT).astype(BF16)
    nbr_attn = lambda q, kb, vb: _nbr_attn_call(q, kb, vb, ck, cv, tz, bs, ss)
    y_sample, _, _ = _path(x_sample, mod3, 1, True, weights, nbr_attn, state_rglru[:, l], False)

    new_k = k_p.reshape(bp, 1, sp, N_HEADS, HEAD_DIM)
    new_v = v_p.reshape(bp, 1, sp, N_HEADS, HEAD_DIM)
    new_h = h_last.reshape(bp, 1, 2, D_RG)
    return (y_prompt, y_sample, new_k, new_v, new_h)
```

```python
import functools

import numpy as np
import jax
import jax.numpy as jnp
from jax import lax
from jax.experimental import pallas as pl
from jax.experimental.pallas import tpu as pltpu

F32 = jnp.float32
BF16 = jnp.bfloat16

D = 1024
D_ATT = 512
N_HEADS = 8
HEAD_DIM = 64
GRID_W = 64
WIN_H = 8
WIN_W = 16
D_RG = 512
D_IN = 3 * D_ATT + 2 * D_RG
N_KEYS = 128
P_HEADS = 8
TOPK = 16
N_EXPERTS = N_KEYS * N_KEYS
RMS_EPS = 1e-6
NEG_INF = -1e30
RG_C = 8.0
ATT_SCALE = HEAD_DIM ** -0.5

VMEM_LIMIT = 56 * 1024 * 1024

_NT = (((1,), (1,)), ((), ()))


def _cparams(*sem, flags=None):
    return pltpu.CompilerParams(dimension_semantics=sem, vmem_limit_bytes=VMEM_LIMIT, flags=flags)


def _rmsnorm(x, w):
    ms = jnp.mean(x * x, axis=-1, keepdims=True)
    return x * lax.rsqrt(ms + RMS_EPS) * w


def _mod_kernel(c_ref, w_ref, b_ref, o_ref):
    cv = c_ref[...]
    s = cv * jax.nn.sigmoid(cv)
    o_ref[...] = jnp.dot(s.astype(BF16), w_ref[...].astype(BF16),
                         preferred_element_type=F32) + b_ref[...]


def _mod_call(cvec, w_mod, b_mod):
    rows = cvec.shape[0]
    tn = 1536
    return pl.pallas_call(
        _mod_kernel,
        grid=(6 * D // tn,),
        in_specs=[pl.BlockSpec((rows, D), lambda j: (0, 0)),
                  pl.BlockSpec((D, tn), lambda j: (0, j)),
                  pl.BlockSpec((1, tn), lambda j: (0, j))],
        out_specs=pl.BlockSpec((rows, tn), lambda j: (0, j)),
        out_shape=jax.ShapeDtypeStruct((rows, 6 * D), F32),
        compiler_params=_cparams("arbitrary"),
        name="mod",
    )(cvec, w_mod, b_mod)


TM_INPROJ = 512


def _inproj_kernel(x_ref, mod_ref, nw_ref, w_ref, *out_refs, emit_f32_kv):
    if emit_f32_kv:
        q_ref, kb_ref, vb_ref, xr_ref, yr_ref, kf_ref, vf_ref = out_refs
    else:
        q_ref, kb_ref, vb_ref, xr_ref, yr_ref = out_refs
    x = x_ref[...]
    h = _rmsnorm(x, nw_ref[...]) * (1.0 + mod_ref[0, 1:2, :]) + mod_ref[0, 0:1, :]
    p = jnp.dot(h.astype(BF16), w_ref[...], preferred_element_type=F32)
    q_ref[...] = (p[:, 0:D_ATT] * ATT_SCALE).astype(BF16)
    k = p[:, D_ATT:2 * D_ATT]
    v = p[:, 2 * D_ATT:3 * D_ATT]
    kb_ref[...] = k.astype(BF16)
    vb_ref[...] = v.astype(BF16)
    xr_ref[...] = p[:, 3 * D_ATT:3 * D_ATT + D_RG]
    yr_ref[...] = p[:, 3 * D_ATT + D_RG:]
    if emit_f32_kv:
        kf_ref[...] = k
        vf_ref[...] = v


def _inproj_call(x2d, mod3, mod_base, tiles_per_batch, norm_w, w_in_bf, emit_f32_kv):
    n = x2d.shape[0]
    tm = TM_INPROJ
    row = lambda t: (mod_base + t // tiles_per_batch, 0, 0)
    tok = lambda t: (t, 0)
    shp = lambda w, dt: jax.ShapeDtypeStruct((n, w), dt)
    out_specs = [pl.BlockSpec((tm, D_ATT), tok)] * 3 + [pl.BlockSpec((tm, D_RG), tok)] * 2
    out_shape = [shp(D_ATT, BF16)] * 3 + [shp(D_RG, F32)] * 2
    if emit_f32_kv:
        out_specs += [pl.BlockSpec((tm, D_ATT), tok)] * 2
        out_shape += [shp(D_ATT, F32)] * 2
    return pl.pallas_call(
        functools.partial(_inproj_kernel, emit_f32_kv=emit_f32_kv),
        grid=(n // tm,),
        in_specs=[pl.BlockSpec((tm, D), tok),
                  pl.BlockSpec((1, 6, D), row),
                  pl.BlockSpec((1, D), lambda t: (0, 0)),
                  pl.BlockSpec((D, D_IN), lambda t: (0, 0))],
        out_specs=out_specs,
        out_shape=out_shape,
        compiler_params=_cparams("arbitrary"),
        name="inproj",
    )(x2d, mod3, norm_w, w_in_bf)


def _ctx_attn_kernel(q_ref, k_ref, v_ref, o_ref):
    for h in range(N_HEADS):
        sl = slice(h * HEAD_DIM, (h + 1) * HEAD_DIM)
        s = lax.dot_general(q_ref[:, sl], k_ref[:, sl], _NT, preferred_element_type=F32)
        m = jnp.max(s, axis=-1, keepdims=True)
        p = jnp.exp(s - m)
        l = jnp.sum(p, axis=-1, keepdims=True)
        o = jnp.dot(p.astype(BF16), v_ref[:, sl], preferred_element_type=F32)
        o_ref[:, sl] = o / l


def _ctx_attn_call(q, kb, vb, seq):
    n = q.shape[0]
    blk = pl.BlockSpec((seq, D_ATT), lambda b: (b, 0))
    return pl.pallas_call(
        _ctx_attn_kernel,
        grid=(n // seq,),
        in_specs=[blk, blk, blk],
        out_specs=blk,
        out_shape=jax.ShapeDtypeStruct((n, D_ATT), F32),
        compiler_params=_cparams("arbitrary"),
        name="ctx_attn",
    )(q, kb, vb)


ROWS_PER_STEP = 4
WIN_UNION = 12


def _nbr_attn_kernel(q_ref, k_ref, v_ref, ck_ref, cv_ref, tz_ref, o_ref, *, rows):
    g = pl.program_id(0)
    n_loc = WIN_UNION * GRID_W
    first = jnp.clip(g * ROWS_PER_STEP - WIN_H // 2, 0, rows - WIN_H)
    kb = jnp.minimum((first >> 1) << 1, rows - WIN_UNION)
    row0 = pl.multiple_of(kb * GRID_W, 2 * GRID_W)
    lane = lax.broadcasted_iota(jnp.int32, (1, 2 * GRID_W), 1)
    d_idx, pen = [], []
    for i in range(ROWS_PER_STEP):
        r = g * ROWS_PER_STEP + i
        start = jnp.clip(r - WIN_H // 2, 0, rows - WIN_H)
        d_i, pen_i = [], []
        for p in range(WIN_UNION // 2):
            kr = kb + 2 * p
            d_i.append(jnp.clip(kr - r + (WIN_H - 1), 0, 2 * WIN_H - 3))
            in0 = (kr >= start) & (kr < start + WIN_H)
            in1 = (kr + 1 >= start) & (kr + 1 < start + WIN_H)
            pen_i.append(jnp.where(lane < GRID_W, jnp.where(in0, 0.0, NEG_INF), jnp.where(in1, 0.0, NEG_INF)))
        d_idx.append(d_i)
        pen.append(pen_i)
    outs = []
    for h in range(N_HEADS):
        sl = slice(h * HEAD_DIM, (h + 1) * HEAD_DIM)
        qg = q_ref[:, sl]
        bias = jnp.concatenate(
            [jnp.concatenate([tz_ref[h, d_idx[i][p]] + pen[i][p] for p in range(WIN_UNION // 2)], axis=-1)
             for i in range(ROWS_PER_STEP)], axis=0)
        s_loc = lax.dot_general(qg, k_ref[pl.ds(row0, n_loc), sl], _NT, preferred_element_type=F32) + bias
        s_ctx = lax.dot_general(qg, ck_ref[:, sl], _NT, preferred_element_type=F32)
        m = jnp.maximum(jnp.max(s_loc, axis=-1, keepdims=True), jnp.max(s_ctx, axis=-1, keepdims=True))
        p_loc = jnp.exp(s_loc - m)
        p_ctx = jnp.exp(s_ctx - m)
        l = jnp.sum(p_loc, axis=-1, keepdims=True) + jnp.sum(p_ctx, axis=-1, keepdims=True)
        o = (jnp.dot(p_loc.astype(BF16), v_ref[pl.ds(row0, n_loc), sl], preferred_element_type=F32)
             + jnp.dot(p_ctx.astype(BF16), cv_ref[:, sl], preferred_element_type=F32))
        outs.append(o / l)
    o_ref[...] = jnp.concatenate(outs, axis=-1)


def _nbr_attn_call(q, kb, vb, ck, cv, tz, batch, seq):
    rows = seq // GRID_W
    assert rows % 2 == 0 and rows >= WIN_UNION and rows % ROWS_PER_STEP == 0
    groups = rows // ROWS_PER_STEP
    past = ck.shape[0] // batch
    n = q.shape[0]
    qblk = pl.BlockSpec((GRID_W * ROWS_PER_STEP, D_ATT), lambda g, b: (b * groups + g, 0))
    kvblk = pl.BlockSpec((seq, D_ATT), lambda g, b: (b, 0))
    cblk = pl.BlockSpec((past, D_ATT), lambda g, b: (b, 0))
    return pl.pallas_call(
        functools.partial(_nbr_attn_kernel, rows=rows),
        grid=(groups, batch),
        in_specs=[qblk, kvblk, kvblk, cblk, cblk,
                  pl.BlockSpec(tz.shape, lambda g, b: (0, 0, 0, 0))],
        out_specs=qblk,
        out_shape=jax.ShapeDtypeStruct((n, D_ATT), F32),
        compiler_params=_cparams("arbitrary", "arbitrary"),
        name="nbr_attn",
    )(q, kb, vb, ck, cv, tz)


def _bias_toeplitz(rpb_l):
    qc = np.arange(GRID_W)
    off = qc[None, :] - qc[:, None] + (WIN_W - 1)
    place = (off[None, :, :] == np.arange(2 * WIN_W - 1)[:, None, None]).astype(np.float32)
    tz = jnp.einsum('hdx,xqk->hdqk', rpb_l, jnp.asarray(place), precision=lax.Precision.HIGHEST)
    cs = np.clip(qc - WIN_W // 2, 0, GRID_W - WIN_W)
    valid = (qc[None, :] >= cs[:, None]) & (qc[None, :] < cs[:, None] + WIN_W)
    tz = jnp.where(jnp.asarray(valid)[None, None], tz, NEG_INF)
    return jnp.concatenate([tz[:, :-1], tz[:, 1:]], axis=-1)


def _log_sigmoid(x):
    return jnp.minimum(x, 0.0) - jnp.log(1.0 + jnp.exp(-jnp.abs(x)))


def _gelu_tanh(x):
    return 0.5 * x * (1.0 + jnp.tanh(0.7978845608028654 * (x + 0.044715 * x * x * x)))


def _rglru_kernel(xr_ref, yr_ref, h0_ref, cw_ref, cb_ref, wbd_ref, ba_ref, bi_ref, lam_ref,
                  y_ref, hl_ref, xpad, a_f, b_f, a_b, b_b):
    T = xr_ref.shape[0]
    CH = 256
    xpad[0:8, :] = jnp.zeros((8, D_RG), F32)
    xpad[T + 8:T + 16, :] = jnp.zeros((8, D_RG), F32)
    xpad[8:T + 8, :] = xr_ref[...]
    a_refs = (a_f, a_b)
    b_refs = (b_f, b_b)
    for c0 in range(0, T, CH):
        xc = cb_ref[...] + cw_ref[0:1, :] * xpad[c0 + 6:c0 + 6 + CH, :]
        for i in range(1, 4):
            xc = xc + cw_ref[i:i + 1, :] * xpad[c0 + 6 + i:c0 + 6 + i + CH, :]
        xcb = xc.astype(BF16)
        for d in range(2):
            gates = []
            for g in range(2):
                halves = [jnp.dot(xcb[:, hf * 256:(hf + 1) * 256], wbd_ref[d, g, hf],
                                  preferred_element_type=F32) for hf in range(2)]
                gates.append(jnp.concatenate(halves, axis=-1))
            rg = jax.nn.sigmoid(gates[0] + ba_ref[d:d + 1, :])
            ig = jax.nn.sigmoid(gates[1] + bi_ref[d:d + 1, :])
            log_a = RG_C * rg * _log_sigmoid(lam_ref[d:d + 1, :])
            a = jnp.exp(log_a)
            b = jnp.sqrt(1.0 - a * a) * (ig * xc)
            a_refs[d][c0:c0 + CH, :] = a
            b_refs[d][c0:c0 + CH, :] = b

    sub = lax.broadcasted_iota(jnp.int32, (8, D_RG), 0)
    n_tiles = T // 8

    def scan_tile(a_ref, b_ref, t8, h, order):
        base = pl.multiple_of(t8 * 8, 8)
        a_blk = a_ref[pl.ds(base, 8), :]
        b_blk = b_ref[pl.ds(base, 8), :]
        out = b_blk
        for j in order:
            cand = a_blk * h + b_blk
            h = jnp.broadcast_to(cand[j:j + 1, :], (8, D_RG))
            out = jnp.where(sub == j, cand, out)
        b_ref[pl.ds(base, 8), :] = out
        return h

    def body(i, carry):
        hf, hb = carry
        hf = scan_tile(a_f, b_f, i, hf, range(8))
        hb = scan_tile(a_b, b_b, n_tiles - 1 - i, hb, range(7, -1, -1))
        return hf, hb

    hf0 = jnp.broadcast_to(h0_ref[0, 0:1, :], (8, D_RG))
    hb0 = jnp.broadcast_to(h0_ref[0, 1:2, :], (8, D_RG))
    hf, hb = lax.fori_loop(0, n_tiles, body, (hf0, hb0))
    hl_ref[0, 0:1, :] = hf[0:1, :]
    hl_ref[0, 1:2, :] = hb[0:1, :]
    for c0 in range(0, T, CH):
        sl = slice(c0, c0 + CH)
        y_ref[sl, :] = (b_f[sl, :] + b_b[sl, :]) * _gelu_tanh(yr_ref[sl, :])


def _rglru_call(xr, yr, h0, conv_w, conv_b, wbd, b_a, b_i, lam, seq):
    n = xr.shape[0]
    batch = n // seq
    tok = pl.BlockSpec((seq, D_RG), lambda b: (b, 0))
    full = lambda shape: pl.BlockSpec(shape, lambda b: (0,) * len(shape))
    return pl.pallas_call(
        _rglru_kernel,
        grid=(batch,),
        in_specs=[tok, tok, pl.BlockSpec((1, 2, D_RG), lambda b: (b, 0, 0)),
                  full((4, D_RG)), full((1, D_RG)), full((2, 2, 2, 256, 256)),
                  full((2, D_RG)), full((2, D_RG)), full((2, D_RG))],
        out_specs=[tok, pl.BlockSpec((1, 2, D_RG), lambda b: (b, 0, 0))],
        out_shape=[jax.ShapeDtypeStruct((n, D_RG), F32), jax.ShapeDtypeStruct((batch, 2, D_RG), F32)],
        scratch_shapes=[pltpu.VMEM((seq + 16, D_RG), F32)] + [pltpu.VMEM((seq, D_RG), F32)] * 4,
        compiler_params=_cparams("arbitrary"),
        name="rglru",
    )(xr, yr, h0, conv_w, conv_b, wbd, b_a, b_i, lam)


def _block_diag_halves(w):
    d = w.shape[0]
    w4 = w.reshape(d, 2, 4, 64, 64)
    eye = jnp.eye(4, dtype=w.dtype)
    out = jnp.einsum('dhncf,nm->dhncmf', w4, eye)
    return out.reshape(d, 2, 256, 256)


N_HP = 2 * P_HEADS


def _candidate_flat(tm):
    subf = lax.broadcasted_iota(jnp.int32, (8, tm), 0).astype(F32)
    flat = [subf + float(r1 * TOPK) for r1 in range(4)]
    flat.append(subf + 8.0)
    flat += [subf * float(TOPK) + float(r2) for r2 in range(3)]
    flat.append((subf + 8.0) * float(TOPK))
    return jnp.concatenate(flat, axis=0)


def _candidate_sums(t1, t2):
    tm = t1.shape[1]
    sub = lax.broadcasted_iota(jnp.int32, (8, tm), 0)
    ninf = -jnp.inf
    lim_a = (8, 8, 5, 4)
    vals = [jnp.where(sub < lim_a[r1], t1[r1:r1 + 1, :] + t2[0:8, :], ninf) for r1 in range(4)]
    vals.append(t1[0:1, :] + t2[8:16, :])
    for r2 in range(3):
        ok = (sub >= 4) if r2 < 2 else (sub == 4)
        vals.append(jnp.where(ok, t1[0:8, :] + t2[r2:r2 + 1, :], ninf))
    vals.append(t1[8:16, :] + t2[0:1, :])
    return jnp.concatenate(vals, axis=0)


def _staircase_rows(sel):
    rows = []
    for r1 in range(4):
        l = jnp.sum(sel[8 * r1:8 * r1 + 8, :], axis=0, keepdims=True)
        if r1 == 0:
            l = l + jnp.sum(sel[32:40, :], axis=0, keepdims=True)
        rows.append(l)
    l_mid = sel[40:48, :] + sel[48:56, :] + sel[56:64, :]
    l_hi = sel[64:72, :]
    rows += [l_mid[r1:r1 + 1, :] for r1 in range(4, 8)]
    rows += [l_hi[r1 - 8:r1 - 7, :] for r1 in range(8, 16)]
    return rows


def _batcher_pairs(n):
    pairs = []

    def merge(lo, cnt, r):
        step = r * 2
        if step < cnt:
            merge(lo, cnt, step)
            merge(lo + r, cnt, step)
            for i in range(lo + r, lo + cnt - r, step):
                pairs.append((i, i + r))
        else:
            pairs.append((lo, lo + r))

    def sort(lo, cnt):
        if cnt > 1:
            m = cnt // 2
            sort(lo, m)
            sort(lo + m, m)
            merge(lo, cnt, 1)

    sort(0, n)
    return pairs


_SORT16 = _batcher_pairs(TOPK)


def _top16_network(s):
    v = list(s)
    for i, j in _SORT16:
        v[i], v[j] = jnp.maximum(v[i], v[j]), jnp.minimum(v[i], v[j])
    for shift in (4, 2, 1):
        rolled = [pltpu.roll(x, shift, 0) for x in v]
        c = [jnp.maximum(v[k], rolled[TOPK - 1 - k]) for k in range(TOPK)]
        for d in (8, 4, 2, 1):
            for k in range(TOPK):
                if k & d == 0:
                    c[k], c[k + d] = jnp.maximum(c[k], c[k + d]), jnp.minimum(c[k], c[k + d])
        v = c
    return v


def _route_tile(sk_ref, qt_scr, s_scr, w_scr, rank_scr, t_scr, cand_scr, sel_scr, z_scr,
                a1_ref, c_ref, b2_ref, e2_ref):
    tm = qt_scr.shape[1]
    iota_f = lax.broadcasted_iota(jnp.int32, (N_KEYS, tm), 0).astype(F32)
    flat = _candidate_flat(tm)
    ninf = -jnp.inf
    for hp in range(N_HP):
        s_scr[hp] = jnp.dot(sk_ref[hp % 2], qt_scr[hp * N_KEYS:(hp + 1) * N_KEYS, :],
                            preferred_element_type=F32)

    def stage1_extract():
        for hp in range(N_HP):
            w_scr[hp] = s_scr[hp]
            rank_scr[hp] = jnp.full((N_KEYS, tm), float(TOPK), F32)

        def round1(r, carry):
            rf = jnp.asarray(r, jnp.int32).astype(F32)
            for hp in range(N_HP):
                w = w_scr[hp]
                m = jnp.max(w, axis=0, keepdims=True)
                idx = jnp.min(jnp.where(w == m, iota_f, float(N_KEYS)), axis=0, keepdims=True)
                hit = iota_f == idx
                rank_scr[hp] = jnp.where(hit, rf, rank_scr[hp])
                w_scr[hp] = jnp.where(hit, ninf, w)
                t_scr[hp, pl.ds(r, 1), :] = m
            return carry

        lax.fori_loop(0, TOPK, round1, 0)

    def stage1_network():
        lane_blocks = tm // 128
        sub = lax.broadcasted_iota(jnp.int32, (8, 128), 0)

        def body(h, flags):
            out = list(flags)
            for p in range(2):
                hp = 2 * h + p
                for lb in range(lane_blocks):
                    ls = slice(lb * 128, (lb + 1) * 128)
                    s = [s_scr[hp, 8 * v:8 * v + 8, ls] for v in range(N_KEYS // 8)]
                    t = _top16_network(s)
                    lo, hi = t[0], t[8]
                    for r in range(1, 8):
                        lo = jnp.where(sub == r, t[r], lo)
                        hi = jnp.where(sub == r, t[8 + r], hi)
                    t_scr[hp, 0:8, ls] = lo
                    t_scr[hp, 8:16, ls] = hi
                    if p == 1:
                        for v in range(N_KEYS // 8):
                            rank = jnp.zeros((8, 128), F32)
                            for r in range(TOPK):
                                rank = jnp.where(t[r] > s[v], float(r + 1), rank)
                            rank_scr[hp, 8 * v:8 * v + 8, ls] = rank
                    tie = out[lb]
                    for r in range(TOPK - 1):
                        tie = jnp.where(t[r] == t[r + 1], 1.0, tie)
                    out[lb] = tie
            return tuple(out)

        flags = lax.fori_loop(0, P_HEADS, body, (jnp.zeros((8, 128), F32),) * lane_blocks)
        return jnp.concatenate([f[0:1, :] for f in flags], axis=1)

    def stage2_extract():
        for h in range(P_HEADS):
            cand_scr[h] = _candidate_sums(t_scr[2 * h], t_scr[2 * h + 1])
            sel_scr[h] = jnp.zeros((72, tm), F32)

        def round2(r, carry):
            zs, b0s = carry
            zs_new, b0s_new = [], []
            for h in range(P_HEADS):
                w = cand_scr[h]
                m = jnp.max(w, axis=0, keepdims=True)
                f = jnp.min(jnp.where(w == m, flat, 1e9), axis=0, keepdims=True)
                hit = flat == f
                sel_scr[h] = jnp.where(hit, 1.0, sel_scr[h])
                cand_scr[h] = jnp.where(hit, ninf, w)
                b0 = jnp.where(r == 0, m, b0s[h])
                zs_new.append(zs[h] + jnp.exp(m - b0))
                b0s_new.append(b0)
            return tuple(zs_new), tuple(b0s_new)

        zero_row = jnp.zeros((1, tm), F32)
        zs, _ = lax.fori_loop(0, TOPK, round2, ((zero_row,) * P_HEADS, (zero_row,) * P_HEADS))
        for h in range(P_HEADS):
            z_scr[h] = jnp.broadcast_to(zs[h], (8, tm))

    def stage2_network():
        pad = jnp.full((8, 128), ninf, F32)

        def body(h, carry):
            for lb in range(tm // 128):
                ls = slice(lb * 128, (lb + 1) * 128)
                cand = _candidate_sums(t_scr[2 * h, :, ls], t_scr[2 * h + 1, :, ls])
                c = [cand[8 * v:8 * v + 8, :] for v in range(9)]
                top = _top16_network(c + [pad] * (TOPK - 9))
                for v in range(9):
                    sel_scr[h, 8 * v:8 * v + 8, ls] = jnp.where(c[v] >= top[TOPK - 1], 1.0, 0.0)
                z = jnp.ones((8, 128), F32)
                for r in range(1, TOPK):
                    z = z + jnp.exp(top[r] - top[0])
                z_scr[h, :, ls] = z
            return carry

        lax.fori_loop(0, P_HEADS, body, 0)

    excess = stage1_network()
    stage2_network()
    for hp in range(N_HP):
        if hp % 2 == 0:
            marks = jnp.where(s_scr[hp] >= t_scr[hp, TOPK - 1:TOPK, :], 1.0, 0.0)
        else:
            marks = jnp.where(rank_scr[hp] < float(TOPK), 1.0, 0.0)
        cnt = jnp.sum(marks, axis=0, keepdims=True)
        excess = jnp.maximum(excess, jnp.abs(cnt - float(TOPK)))
    for h in range(P_HEADS):
        cnt = jnp.sum(sel_scr[h], axis=0, keepdims=True)
        excess = jnp.maximum(excess, jnp.abs(cnt - float(TOPK)))
    tied = jnp.max(excess) > 0.0

    @pl.when(tied)
    def _():
        stage1_extract()
        stage2_extract()
        for h in range(P_HEADS):
            l_rows = _staircase_rows(sel_scr[h])
            rank1 = rank_scr[2 * h]
            a1 = jnp.zeros((N_KEYS, tm), F32)
            for r1 in range(TOPK):
                a1 = jnp.where(rank1 == float(r1), l_rows[r1], a1)
            a1_ref[h] = a1

    @pl.when(jnp.logical_not(tied))
    def finish_from_values():
        for h in range(P_HEADS):
            l_rows = _staircase_rows(sel_scr[h])
            s1 = s_scr[2 * h]
            a1 = jnp.zeros((N_KEYS, tm), F32)
            for r1 in range(TOPK - 1, -1, -1):
                a1 = jnp.where(s1 >= t_scr[2 * h, r1:r1 + 1, :], l_rows[r1], a1)
            a1_ref[h] = a1

    for h in range(P_HEADS):
        c_ref[h] = jnp.exp(s_scr[2 * h] - t_scr[2 * h, 0:1, :]) * (0.5 / z_scr[h, 0:1, :])
        b2_ref[h] = rank_scr[2 * h + 1].astype(BF16)
        e2_ref[h] = jnp.exp(s_scr[2 * h + 1] - t_scr[2 * h + 1, 0:1, :]).astype(BF16)


def _mixffn_kernel(x_ref, att_ref, rg_ref, mod_ref, wo_ref, nw_ref, wqt_ref, sk_ref,
                   x1_ref, h2t_ref, a1_ref, c_ref, b2_ref, e2_ref,
                   qt_scr, s_scr, w_scr, rank_scr, t_scr, cand_scr, sel_scr, z_scr):
    o = (jnp.dot(att_ref[...].astype(BF16), wo_ref[0:D_ATT, :], preferred_element_type=F32)
         + jnp.dot(rg_ref[...].astype(BF16), wo_ref[D_ATT:, :], preferred_element_type=F32))
    x1 = x_ref[...] + mod_ref[0, 2:3, :] * o
    x1_ref[...] = x1
    h2 = _rmsnorm(x1, nw_ref[...]) * (1.0 + mod_ref[0, 4:5, :]) + mod_ref[0, 3:4, :]
    h2t = h2.T.astype(BF16)
    h2t_ref[...] = h2t
    qt_scr[...] = jnp.dot(wqt_ref[...], h2t, preferred_element_type=F32).astype(BF16)
    _route_tile(sk_ref, qt_scr, s_scr, w_scr, rank_scr, t_scr, cand_scr, sel_scr, z_scr,
                a1_ref, c_ref, b2_ref, e2_ref)


def _mixffn_call(x2d, att, rg, mod3, mod_base, tiles_per_batch, w_out_bf, norm_w, w_qt_bf, sk_bf, tm):
    n = x2d.shape[0]
    tok = lambda t: (t, 0)
    route = pl.BlockSpec((P_HEADS, N_KEYS, tm), lambda t: (0, 0, t))
    route_shape = jax.ShapeDtypeStruct((P_HEADS, N_KEYS, n), F32)
    return pl.pallas_call(
        _mixffn_kernel,
        grid=(n // tm,),
        in_specs=[pl.BlockSpec((tm, D), tok),
                  pl.BlockSpec((tm, D_ATT), tok),
                  pl.BlockSpec((tm, D_RG), tok),
                  pl.BlockSpec((1, 6, D), lambda t: (mod_base + t // tiles_per_batch, 0, 0)),
                  pl.BlockSpec((D, D), lambda t: (0, 0)),
                  pl.BlockSpec((1, D), lambda t: (0, 0)),
                  pl.BlockSpec((2 * P_HEADS * N_KEYS, D), lambda t: (0, 0)),
                  pl.BlockSpec((2, N_KEYS, N_KEYS), lambda t: (0, 0, 0))],
        out_specs=[pl.BlockSpec((tm, D), tok),
                   pl.BlockSpec((D, tm), lambda t: (0, t)),
                   route, route, route, route],
        out_shape=[jax.ShapeDtypeStruct((n, D), F32), jax.ShapeDtypeStruct((D, n), BF16),
                   route_shape, route_shape,
                   jax.ShapeDtypeStruct((P_HEADS, N_KEYS, n), BF16), jax.ShapeDtypeStruct((P_HEADS, N_KEYS, n), BF16)],
        scratch_shapes=[pltpu.VMEM((2 * P_HEADS * N_KEYS, tm), BF16)]
                       + [pltpu.VMEM((N_HP, N_KEYS, tm), F32)] * 3
                       + [pltpu.VMEM((N_HP, TOPK, tm), F32)]
                       + [pltpu.VMEM((P_HEADS, 72, tm), F32)] * 2
                       + [pltpu.VMEM((P_HEADS, 8, tm), F32)],
        compiler_params=_cparams("arbitrary"),
        name="mixffn",
    )(x2d, att, rg, mod3, w_out_bf, norm_w, w_qt_bf, sk_bf)


def _expert_prep_kernel(u_ref, v_ref, ub_ref, vt_ref):
    ub_ref[...] = u_ref[...].astype(BF16)
    vt_ref[...] = v_ref[...].T.astype(BF16)


def _expert_prep_call(u, v):
    te = 512
    return pl.pallas_call(
        _expert_prep_kernel,
        grid=(N_EXPERTS // te,),
        in_specs=[pl.BlockSpec((te, D), lambda j: (j, 0)), pl.BlockSpec((te, D), lambda j: (j, 0))],
        out_specs=[pl.BlockSpec((te, D), lambda j: (j, 0)), pl.BlockSpec((D, te), lambda j: (0, j))],
        out_shape=[jax.ShapeDtypeStruct((N_EXPERTS, D), BF16), jax.ShapeDtypeStruct((D, N_EXPERTS), BF16)],
        compiler_params=_cparams("arbitrary"),
        name="expert_prep",
    )(u, v)


def _peer_kernel(h2t_ref, u_ref, vt_ref, a1_ref, c_ref, b2_ref, e2_ref, x1_ref, mod_ref, nw_ref,
                 y_ref, acc, g_scr, *, i1_per_step):
    j = pl.program_id(1)
    tm = acc.shape[1]

    @pl.when(j == 0)
    def _():
        acc[...] = jnp.zeros_like(acc)

    grp = min(i1_per_step, 8)
    one = jnp.ones((), BF16)
    zero = jnp.zeros((), BF16)
    for i0 in range(0, i1_per_step, grp):
        for l in range(tm // 128):
            ls = slice(l * 128, (l + 1) * 128)
            g = [None] * grp
            for h in range(P_HEADS):
                b2 = b2_ref[h, :, ls]
                e2 = e2_ref[h, :, ls]
                for k in range(grp):
                    a1 = a1_ref[h, i0 + k:i0 + k + 1, ls].astype(BF16)
                    ch = c_ref[h, i0 + k:i0 + k + 1, ls].astype(BF16)
                    m = jnp.minimum(jnp.maximum(a1 - b2, zero), one)
                    w = (ch * e2) * m
                    g[k] = w if g[k] is None else g[k] + w
            for k in range(grp):
                g_scr[(i0 + k) * N_KEYS:(i0 + k + 1) * N_KEYS, ls] = g[k]
    st = jnp.dot(u_ref[...], h2t_ref[...], preferred_element_type=F32)
    act = st * (1.0 + lax.erf(st * 0.7071067811865476))
    at = act.astype(BF16) * g_scr[...]
    acc[...] += jnp.dot(vt_ref[...], at, preferred_element_type=F32)

    @pl.when(j == pl.num_programs(1) - 1)
    def _():
        x2 = x1_ref[...] + mod_ref[0, 5:6, :] * acc[...].T
        y_ref[...] = _rmsnorm(x2, nw_ref[...])


def _peer_call(h2t, u_bf, vt_bf, a1, c, b2, e2, x1, mod3, mod_base, tiles_per_batch, norm_f_w, tm, te):
    n = x1.shape[0]
    ips = te // N_KEYS
    route_lo = pl.BlockSpec((P_HEADS, ips, tm), lambda t, j: (0, j, t))
    route_full = pl.BlockSpec((P_HEADS, N_KEYS, tm), lambda t, j: (0, 0, t))
    return pl.pallas_call(
        functools.partial(_peer_kernel, i1_per_step=ips),
        grid=(n // tm, N_EXPERTS // te),
        in_specs=[pl.BlockSpec((D, tm), lambda t, j: (0, t)),
                  pl.BlockSpec((te, D), lambda t, j: (j, 0)),
                  pl.BlockSpec((D, te), lambda t, j: (0, j)),
                  route_lo, route_lo, route_full, route_full,
                  pl.BlockSpec((tm, D), lambda t, j: (t, 0)),
                  pl.BlockSpec((1, 6, D), lambda t, j: (mod_base + t // tiles_per_batch, 0, 0)),
                  pl.BlockSpec((1, D), lambda t, j: (0, 0))],
        out_specs=pl.BlockSpec((tm, D), lambda t, j: (t, 0)),
        out_shape=jax.ShapeDtypeStruct((n, D), F32),
        scratch_shapes=[pltpu.VMEM((D, tm), F32), pltpu.VMEM((te, tm), BF16)],
        compiler_params=_cparams("arbitrary", "arbitrary"),
        name="peer",
    )(h2t, u_bf, vt_bf, a1, c, b2, e2, x1, mod3, norm_f_w)


def _path(x, mod3, mod_base, per_batch_mod, weights, attn_fn, h0, emit_cache):
    (norm_mix_w, w_in_bf, rg_params, w_out_bf, norm_ffn_w, w_qt_bf, sk_bf, u_bf, vt_bf, norm_f_w) = weights
    bsz, seq, _ = x.shape
    n = bsz * seq
    x2d = x.reshape(n, D)
    tpb = lambda tm: (seq // tm) if per_batch_mod else n
    q, kb, vb, xr, yr, *kv_f32 = _inproj_call(x2d, mod3, mod_base, tpb(TM_INPROJ), norm_mix_w, w_in_bf,
                                              emit_cache)
    att = attn_fn(q, kb, vb)
    rg_out, h_last = _rglru_call(xr, yr, h0, *rg_params, seq)
    tm_mix = 256
    x1, h2t, a1, c, b2, e2 = _mixffn_call(x2d, att, rg_out, mod3, mod_base, tpb(tm_mix), w_out_bf,
                                          norm_ffn_w, w_qt_bf, sk_bf, tm_mix)
    tm_peer = 1024
    y = _peer_call(h2t, u_bf, vt_bf, a1, c, b2, e2, x1, mod3, mod_base, tpb(tm_peer), norm_f_w,
                   tm_peer, 1024)
    return y.reshape(bsz, seq, D), kv_f32, h_last


def kernel(x_prompt, x_sample, c, cache_k, cache_v, state_rglru, c_ctx, w_mod, b_mod, norm_mix_w, w_in,
           rpb, conv_w, conv_b, rg_w_a, rg_b_a, rg_w_i, rg_b_i, rg_lambda, w_out, norm_ffn_w,
           peer_w_q, peer_sub_keys, peer_u, peer_v, norm_f_w):
    depth = w_mod.shape[0]
    assert depth == 1, "single-layer problem"
    l = 0
    bp, sp, _ = x_prompt.shape
    bs, ss, _ = x_sample.shape

    n_rows = 16
    cvec = jnp.concatenate([c_ctx[None, :], c, jnp.zeros((n_rows - 1 - bs, D), F32)], axis=0)
    mod3 = _mod_call(cvec, w_mod[l], b_mod[l][None, :]).reshape(n_rows, 6, D)

    wbd = jnp.stack([_block_diag_halves(rg_w_a[l]), _block_diag_halves(rg_w_i[l])], axis=1).astype(BF16)
    rg_params = (conv_w[l], conv_b[l][None, :], wbd, rg_b_a[l], rg_b_i[l], rg_lambda[l])
    u_bf, vt_bf = _expert_prep_call(peer_u[l], peer_v[l])
    weights = (norm_mix_w[l][None, :], w_in[l].astype(BF16), rg_params, w_out[l].astype(BF16),
               norm_ffn_w[l][None, :], peer_w_q[l].T.astype(BF16), peer_sub_keys[l].astype(BF16),
               u_bf, vt_bf, norm_f_w[None, :])

    ctx_attn = lambda q, kb, vb: _ctx_attn_call(q, kb, vb, sp)
    h0_p = jnp.zeros((bp, 2, D_RG), F32)
    y_prompt, (k_p, v_p), h_last = _path(x_prompt, mod3, 0, False, weights, ctx_attn, h0_p, True)

    tz = _bias_toeplitz(rpb[l])
    ck = cache_k[:, l].reshape(-1, D_ATT).astype(BF16)
    cv = cache_v[:, l].reshape(-1, D_ATT).astype(BF16)
    nbr_attn = lambda q, kb, vb: _nbr_attn_call(q, kb, vb, ck, cv, tz, bs, ss)
    y_sample, _, _ = _path(x_sample, mod3, 1, True, weights, nbr_attn, state_rglru[:, l], False)

    new_k = k_p.reshape(bp, 1, sp, N_HEADS, HEAD_DIM)
    new_v = v_p.reshape(bp, 1, sp, N_HEADS, HEAD_DIM)
    new_h = h_last.reshape(bp, 1, 2, D_RG)
    return (y_prompt, y_sample, new_k, new_v, new_h)
```

```python
import functools

import numpy as np
import jax
import jax.numpy as jnp
from jax import lax
from jax.experimental import pallas as pl
from jax.experimental.pallas import tpu as pltpu

F32 = jnp.float32
BF16 = jnp.bfloat16

D = 1024
D_ATT = 512
N_HEADS = 8
HEAD_DIM = 64
GRID_W = 64
WIN_H = 8
WIN_W = 16
D_RG = 512
D_IN = 3 * D_ATT + 2 * D_RG
N_KEYS = 128
P_HEADS = 8
TOPK = 16
N_EXPERTS = N_KEYS * N_KEYS
RMS_EPS = 1e-6
NEG_INF = -1e30
RG_C = 8.0
ATT_SCALE = HEAD_DIM ** -0.5

VMEM_LIMIT = 56 * 1024 * 1024

_NT = (((1,), (1,)), ((), ()))


def _cparams(*sem, flags=None):
    return pltpu.CompilerParams(dimension_semantics=sem, vmem_limit_bytes=VMEM_LIMIT, flags=flags)


def _rmsnorm(x, w):
    ms = jnp.mean(x * x, axis=-1, keepdims=True)
    return x * lax.rsqrt(ms + RMS_EPS) * w


def _mod_kernel(c_ref, w_ref, b_ref, o_ref):
    cv = c_ref[...]
    s = cv * jax.nn.sigmoid(cv)
    o_ref[...] = jnp.dot(s.astype(BF16), w_ref[...].astype(BF16),
                         preferred_element_type=F32) + b_ref[...]


def _mod_call(cvec, w_mod, b_mod):
    rows = cvec.shape[0]
    tn = 1536
    return pl.pallas_call(
        _mod_kernel,
        grid=(6 * D // tn,),
        in_specs=[pl.BlockSpec((rows, D), lambda j: (0, 0)),
                  pl.BlockSpec((D, tn), lambda j: (0, j)),
                  pl.BlockSpec((1, tn), lambda j: (0, j))],
        out_specs=pl.BlockSpec((rows, tn), lambda j: (0, j)),
        out_shape=jax.ShapeDtypeStruct((rows, 6 * D), F32),
        compiler_params=_cparams("arbitrary"),
        name="mod",
    )(cvec, w_mod, b_mod)


TM_INPROJ = 512


def _inproj_kernel(x_ref, mod_ref, nw_ref, w_ref, *out_refs, emit_f32_kv):
    if emit_f32_kv:
        q_ref, kb_ref, vb_ref, xr_ref, yr_ref, kf_ref, vf_ref = out_refs
    else:
        q_ref, kb_ref, vb_ref, xr_ref, yr_ref = out_refs
    x = x_ref[...]
    h = _rmsnorm(x, nw_ref[...]) * (1.0 + mod_ref[0, 1:2, :]) + mod_ref[0, 0:1, :]
    p = jnp.dot(h.astype(BF16), w_ref[...], preferred_element_type=F32)
    q_ref[...] = (p[:, 0:D_ATT] * ATT_SCALE).astype(BF16)
    k = p[:, D_ATT:2 * D_ATT]
    v = p[:, 2 * D_ATT:3 * D_ATT]
    kb_ref[...] = k.astype(BF16)
    vb_ref[...] = v.astype(BF16)
    xr_ref[...] = p[:, 3 * D_ATT:3 * D_ATT + D_RG]
    yr_ref[...] = p[:, 3 * D_ATT + D_RG:]
    if emit_f32_kv:
        kf_ref[...] = k
        vf_ref[...] = v


def _inproj_call(x2d, mod3, mod_base, tiles_per_batch, norm_w, w_in_bf, emit_f32_kv):
    n = x2d.shape[0]
    tm = TM_INPROJ
    row = lambda t: (mod_base + t // tiles_per_batch, 0, 0)
    tok = lambda t: (t, 0)
    shp = lambda w, dt: jax.ShapeDtypeStruct((n, w), dt)
    out_specs = [pl.BlockSpec((tm, D_ATT), tok)] * 3 + [pl.BlockSpec((tm, D_RG), tok)] * 2
    out_shape = [shp(D_ATT, BF16)] * 3 + [shp(D_RG, F32)] * 2
    if emit_f32_kv:
        out_specs += [pl.BlockSpec((tm, D_ATT), tok)] * 2
        out_shape += [shp(D_ATT, F32)] * 2
    return pl.pallas_call(
        functools.partial(_inproj_kernel, emit_f32_kv=emit_f32_kv),
        grid=(n // tm,),
        in_specs=[pl.BlockSpec((tm, D), tok),
                  pl.BlockSpec((1, 6, D), row),
                  pl.BlockSpec((1, D), lambda t: (0, 0)),
                  pl.BlockSpec((D, D_IN), lambda t: (0, 0))],
        out_specs=out_specs,
        out_shape=out_shape,
        compiler_params=_cparams("arbitrary"),
        name="inproj",
    )(x2d, mod3, norm_w, w_in_bf)


def _ctx_attn_kernel(q_ref, k_ref, v_ref, o_ref):
    for h in range(N_HEADS):
        sl = slice(h * HEAD_DIM, (h + 1) * HEAD_DIM)
        s = lax.dot_general(q_ref[:, sl], k_ref[:, sl], _NT, preferred_element_type=F32)
        m = jnp.max(s, axis=-1, keepdims=True)
        p = jnp.exp(s - m)
        l = jnp.sum(p, axis=-1, keepdims=True)
        o = jnp.dot(p.astype(BF16), v_ref[:, sl], preferred_element_type=F32)
        o_ref[:, sl] = o / l


def _ctx_attn_call(q, kb, vb, seq):
    n = q.shape[0]
    blk = pl.BlockSpec((seq, D_ATT), lambda b: (b, 0))
    return pl.pallas_call(
        _ctx_attn_kernel,
        grid=(n // seq,),
        in_specs=[blk, blk, blk],
        out_specs=blk,
        out_shape=jax.ShapeDtypeStruct((n, D_ATT), F32),
        compiler_params=_cparams("arbitrary"),
        name="ctx_attn",
    )(q, kb, vb)


ROWS_PER_STEP = 8
WIN_UNION = 12


def _nbr_attn_kernel(q_ref, k_ref, v_ref, ck_ref, cv_ref, tz_ref, o_ref, *, rows):
    g = pl.program_id(0)
    n_loc = WIN_UNION * GRID_W
    first = jnp.clip(g * ROWS_PER_STEP - WIN_H // 2, 0, rows - WIN_H)
    kb = jnp.minimum((first >> 1) << 1, rows - WIN_UNION)
    row0 = pl.multiple_of(kb * GRID_W, 2 * GRID_W)
    lane = lax.broadcasted_iota(jnp.int32, (1, 2 * GRID_W), 1)
    d_idx, pen = [], []
    for i in range(ROWS_PER_STEP):
        r = g * ROWS_PER_STEP + i
        start = jnp.clip(r - WIN_H // 2, 0, rows - WIN_H)
        d_i, pen_i = [], []
        for p in range(WIN_UNION // 2):
            kr = kb + 2 * p
            d_i.append(jnp.clip(kr - r + (WIN_H - 1), 0, 2 * WIN_H - 3))
            in0 = (kr >= start) & (kr < start + WIN_H)
            in1 = (kr + 1 >= start) & (kr + 1 < start + WIN_H)
            pen_i.append(jnp.where(lane < GRID_W, jnp.where(in0, 0.0, NEG_INF), jnp.where(in1, 0.0, NEG_INF)))
        d_idx.append(d_i)
        pen.append(pen_i)
    outs = []
    for h in range(N_HEADS):
        sl = slice(h * HEAD_DIM, (h + 1) * HEAD_DIM)
        qg = q_ref[:, sl]
        bias = jnp.concatenate(
            [jnp.concatenate([tz_ref[h, d_idx[i][p]] + pen[i][p] for p in range(WIN_UNION // 2)], axis=-1)
             for i in range(ROWS_PER_STEP)], axis=0)
        s_loc = lax.dot_general(qg, k_ref[pl.ds(row0, n_loc), sl], _NT, preferred_element_type=F32) + bias
        s_ctx = lax.dot_general(qg, ck_ref[:, sl], _NT, preferred_element_type=F32)
        m = jnp.maximum(jnp.max(s_loc, axis=-1, keepdims=True), jnp.max(s_ctx, axis=-1, keepdims=True))
        p_loc = jnp.exp(s_loc - m)
        p_ctx = jnp.exp(s_ctx - m)
        l = jnp.sum(p_loc, axis=-1, keepdims=True) + jnp.sum(p_ctx, axis=-1, keepdims=True)
        o = (jnp.dot(p_loc.astype(BF16), v_ref[pl.ds(row0, n_loc), sl], preferred_element_type=F32)
             + jnp.dot(p_ctx.astype(BF16), cv_ref[:, sl], preferred_element_type=F32))
        outs.append(o / l)
    o_ref[...] = jnp.concatenate(outs, axis=-1)


def _nbr_attn_call(q, kb, vb, ck, cv, tz, batch, seq):
    rows = seq // GRID_W
    assert rows % 2 == 0 and rows >= WIN_UNION and rows % ROWS_PER_STEP == 0
    for g in range(rows // ROWS_PER_STEP):
        lo = min(max(g * ROWS_PER_STEP - WIN_H // 2, 0), rows - WIN_H)
        hi = min(max((g + 1) * ROWS_PER_STEP - 1 - WIN_H // 2, 0), rows - WIN_H) + WIN_H
        first_key_row = min(lo // 2 * 2, rows - WIN_UNION)
        assert first_key_row <= lo and hi <= first_key_row + WIN_UNION, (g, lo, hi)
    groups = rows // ROWS_PER_STEP
    past = ck.shape[0] // batch
    n = q.shape[0]
    qblk = pl.BlockSpec((GRID_W * ROWS_PER_STEP, D_ATT), lambda g, b: (b * groups + g, 0))
    kvblk = pl.BlockSpec((seq, D_ATT), lambda g, b: (b, 0))
    cblk = pl.BlockSpec((past, D_ATT), lambda g, b: (b, 0))
    return pl.pallas_call(
        functools.partial(_nbr_attn_kernel, rows=rows),
        grid=(groups, batch),
        in_specs=[qblk, kvblk, kvblk, cblk, cblk,
                  pl.BlockSpec(tz.shape, lambda g, b: (0, 0, 0, 0))],
        out_specs=qblk,
        out_shape=jax.ShapeDtypeStruct((n, D_ATT), F32),
        compiler_params=_cparams("arbitrary", "arbitrary"),
        name="nbr_attn",
    )(q, kb, vb, ck, cv, tz)


def _bias_toeplitz(rpb_l):
    qc = np.arange(GRID_W)
    off = qc[None, :] - qc[:, None] + (WIN_W - 1)
    place = (off[None, :, :] == np.arange(2 * WIN_W - 1)[:, None, None]).astype(np.float32)
    tz = jnp.einsum('hdx,xqk->hdqk', rpb_l, jnp.asarray(place), precision=lax.Precision.HIGHEST)
    cs = np.clip(qc - WIN_W // 2, 0, GRID_W - WIN_W)
    valid = (qc[None, :] >= cs[:, None]) & (qc[None, :] < cs[:, None] + WIN_W)
    tz = jnp.where(jnp.asarray(valid)[None, None], tz, NEG_INF)
    return jnp.concatenate([tz[:, :-1], tz[:, 1:]], axis=-1)


def _log_sigmoid(x):
    return jnp.minimum(x, 0.0) - jnp.log(1.0 + jnp.exp(-jnp.abs(x)))


def _gelu_tanh(x):
    return 0.5 * x * (1.0 + jnp.tanh(0.7978845608028654 * (x + 0.044715 * x * x * x)))


def _rglru_kernel(xr_ref, yr_ref, h0_ref, cw_ref, cb_ref, wbd_ref, ba_ref, bi_ref, lam_ref,
                  y_ref, hl_ref, xpad, a_f, b_f, a_b, b_b):
    T = xr_ref.shape[0]
    CH = 256
    xpad[0:8, :] = jnp.zeros((8, D_RG), F32)
    xpad[T + 8:T + 16, :] = jnp.zeros((8, D_RG), F32)
    xpad[8:T + 8, :] = xr_ref[...]
    a_refs = (a_f, a_b)
    b_refs = (b_f, b_b)
    for c0 in range(0, T, CH):
        xc = cb_ref[...] + cw_ref[0:1, :] * xpad[c0 + 6:c0 + 6 + CH, :]
        for i in range(1, 4):
            xc = xc + cw_ref[i:i + 1, :] * xpad[c0 + 6 + i:c0 + 6 + i + CH, :]
        xcb = xc.astype(BF16)
        for d in range(2):
            gates = []
            for g in range(2):
                halves = [jnp.dot(xcb[:, hf * 256:(hf + 1) * 256], wbd_ref[d, g, hf],
                                  preferred_element_type=F32) for hf in range(2)]
                gates.append(jnp.concatenate(halves, axis=-1))
            rg = jax.nn.sigmoid(gates[0] + ba_ref[d:d + 1, :])
            ig = jax.nn.sigmoid(gates[1] + bi_ref[d:d + 1, :])
            log_a = RG_C * rg * _log_sigmoid(lam_ref[d:d + 1, :])
            a = jnp.exp(log_a)
            b = jnp.sqrt(1.0 - a * a) * (ig * xc)
            a_refs[d][c0:c0 + CH, :] = a
            b_refs[d][c0:c0 + CH, :] = b

    sub = lax.broadcasted_iota(jnp.int32, (8, D_RG), 0)
    n_tiles = T // 8

    def scan_tile(a_ref, b_ref, t8, h, order):
        base = pl.multiple_of(t8 * 8, 8)
        a_blk = a_ref[pl.ds(base, 8), :]
        b_blk = b_ref[pl.ds(base, 8), :]
        out = b_blk
        for j in order:
            cand = a_blk * h + b_blk
            h = jnp.broadcast_to(cand[j:j + 1, :], (8, D_RG))
            out = jnp.where(sub == j, cand, out)
        b_ref[pl.ds(base, 8), :] = out
        return h

    def body(i, carry):
        hf, hb = carry
        hf = scan_tile(a_f, b_f, i, hf, range(8))
        hb = scan_tile(a_b, b_b, n_tiles - 1 - i, hb, range(7, -1, -1))
        return hf, hb

    hf0 = jnp.broadcast_to(h0_ref[0, 0:1, :], (8, D_RG))
    hb0 = jnp.broadcast_to(h0_ref[0, 1:2, :], (8, D_RG))
    hf, hb = lax.fori_loop(0, n_tiles, body, (hf0, hb0))
    hl_ref[0, 0:1, :] = hf[0:1, :]
    hl_ref[0, 1:2, :] = hb[0:1, :]
    for c0 in range(0, T, CH):
        sl = slice(c0, c0 + CH)
        y_ref[sl, :] = (b_f[sl, :] + b_b[sl, :]) * _gelu_tanh(yr_ref[sl, :])


def _rglru_call(xr, yr, h0, conv_w, conv_b, wbd, b_a, b_i, lam, seq):
    n = xr.shape[0]
    batch = n // seq
    tok = pl.BlockSpec((seq, D_RG), lambda b: (b, 0))
    full = lambda shape: pl.BlockSpec(shape, lambda b: (0,) * len(shape))
    return pl.pallas_call(
        _rglru_kernel,
        grid=(batch,),
        in_specs=[tok, tok, pl.BlockSpec((1, 2, D_RG), lambda b: (b, 0, 0)),
                  full((4, D_RG)), full((1, D_RG)), full((2, 2, 2, 256, 256)),
                  full((2, D_RG)), full((2, D_RG)), full((2, D_RG))],
        out_specs=[tok, pl.BlockSpec((1, 2, D_RG), lambda b: (b, 0, 0))],
        out_shape=[jax.ShapeDtypeStruct((n, D_RG), F32), jax.ShapeDtypeStruct((batch, 2, D_RG), F32)],
        scratch_shapes=[pltpu.VMEM((seq + 16, D_RG), F32)] + [pltpu.VMEM((seq, D_RG), F32)] * 4,
        compiler_params=_cparams("arbitrary"),
        name="rglru",
    )(xr, yr, h0, conv_w, conv_b, wbd, b_a, b_i, lam)


def _block_diag_halves(w):
    d = w.shape[0]
    w4 = w.reshape(d, 2, 4, 64, 64)
    eye = jnp.eye(4, dtype=w.dtype)
    out = jnp.einsum('dhncf,nm->dhncmf', w4, eye)
    return out.reshape(d, 2, 256, 256)


N_HP = 2 * P_HEADS


def _candidate_flat(tm):
    subf = lax.broadcasted_iota(jnp.int32, (8, tm), 0).astype(F32)
    flat = [subf + float(r1 * TOPK) for r1 in range(4)]
    flat.append(subf + 8.0)
    flat += [subf * float(TOPK) + float(r2) for r2 in range(3)]
    flat.append((subf + 8.0) * float(TOPK))
    return jnp.concatenate(flat, axis=0)


def _candidate_sums(t1, t2):
    tm = t1.shape[1]
    sub = lax.broadcasted_iota(jnp.int32, (8, tm), 0)
    ninf = -jnp.inf
    lim_a = (8, 8, 5, 4)
    vals = [jnp.where(sub < lim_a[r1], t1[r1:r1 + 1, :] + t2[0:8, :], ninf) for r1 in range(4)]
    vals.append(t1[0:1, :] + t2[8:16, :])
    for r2 in range(3):
        ok = (sub >= 4) if r2 < 2 else (sub == 4)
        vals.append(jnp.where(ok, t1[0:8, :] + t2[r2:r2 + 1, :], ninf))
    vals.append(t1[8:16, :] + t2[0:1, :])
    return jnp.concatenate(vals, axis=0)


def _staircase_rows(sel):
    rows = []
    for r1 in range(4):
        l = jnp.sum(sel[8 * r1:8 * r1 + 8, :], axis=0, keepdims=True)
        if r1 == 0:
            l = l + jnp.sum(sel[32:40, :], axis=0, keepdims=True)
        rows.append(l)
    l_mid = sel[40:48, :] + sel[48:56, :] + sel[56:64, :]
    l_hi = sel[64:72, :]
    rows += [l_mid[r1:r1 + 1, :] for r1 in range(4, 8)]
    rows += [l_hi[r1 - 8:r1 - 7, :] for r1 in range(8, 16)]
    return rows


def _batcher_pairs(n):
    pairs = []

    def merge(lo, cnt, r):
        step = r * 2
        if step < cnt:
            merge(lo, cnt, step)
            merge(lo + r, cnt, step)
            for i in range(lo + r, lo + cnt - r, step):
                pairs.append((i, i + r))
        else:
            pairs.append((lo, lo + r))

    def sort(lo, cnt):
        if cnt > 1:
            m = cnt // 2
            sort(lo, m)
            sort(lo + m, m)
            merge(lo, cnt, 1)

    sort(0, n)
    return pairs


_SORT16 = _batcher_pairs(TOPK)


def _top16_network(s):
    v = list(s)
    for i, j in _SORT16:
        v[i], v[j] = jnp.maximum(v[i], v[j]), jnp.minimum(v[i], v[j])
    for shift in (4, 2, 1):
        rolled = [pltpu.roll(x, shift, 0) for x in v]
        c = [jnp.maximum(v[k], rolled[TOPK - 1 - k]) for k in range(TOPK)]
        for d in (8, 4, 2, 1):
            for k in range(TOPK):
                if k & d == 0:
                    c[k], c[k + d] = jnp.maximum(c[k], c[k + d]), jnp.minimum(c[k], c[k + d])
        v = c
    return v


def _route_tile(sk_ref, qt_scr, s_scr, w_scr, rank_scr, t_scr, cand_scr, sel_scr, z_scr,
                a1_ref, c_ref, b2_ref, e2_ref):
    tm = qt_scr.shape[1]
    iota_f = lax.broadcasted_iota(jnp.int32, (N_KEYS, tm), 0).astype(F32)
    flat = _candidate_flat(tm)
    ninf = -jnp.inf
    for hp in range(N_HP):
        s_scr[hp] = jnp.dot(sk_ref[hp % 2], qt_scr[hp * N_KEYS:(hp + 1) * N_KEYS, :],
                            preferred_element_type=F32)

    def stage1_extract():
        for hp in range(N_HP):
            w_scr[hp] = s_scr[hp]
            rank_scr[hp] = jnp.full((N_KEYS, tm), float(TOPK), F32)

        def round1(r, carry):
            rf = jnp.asarray(r, jnp.int32).astype(F32)
            for hp in range(N_HP):
                w = w_scr[hp]
                m = jnp.max(w, axis=0, keepdims=True)
                idx = jnp.min(jnp.where(w == m, iota_f, float(N_KEYS)), axis=0, keepdims=True)
                hit = iota_f == idx
                rank_scr[hp] = jnp.where(hit, rf, rank_scr[hp])
                w_scr[hp] = jnp.where(hit, ninf, w)
                t_scr[hp, pl.ds(r, 1), :] = m
            return carry

        lax.fori_loop(0, TOPK, round1, 0)

    def stage1_network():
        lane_blocks = tm // 128
        sub = lax.broadcasted_iota(jnp.int32, (8, 128), 0)

        def body(h, flags):
            out = list(flags)
            for p in range(2):
                hp = 2 * h + p
                for lb in range(lane_blocks):
                    ls = slice(lb * 128, (lb + 1) * 128)
                    s = [s_scr[hp, 8 * v:8 * v + 8, ls] for v in range(N_KEYS // 8)]
                    t = _top16_network(s)
                    lo, hi = t[0], t[8]
                    for r in range(1, 8):
                        lo = jnp.where(sub == r, t[r], lo)
                        hi = jnp.where(sub == r, t[8 + r], hi)
                    t_scr[hp, 0:8, ls] = lo
                    t_scr[hp, 8:16, ls] = hi
                    if p == 1:
                        for v in range(N_KEYS // 8):
                            rank = jnp.zeros((8, 128), F32)
                            for r in range(TOPK):
                                rank = jnp.where(t[r] > s[v], float(r + 1), rank)
                            rank_scr[hp, 8 * v:8 * v + 8, ls] = rank
                    tie = out[lb]
                    for r in range(TOPK - 1):
                        tie = jnp.where(t[r] == t[r + 1], 1.0, tie)
                    out[lb] = tie
            return tuple(out)

        flags = lax.fori_loop(0, P_HEADS, body, (jnp.zeros((8, 128), F32),) * lane_blocks)
        return jnp.concatenate([f[0:1, :] for f in flags], axis=1)

    def stage2_extract():
        for h in range(P_HEADS):
            cand_scr[h] = _candidate_sums(t_scr[2 * h], t_scr[2 * h + 1])
            sel_scr[h] = jnp.zeros((72, tm), F32)

        def round2(r, carry):
            zs, b0s = carry
            zs_new, b0s_new = [], []
            for h in range(P_HEADS):
                w = cand_scr[h]
                m = jnp.max(w, axis=0, keepdims=True)
                f = jnp.min(jnp.where(w == m, flat, 1e9), axis=0, keepdims=True)
                hit = flat == f
                sel_scr[h] = jnp.where(hit, 1.0, sel_scr[h])
                cand_scr[h] = jnp.where(hit, ninf, w)
                b0 = jnp.where(r == 0, m, b0s[h])
                zs_new.append(zs[h] + jnp.exp(m - b0))
                b0s_new.append(b0)
            return tuple(zs_new), tuple(b0s_new)

        zero_row = jnp.zeros((1, tm), F32)
        zs, _ = lax.fori_loop(0, TOPK, round2, ((zero_row,) * P_HEADS, (zero_row,) * P_HEADS))
        for h in range(P_HEADS):
            z_scr[h] = jnp.broadcast_to(zs[h], (8, tm))

    def stage2_network():
        pad = jnp.full((8, 128), ninf, F32)

        def body(h, carry):
            for lb in range(tm // 128):
                ls = slice(lb * 128, (lb + 1) * 128)
                cand = _candidate_sums(t_scr[2 * h, :, ls], t_scr[2 * h + 1, :, ls])
                c = [cand[8 * v:8 * v + 8, :] for v in range(9)]
                top = _top16_network(c + [pad] * (TOPK - 9))
                for v in range(9):
                    sel_scr[h, 8 * v:8 * v + 8, ls] = jnp.where(c[v] >= top[TOPK - 1], 1.0, 0.0)
                z = jnp.ones((8, 128), F32)
                for r in range(1, TOPK):
                    z = z + jnp.exp(top[r] - top[0])
                z_scr[h, :, ls] = z
            return carry

        lax.fori_loop(0, P_HEADS, body, 0)

    excess = stage1_network()
    stage2_network()
    for hp in range(N_HP):
        if hp % 2 == 0:
            marks = jnp.where(s_scr[hp] >= t_scr[hp, TOPK - 1:TOPK, :], 1.0, 0.0)
        else:
            marks = jnp.where(rank_scr[hp] < float(TOPK), 1.0, 0.0)
        cnt = jnp.sum(marks, axis=0, keepdims=True)
        excess = jnp.maximum(excess, jnp.abs(cnt - float(TOPK)))
    for h in range(P_HEADS):
        cnt = jnp.sum(sel_scr[h], axis=0, keepdims=True)
        excess = jnp.maximum(excess, jnp.abs(cnt - float(TOPK)))
    tied = jnp.max(excess) > 0.0

    @pl.when(tied)
    def _():
        stage1_extract()
        stage2_extract()
        for h in range(P_HEADS):
            l_rows = _staircase_rows(sel_scr[h])
            rank1 = rank_scr[2 * h]
            a1 = jnp.zeros((N_KEYS, tm), F32)
            for r1 in range(TOPK):
                a1 = jnp.where(rank1 == float(r1), l_rows[r1], a1)
            a1_ref[h] = a1

    @pl.when(jnp.logical_not(tied))
    def finish_from_values():
        for h in range(P_HEADS):
            l_rows = _staircase_rows(sel_scr[h])
            s1 = s_scr[2 * h]
            a1 = jnp.zeros((N_KEYS, tm), F32)
            for r1 in range(TOPK - 1, -1, -1):
                a1 = jnp.where(s1 >= t_scr[2 * h, r1:r1 + 1, :], l_rows[r1], a1)
            a1_ref[h] = a1

    for h in range(P_HEADS):
        c_ref[h] = jnp.exp(s_scr[2 * h] - t_scr[2 * h, 0:1, :]) * (0.5 / z_scr[h, 0:1, :])
        b2_ref[h] = rank_scr[2 * h + 1].astype(BF16)
        e2_ref[h] = jnp.exp(s_scr[2 * h + 1] - t_scr[2 * h + 1, 0:1, :]).astype(BF16)


def _mixffn_kernel(x_ref, att_ref, rg_ref, mod_ref, wo_ref, nw_ref, wqt_ref, sk_ref,
                   x1_ref, h2t_ref, a1_ref, c_ref, b2_ref, e2_ref,
                   qt_scr, s_scr, w_scr, rank_scr, t_scr, cand_scr, sel_scr, z_scr):
    o = (jnp.dot(att_ref[...].astype(BF16), wo_ref[0:D_ATT, :], preferred_element_type=F32)
         + jnp.dot(rg_ref[...].astype(BF16), wo_ref[D_ATT:, :], preferred_element_type=F32))
    x1 = x_ref[...] + mod_ref[0, 2:3, :] * o
    x1_ref[...] = x1
    h2 = _rmsnorm(x1, nw_ref[...]) * (1.0 + mod_ref[0, 4:5, :]) + mod_ref[0, 3:4, :]
    h2t = h2.T.astype(BF16)
    h2t_ref[...] = h2t
    qt_scr[...] = jnp.dot(wqt_ref[...], h2t, preferred_element_type=F32).astype(BF16)
    _route_tile(sk_ref, qt_scr, s_scr, w_scr, rank_scr, t_scr, cand_scr, sel_scr, z_scr,
                a1_ref, c_ref, b2_ref, e2_ref)


def _mixffn_call(x2d, att, rg, mod3, mod_base, tiles_per_batch, w_out_bf, norm_w, w_qt_bf, sk_bf, tm):
    n = x2d.shape[0]
    tok = lambda t: (t, 0)
    route = pl.BlockSpec((P_HEADS, N_KEYS, tm), lambda t: (0, 0, t))
    route_shape = jax.ShapeDtypeStruct((P_HEADS, N_KEYS, n), F32)
    return pl.pallas_call(
        _mixffn_kernel,
        grid=(n // tm,),
        in_specs=[pl.BlockSpec((tm, D), tok),
                  pl.BlockSpec((tm, D_ATT), tok),
                  pl.BlockSpec((tm, D_RG), tok),
                  pl.BlockSpec((1, 6, D), lambda t: (mod_base + t // tiles_per_batch, 0, 0)),
                  pl.BlockSpec((D, D), lambda t: (0, 0)),
                  pl.BlockSpec((1, D), lambda t: (0, 0)),
                  pl.BlockSpec((2 * P_HEADS * N_KEYS, D), lambda t: (0, 0)),
                  pl.BlockSpec((2, N_KEYS, N_KEYS), lambda t: (0, 0, 0))],
        out_specs=[pl.BlockSpec((tm, D), tok),
                   pl.BlockSpec((D, tm), lambda t: (0, t)),
                   route, route, route, route],
        out_shape=[jax.ShapeDtypeStruct((n, D), F32), jax.ShapeDtypeStruct((D, n), BF16),
                   route_shape, route_shape,
                   jax.ShapeDtypeStruct((P_HEADS, N_KEYS, n), BF16), jax.ShapeDtypeStruct((P_HEADS, N_KEYS, n), BF16)],
        scratch_shapes=[pltpu.VMEM((2 * P_HEADS * N_KEYS, tm), BF16)]
                       + [pltpu.VMEM((N_HP, N_KEYS, tm), F32)] * 3
                       + [pltpu.VMEM((N_HP, TOPK, tm), F32)]
                       + [pltpu.VMEM((P_HEADS, 72, tm), F32)] * 2
                       + [pltpu.VMEM((P_HEADS, 8, tm), F32)],
        compiler_params=_cparams("arbitrary"),
        name="mixffn",
    )(x2d, att, rg, mod3, w_out_bf, norm_w, w_qt_bf, sk_bf)


def _expert_prep_kernel(u_ref, v_ref, ub_ref, vt_ref):
    ub_ref[...] = u_ref[...].astype(BF16)
    vt_ref[...] = v_ref[...].T.astype(BF16)


def _expert_prep_call(u, v):
    te = 512
    return pl.pallas_call(
        _expert_prep_kernel,
        grid=(N_EXPERTS // te,),
        in_specs=[pl.BlockSpec((te, D), lambda j: (j, 0)), pl.BlockSpec((te, D), lambda j: (j, 0))],
        out_specs=[pl.BlockSpec((te, D), lambda j: (j, 0)), pl.BlockSpec((D, te), lambda j: (0, j))],
        out_shape=[jax.ShapeDtypeStruct((N_EXPERTS, D), BF16), jax.ShapeDtypeStruct((D, N_EXPERTS), BF16)],
        compiler_params=_cparams("arbitrary"),
        name="expert_prep",
    )(u, v)


def _peer_kernel(h2t_ref, u_ref, vt_ref, a1_ref, c_ref, b2_ref, e2_ref, x1_ref, mod_ref, nw_ref,
                 y_ref, acc, g_scr, *, i1_per_step):
    j = pl.program_id(1)
    tm = acc.shape[1]

    @pl.when(j == 0)
    def _():
        acc[...] = jnp.zeros_like(acc)

    grp = min(i1_per_step, 8)
    one = jnp.ones((), BF16)
    zero = jnp.zeros((), BF16)
    for i0 in range(0, i1_per_step, grp):
        for l in range(tm // 128):
            ls = slice(l * 128, (l + 1) * 128)
            g = [None] * grp
            for h in range(P_HEADS):
                b2 = b2_ref[h, :, ls]
                e2 = e2_ref[h, :, ls]
                for k in range(grp):
                    a1 = a1_ref[h, i0 + k:i0 + k + 1, ls].astype(BF16)
                    ch = c_ref[h, i0 + k:i0 + k + 1, ls].astype(BF16)
                    m = jnp.minimum(jnp.maximum(a1 - b2, zero), one)
                    w = (ch * e2) * m
                    g[k] = w if g[k] is None else g[k] + w
            for k in range(grp):
                g_scr[(i0 + k) * N_KEYS:(i0 + k + 1) * N_KEYS, ls] = g[k]
    st = jnp.dot(u_ref[...], h2t_ref[...], preferred_element_type=F32)
    act = st * (1.0 + lax.erf(st * 0.7071067811865476))
    at = act.astype(BF16) * g_scr[...]
    acc[...] += jnp.dot(vt_ref[...], at, preferred_element_type=F32)

    @pl.when(j == pl.num_programs(1) - 1)
    def _():
        x2 = x1_ref[...] + mod_ref[0, 5:6, :] * acc[...].T
        y_ref[...] = _rmsnorm(x2, nw_ref[...])


def _peer_call(h2t, u_bf, vt_bf, a1, c, b2, e2, x1, mod3, mod_base, tiles_per_batch, norm_f_w, tm, te):
    n = x1.shape[0]
    ips = te // N_KEYS
    route_lo = pl.BlockSpec((P_HEADS, ips, tm), lambda t, j: (0, j, t))
    route_full = pl.BlockSpec((P_HEADS, N_KEYS, tm), lambda t, j: (0, 0, t))
    return pl.pallas_call(
        functools.partial(_peer_kernel, i1_per_step=ips),
        grid=(n // tm, N_EXPERTS // te),
        in_specs=[pl.BlockSpec((D, tm), lambda t, j: (0, t)),
                  pl.BlockSpec((te, D), lambda t, j: (j, 0)),
                  pl.BlockSpec((D, te), lambda t, j: (0, j)),
                  route_lo, route_lo, route_full, route_full,
                  pl.BlockSpec((tm, D), lambda t, j: (t, 0)),
                  pl.BlockSpec((1, 6, D), lambda t, j: (mod_base + t // tiles_per_batch, 0, 0)),
                  pl.BlockSpec((1, D), lambda t, j: (0, 0))],
        out_specs=pl.BlockSpec((tm, D), lambda t, j: (t, 0)),
        out_shape=jax.ShapeDtypeStruct((n, D), F32),
        scratch_shapes=[pltpu.VMEM((D, tm), F32), pltpu.VMEM((te, tm), BF16)],
        compiler_params=_cparams("arbitrary", "arbitrary"),
        name="peer",
    )(h2t, u_bf, vt_bf, a1, c, b2, e2, x1, mod3, norm_f_w)


def _path(x, mod3, mod_base, per_batch_mod, weights, attn_fn, h0, emit_cache):
    (norm_mix_w, w_in_bf, rg_params, w_out_bf, norm_ffn_w, w_qt_bf, sk_bf, u_bf, vt_bf, norm_f_w) = weights
    bsz, seq, _ = x.shape
    n = bsz * seq
    x2d = x.reshape(n, D)
    tpb = lambda tm: (seq // tm) if per_batch_mod else n
    q, kb, vb, xr, yr, *kv_f32 = _inproj_call(x2d, mod3, mod_base, tpb(TM_INPROJ), norm_mix_w, w_in_bf,
                                              emit_cache)
    att = attn_fn(q, kb, vb)
    rg_out, h_last = _rglru_call(xr, yr, h0, *rg_params, seq)
    tm_mix = 256
    x1, h2t, a1, c, b2, e2 = _mixffn_call(x2d, att, rg_out, mod3, mod_base, tpb(tm_mix), w_out_bf,
                                          norm_ffn_w, w_qt_bf, sk_bf, tm_mix)
    tm_peer = 1024
    y = _peer_call(h2t, u_bf, vt_bf, a1, c, b2, e2, x1, mod3, mod_base, tpb(tm_peer), norm_f_w,
                   tm_peer, 1024)
    return y.reshape(bsz, seq, D), kv_f32, h_last


def kernel(x_prompt, x_sample, c, cache_k, cache_v, state_rglru, c_ctx, w_mod, b_mod, norm_mix_w, w_in,
           rpb, conv_w, conv_b, rg_w_a, rg_b_a, rg_w_i, rg_b_i, rg_lambda, w_out, norm_ffn_w,
           peer_w_q, peer_sub_keys, peer_u, peer_v, norm_f_w):
    depth = w_mod.shape[0]
    assert depth == 1, "single-layer problem"
    l = 0
    bp, sp, _ = x_prompt.shape
    bs, ss, _ = x_sample.shape

    n_rows = 16
    cvec = jnp.concatenate([c_ctx[None, :], c, jnp.zeros((n_rows - 1 - bs, D), F32)], axis=0)
    mod3 = _mod_call(cvec, w_mod[l], b_mod[l][None, :]).reshape(n_rows, 6, D)

    wbd = jnp.stack([_block_diag_halves(rg_w_a[l]), _block_diag_halves(rg_w_i[l])], axis=1).astype(BF16)
    rg_params = (conv_w[l], conv_b[l][None, :], wbd, rg_b_a[l], rg_b_i[l], rg_lambda[l])
    u_bf, vt_bf = _expert_prep_call(peer_u[l], peer_v[l])
    weights = (norm_mix_w[l][None, :], w_in[l].astype(BF16), rg_params, w_out[l].astype(BF16),
               norm_ffn_w[l][None, :], peer_w_q[l].T.astype(BF16), peer_sub_keys[l].astype(BF16),
               u_bf, vt_bf, norm_f_w[None, :])

    ctx_attn = lambda q, kb, vb: _ctx_attn_call(q, kb, vb, sp)
    h0_p = jnp.zeros((bp, 2, D_RG), F32)
    y_prompt, (k_p, v_p), h_last = _path(x_prompt, mod3, 0, False, weights, ctx_attn, h0_p, True)

    tz = _bias_toeplitz(rpb[l])
    ck = cache_k[:, l].reshape(-1, D_ATT).astype(BF16)
    cv = cache_v[:, l].reshape(-1, D_ATT).astype(BF16)
    nbr_attn = lambda q, kb, vb: _nbr_attn_call(q, kb, vb, ck, cv, tz, bs, ss)
    y_sample, _, _ = _path(x_sample, mod3, 1, True, weights, nbr_attn, state_rglru[:, l], False)

    new_k = k_p.reshape(bp, 1, sp, N_HEADS, HEAD_DIM)
    new_v = v_p.reshape(bp, 1, sp, N_HEADS, HEAD_DIM)
    new_h = h_last.reshape(bp, 1, 2, D_RG)
    return (y_prompt, y_sample, new_k, new_v, new_h)
```

```python
import functools

import numpy as np
import jax
import jax.numpy as jnp
from jax import lax
from jax.experimental import pallas as pl
from jax.experimental.pallas import tpu as pltpu

F32 = jnp.float32
BF16 = jnp.bfloat16

D = 1024
D_ATT = 512
N_HEADS = 8
HEAD_DIM = 64
GRID_W = 64
WIN_H = 8
WIN_W = 16
D_RG = 512
D_IN = 3 * D_ATT + 2 * D_RG
N_KEYS = 128
P_HEADS = 8
TOPK = 16
N_EXPERTS = N_KEYS * N_KEYS
RMS_EPS = 1e-6
NEG_INF = -1e30
RG_C = 8.0
ATT_SCALE = HEAD_DIM ** -0.5

VMEM_LIMIT = 56 * 1024 * 1024

_NT = (((1,), (1,)), ((), ()))


def _cparams(*sem, flags=None):
    return pltpu.CompilerParams(dimension_semantics=sem, vmem_limit_bytes=VMEM_LIMIT, flags=flags)


def _rmsnorm(x, w):
    ms = jnp.mean(x * x, axis=-1, keepdims=True)
    return x * lax.rsqrt(ms + RMS_EPS) * w


def _mod_kernel(c_ref, w_ref, b_ref, o_ref):
    cv = c_ref[...]
    s = cv * jax.nn.sigmoid(cv)
    o_ref[...] = jnp.dot(s.astype(BF16), w_ref[...].astype(BF16),
                         preferred_element_type=F32) + b_ref[...]


def _mod_call(cvec, w_mod, b_mod):
    rows = cvec.shape[0]
    tn = 1536
    return pl.pallas_call(
        _mod_kernel,
        grid=(6 * D // tn,),
        in_specs=[pl.BlockSpec((rows, D), lambda j: (0, 0)),
                  pl.BlockSpec((D, tn), lambda j: (0, j)),
                  pl.BlockSpec((1, tn), lambda j: (0, j))],
        out_specs=pl.BlockSpec((rows, tn), lambda j: (0, j)),
        out_shape=jax.ShapeDtypeStruct((rows, 6 * D), F32),
        compiler_params=_cparams("arbitrary"),
        name="mod",
    )(cvec, w_mod, b_mod)


TM_INPROJ = 512


def _inproj_kernel(x_ref, mod_ref, nw_ref, w_ref, *out_refs, emit_f32_kv):
    if emit_f32_kv:
        q_ref, kb_ref, vb_ref, xr_ref, yr_ref, kf_ref, vf_ref = out_refs
    else:
        q_ref, kb_ref, vb_ref, xr_ref, yr_ref = out_refs
    x = x_ref[...]
    h = _rmsnorm(x, nw_ref[...]) * (1.0 + mod_ref[0, 1:2, :]) + mod_ref[0, 0:1, :]
    p = jnp.dot(h.astype(BF16), w_ref[...], preferred_element_type=F32)
    q_ref[...] = (p[:, 0:D_ATT] * ATT_SCALE).astype(BF16)
    k = p[:, D_ATT:2 * D_ATT]
    v = p[:, 2 * D_ATT:3 * D_ATT]
    kb_ref[...] = k.astype(BF16)
    vb_ref[...] = v.astype(BF16)
    xr_ref[...] = p[:, 3 * D_ATT:3 * D_ATT + D_RG]
    yr_ref[...] = p[:, 3 * D_ATT + D_RG:]
    if emit_f32_kv:
        kf_ref[...] = k
        vf_ref[...] = v


def _inproj_call(x2d, mod3, mod_base, tiles_per_batch, norm_w, w_in_bf, emit_f32_kv):
    n = x2d.shape[0]
    tm = TM_INPROJ
    row = lambda t: (mod_base + t // tiles_per_batch, 0, 0)
    tok = lambda t: (t, 0)
    shp = lambda w, dt: jax.ShapeDtypeStruct((n, w), dt)
    out_specs = [pl.BlockSpec((tm, D_ATT), tok)] * 3 + [pl.BlockSpec((tm, D_RG), tok)] * 2
    out_shape = [shp(D_ATT, BF16)] * 3 + [shp(D_RG, F32)] * 2
    if emit_f32_kv:
        out_specs += [pl.BlockSpec((tm, D_ATT), tok)] * 2
        out_shape += [shp(D_ATT, F32)] * 2
    return pl.pallas_call(
        functools.partial(_inproj_kernel, emit_f32_kv=emit_f32_kv),
        grid=(n // tm,),
        in_specs=[pl.BlockSpec((tm, D), tok),
                  pl.BlockSpec((1, 6, D), row),
                  pl.BlockSpec((1, D), lambda t: (0, 0)),
                  pl.BlockSpec((D, D_IN), lambda t: (0, 0))],
        out_specs=out_specs,
        out_shape=out_shape,
        compiler_params=_cparams("arbitrary"),
        name="inproj",
    )(x2d, mod3, norm_w, w_in_bf)


def _ctx_attn_kernel(q_ref, k_ref, v_ref, o_ref):
    for h in range(N_HEADS):
        sl = slice(h * HEAD_DIM, (h + 1) * HEAD_DIM)
        s = lax.dot_general(q_ref[:, sl], k_ref[:, sl], _NT, preferred_element_type=F32)
        m = jnp.max(s, axis=-1, keepdims=True)
        p = jnp.exp(s - m)
        l = jnp.sum(p, axis=-1, keepdims=True)
        o = jnp.dot(p.astype(BF16), v_ref[:, sl], preferred_element_type=F32)
        o_ref[:, sl] = o / l


def _ctx_attn_call(q, kb, vb, seq):
    n = q.shape[0]
    blk = pl.BlockSpec((seq, D_ATT), lambda b: (b, 0))
    return pl.pallas_call(
        _ctx_attn_kernel,
        grid=(n // seq,),
        in_specs=[blk, blk, blk],
        out_specs=blk,
        out_shape=jax.ShapeDtypeStruct((n, D_ATT), F32),
        compiler_params=_cparams("arbitrary"),
        name="ctx_attn",
    )(q, kb, vb)


ROWS_PER_STEP = 8
WIN_UNION = 12


def _nbr_attn_kernel(q_ref, k_ref, v_ref, ck_ref, cv_ref, tz_ref, o_ref, *, rows):
    g = pl.program_id(0)
    n_loc = WIN_UNION * GRID_W
    first = jnp.clip(g * ROWS_PER_STEP - WIN_H // 2, 0, rows - WIN_H)
    kb = jnp.minimum((first >> 1) << 1, rows - WIN_UNION)
    row0 = pl.multiple_of(kb * GRID_W, 2 * GRID_W)
    lane = lax.broadcasted_iota(jnp.int32, (1, 2 * GRID_W), 1)
    d_idx, pen = [], []
    for i in range(ROWS_PER_STEP):
        r = g * ROWS_PER_STEP + i
        start = jnp.clip(r - WIN_H // 2, 0, rows - WIN_H)
        d_i, pen_i = [], []
        for p in range(WIN_UNION // 2):
            kr = kb + 2 * p
            d_i.append(jnp.clip(kr - r + (WIN_H - 1), 0, 2 * WIN_H - 3))
            in0 = (kr >= start) & (kr < start + WIN_H)
            in1 = (kr + 1 >= start) & (kr + 1 < start + WIN_H)
            pen_i.append(jnp.where(lane < GRID_W, jnp.where(in0, 0.0, NEG_INF), jnp.where(in1, 0.0, NEG_INF)))
        d_idx.append(d_i)
        pen.append(pen_i)
    outs = []
    for h in range(N_HEADS):
        sl = slice(h * HEAD_DIM, (h + 1) * HEAD_DIM)
        qg = q_ref[:, sl]
        bias = jnp.concatenate(
            [jnp.concatenate([tz_ref[h, d_idx[i][p]] + pen[i][p] for p in range(WIN_UNION // 2)], axis=-1)
             for i in range(ROWS_PER_STEP)], axis=0)
        s_loc = lax.dot_general(qg, k_ref[pl.ds(row0, n_loc), sl], _NT, preferred_element_type=F32) + bias
        s_ctx = lax.dot_general(qg, ck_ref[:, sl], _NT, preferred_element_type=F32)
        m = jnp.maximum(jnp.max(s_loc, axis=-1, keepdims=True), jnp.max(s_ctx, axis=-1, keepdims=True))
        p_loc = jnp.exp(s_loc - m)
        p_ctx = jnp.exp(s_ctx - m)
        l = jnp.sum(p_loc, axis=-1, keepdims=True) + jnp.sum(p_ctx, axis=-1, keepdims=True)
        o = (jnp.dot(p_loc.astype(BF16), v_ref[pl.ds(row0, n_loc), sl], preferred_element_type=F32)
             + jnp.dot(p_ctx.astype(BF16), cv_ref[:, sl], preferred_element_type=F32))
        outs.append(o / l)
    o_ref[...] = jnp.concatenate(outs, axis=-1)


def _nbr_attn_call(q, kb, vb, ck, cv, tz, batch, seq):
    rows = seq // GRID_W
    assert rows % 2 == 0 and rows >= WIN_UNION and rows % ROWS_PER_STEP == 0
    for g in range(rows // ROWS_PER_STEP):
        lo = min(max(g * ROWS_PER_STEP - WIN_H // 2, 0), rows - WIN_H)
        hi = min(max((g + 1) * ROWS_PER_STEP - 1 - WIN_H // 2, 0), rows - WIN_H) + WIN_H
        first_key_row = min(lo // 2 * 2, rows - WIN_UNION)
        assert first_key_row <= lo and hi <= first_key_row + WIN_UNION, (g, lo, hi)
    groups = rows // ROWS_PER_STEP
    past = ck.shape[0] // batch
    n = q.shape[0]
    qblk = pl.BlockSpec((GRID_W * ROWS_PER_STEP, D_ATT), lambda g, b: (b * groups + g, 0))
    kvblk = pl.BlockSpec((seq, D_ATT), lambda g, b: (b, 0))
    cblk = pl.BlockSpec((past, D_ATT), lambda g, b: (b, 0))
    return pl.pallas_call(
        functools.partial(_nbr_attn_kernel, rows=rows),
        grid=(groups, batch),
        in_specs=[qblk, kvblk, kvblk, cblk, cblk,
                  pl.BlockSpec(tz.shape, lambda g, b: (0, 0, 0, 0))],
        out_specs=qblk,
        out_shape=jax.ShapeDtypeStruct((n, D_ATT), F32),
        compiler_params=_cparams("arbitrary", "arbitrary"),
        name="nbr_attn",
    )(q, kb, vb, ck, cv, tz)


def _bias_toeplitz(rpb_l):
    qc = np.arange(GRID_W)
    off = qc[None, :] - qc[:, None] + (WIN_W - 1)
    place = (off[None, :, :] == np.arange(2 * WIN_W - 1)[:, None, None]).astype(np.float32)
    tz = jnp.einsum('hdx,xqk->hdqk', rpb_l, jnp.asarray(place), precision=lax.Precision.HIGHEST)
    cs = np.clip(qc - WIN_W // 2, 0, GRID_W - WIN_W)
    valid = (qc[None, :] >= cs[:, None]) & (qc[None, :] < cs[:, None] + WIN_W)
    tz = jnp.where(jnp.asarray(valid)[None, None], tz, NEG_INF)
    return jnp.concatenate([tz[:, :-1], tz[:, 1:]], axis=-1)


def _log_sigmoid(x):
    return jnp.minimum(x, 0.0) - jnp.log(1.0 + jnp.exp(-jnp.abs(x)))


def _gelu_tanh(x):
    return 0.5 * x * (1.0 + jnp.tanh(0.7978845608028654 * (x + 0.044715 * x * x * x)))


def _rglru_kernel(xr_ref, yr_ref, h0_ref, cw_ref, cb_ref, wbd_ref, ba_ref, bi_ref, lam_ref,
                  y_ref, hl_ref, xpad, a_f, b_f, a_b, b_b):
    T = xr_ref.shape[0]
    CH = 256
    xpad[0:8, :] = jnp.zeros((8, D_RG), F32)
    xpad[T + 8:T + 16, :] = jnp.zeros((8, D_RG), F32)
    xpad[8:T + 8, :] = xr_ref[...]
    a_refs = (a_f, a_b)
    b_refs = (b_f, b_b)
    for c0 in range(0, T, CH):
        xc = cb_ref[...] + cw_ref[0:1, :] * xpad[c0 + 6:c0 + 6 + CH, :]
        for i in range(1, 4):
            xc = xc + cw_ref[i:i + 1, :] * xpad[c0 + 6 + i:c0 + 6 + i + CH, :]
        xcb = xc.astype(BF16)
        for d in range(2):
            gates = []
            for g in range(2):
                halves = [jnp.dot(xcb[:, hf * 256:(hf + 1) * 256], wbd_ref[d, g, hf],
                                  preferred_element_type=F32) for hf in range(2)]
                gates.append(jnp.concatenate(halves, axis=-1))
            rg = jax.nn.sigmoid(gates[0] + ba_ref[d:d + 1, :])
            ig = jax.nn.sigmoid(gates[1] + bi_ref[d:d + 1, :])
            log_a = RG_C * rg * _log_sigmoid(lam_ref[d:d + 1, :])
            a = jnp.exp(log_a)
            b = jnp.sqrt(1.0 - a * a) * (ig * xc)
            a_refs[d][c0:c0 + CH, :] = a
            b_refs[d][c0:c0 + CH, :] = b

    sub = lax.broadcasted_iota(jnp.int32, (8, D_RG), 0)
    n_tiles = T // 8

    def scan_tile(a_ref, b_ref, t8, h, order):
        base = pl.multiple_of(t8 * 8, 8)
        a_blk = a_ref[pl.ds(base, 8), :]
        b_blk = b_ref[pl.ds(base, 8), :]
        out = b_blk
        for j in order:
            cand = a_blk * h + b_blk
            h = jnp.broadcast_to(cand[j:j + 1, :], (8, D_RG))
            out = jnp.where(sub == j, cand, out)
        b_ref[pl.ds(base, 8), :] = out
        return h

    def body(i, carry):
        hf, hb = carry
        hf = scan_tile(a_f, b_f, i, hf, range(8))
        hb = scan_tile(a_b, b_b, n_tiles - 1 - i, hb, range(7, -1, -1))
        return hf, hb

    hf0 = jnp.broadcast_to(h0_ref[0, 0:1, :], (8, D_RG))
    hb0 = jnp.broadcast_to(h0_ref[0, 1:2, :], (8, D_RG))
    hf, hb = lax.fori_loop(0, n_tiles, body, (hf0, hb0))
    hl_ref[0, 0:1, :] = hf[0:1, :]
    hl_ref[0, 1:2, :] = hb[0:1, :]
    for c0 in range(0, T, CH):
        sl = slice(c0, c0 + CH)
        y_ref[sl, :] = (b_f[sl, :] + b_b[sl, :]) * _gelu_tanh(yr_ref[sl, :])


def _rglru_call(xr, yr, h0, conv_w, conv_b, wbd, b_a, b_i, lam, seq):
    n = xr.shape[0]
    batch = n // seq
    tok = pl.BlockSpec((seq, D_RG), lambda b: (b, 0))
    full = lambda shape: pl.BlockSpec(shape, lambda b: (0,) * len(shape))
    return pl.pallas_call(
        _rglru_kernel,
        grid=(batch,),
        in_specs=[tok, tok, pl.BlockSpec((1, 2, D_RG), lambda b: (b, 0, 0)),
                  full((4, D_RG)), full((1, D_RG)), full((2, 2, 2, 256, 256)),
                  full((2, D_RG)), full((2, D_RG)), full((2, D_RG))],
        out_specs=[tok, pl.BlockSpec((1, 2, D_RG), lambda b: (b, 0, 0))],
        out_shape=[jax.ShapeDtypeStruct((n, D_RG), F32), jax.ShapeDtypeStruct((batch, 2, D_RG), F32)],
        scratch_shapes=[pltpu.VMEM((seq + 16, D_RG), F32)] + [pltpu.VMEM((seq, D_RG), F32)] * 4,
        compiler_params=_cparams("arbitrary"),
        name="rglru",
    )(xr, yr, h0, conv_w, conv_b, wbd, b_a, b_i, lam)


def _block_diag_halves(w):
    d = w.shape[0]
    w4 = w.reshape(d, 2, 4, 64, 64)
    eye = jnp.eye(4, dtype=w.dtype)
    out = jnp.einsum('dhncf,nm->dhncmf', w4, eye)
    return out.reshape(d, 2, 256, 256)


N_HP = 2 * P_HEADS


def _candidate_flat(tm):
    subf = lax.broadcasted_iota(jnp.int32, (8, tm), 0).astype(F32)
    flat = [subf + float(r1 * TOPK) for r1 in range(4)]
    flat.append(subf + 8.0)
    flat += [subf * float(TOPK) + float(r2) for r2 in range(3)]
    flat.append((subf + 8.0) * float(TOPK))
    return jnp.concatenate(flat, axis=0)


def _candidate_sums(t1, t2):
    tm = t1.shape[1]
    sub = lax.broadcasted_iota(jnp.int32, (8, tm), 0)
    ninf = -jnp.inf
    lim_a = (8, 8, 5, 4)
    vals = [jnp.where(sub < lim_a[r1], t1[r1:r1 + 1, :] + t2[0:8, :], ninf) for r1 in range(4)]
    vals.append(t1[0:1, :] + t2[8:16, :])
    for r2 in range(3):
        ok = (sub >= 4) if r2 < 2 else (sub == 4)
        vals.append(jnp.where(ok, t1[0:8, :] + t2[r2:r2 + 1, :], ninf))
    vals.append(t1[8:16, :] + t2[0:1, :])
    return jnp.concatenate(vals, axis=0)


def _staircase_rows(sel):
    rows = []
    for r1 in range(4):
        l = jnp.sum(sel[8 * r1:8 * r1 + 8, :], axis=0, keepdims=True)
        if r1 == 0:
            l = l + jnp.sum(sel[32:40, :], axis=0, keepdims=True)
        rows.append(l)
    l_mid = sel[40:48, :] + sel[48:56, :] + sel[56:64, :]
    l_hi = sel[64:72, :]
    rows += [l_mid[r1:r1 + 1, :] for r1 in range(4, 8)]
    rows += [l_hi[r1 - 8:r1 - 7, :] for r1 in range(8, 16)]
    return rows


def _batcher_pairs(n):
    pairs = []

    def merge(lo, cnt, r):
        step = r * 2
        if step < cnt:
            merge(lo, cnt, step)
            merge(lo + r, cnt, step)
            for i in range(lo + r, lo + cnt - r, step):
                pairs.append((i, i + r))
        else:
            pairs.append((lo, lo + r))

    def sort(lo, cnt):
        if cnt > 1:
            m = cnt // 2
            sort(lo, m)
            sort(lo + m, m)
            merge(lo, cnt, 1)

    sort(0, n)
    return pairs


_SORT16 = _batcher_pairs(TOPK)


def _top16_network(s):
    v = list(s)
    for i, j in _SORT16:
        v[i], v[j] = jnp.maximum(v[i], v[j]), jnp.minimum(v[i], v[j])
    for shift in (4, 2, 1):
        rolled = [pltpu.roll(x, shift, 0) for x in v]
        c = [jnp.maximum(v[k], rolled[TOPK - 1 - k]) for k in range(TOPK)]
        for d in (8, 4, 2, 1):
            for k in range(TOPK):
                if k & d == 0:
                    c[k], c[k + d] = jnp.maximum(c[k], c[k + d]), jnp.minimum(c[k], c[k + d])
        v = c
    return v


def _route_tile(sk_ref, qt_scr, s_scr, w_scr, rank_scr, t_scr, cand_scr, sel_scr, z_scr,
                a1_ref, c_ref, b2_ref, e2_ref):
    tm = qt_scr.shape[1]
    iota_f = lax.broadcasted_iota(jnp.int32, (N_KEYS, tm), 0).astype(F32)
    flat = _candidate_flat(tm)
    ninf = -jnp.inf
    for hp in range(N_HP):
        s_scr[hp] = jnp.dot(sk_ref[hp % 2], qt_scr[hp * N_KEYS:(hp + 1) * N_KEYS, :],
                            preferred_element_type=F32)

    def stage1_extract():
        for hp in range(N_HP):
            w_scr[hp] = s_scr[hp]
            rank_scr[hp] = jnp.full((N_KEYS, tm), float(TOPK), F32)

        def round1(r, carry):
            rf = jnp.asarray(r, jnp.int32).astype(F32)
            for hp in range(N_HP):
                w = w_scr[hp]
                m = jnp.max(w, axis=0, keepdims=True)
                idx = jnp.min(jnp.where(w == m, iota_f, float(N_KEYS)), axis=0, keepdims=True)
                hit = iota_f == idx
                rank_scr[hp] = jnp.where(hit, rf, rank_scr[hp])
                w_scr[hp] = jnp.where(hit, ninf, w)
                t_scr[hp, pl.ds(r, 1), :] = m
            return carry

        lax.fori_loop(0, TOPK, round1, 0)

    def stage1_network():
        lane_blocks = tm // 128
        sub = lax.broadcasted_iota(jnp.int32, (8, 128), 0)

        def body(h, flags):
            out = list(flags)
            for p in range(2):
                hp = 2 * h + p
                for lb in range(lane_blocks):
                    ls = slice(lb * 128, (lb + 1) * 128)
                    s = [s_scr[hp, 8 * v:8 * v + 8, ls] for v in range(N_KEYS // 8)]
                    t = _top16_network(s)
                    lo, hi = t[0], t[8]
                    for r in range(1, 8):
                        lo = jnp.where(sub == r, t[r], lo)
                        hi = jnp.where(sub == r, t[8 + r], hi)
                    t_scr[hp, 0:8, ls] = lo
                    t_scr[hp, 8:16, ls] = hi
                    if p == 1:
                        for v in range(N_KEYS // 8):
                            rank = jnp.zeros((8, 128), F32)
                            for r in range(TOPK):
                                rank = jnp.where(t[r] > s[v], float(r + 1), rank)
                            rank_scr[hp, 8 * v:8 * v + 8, ls] = rank
                    tie = out[lb]
                    for r in range(TOPK - 1):
                        tie = jnp.where(t[r] == t[r + 1], 1.0, tie)
                    out[lb] = tie
            return tuple(out)

        flags = lax.fori_loop(0, P_HEADS, body, (jnp.zeros((8, 128), F32),) * lane_blocks)
        return jnp.concatenate([f[0:1, :] for f in flags], axis=1)

    def stage2_extract():
        for h in range(P_HEADS):
            cand_scr[h] = _candidate_sums(t_scr[2 * h], t_scr[2 * h + 1])
            sel_scr[h] = jnp.zeros((72, tm), F32)

        def round2(r, carry):
            zs, b0s = carry
            zs_new, b0s_new = [], []
            for h in range(P_HEADS):
                w = cand_scr[h]
                m = jnp.max(w, axis=0, keepdims=True)
                f = jnp.min(jnp.where(w == m, flat, 1e9), axis=0, keepdims=True)
                hit = flat == f
                sel_scr[h] = jnp.where(hit, 1.0, sel_scr[h])
                cand_scr[h] = jnp.where(hit, ninf, w)
                b0 = jnp.where(r == 0, m, b0s[h])
                zs_new.append(zs[h] + jnp.exp(m - b0))
                b0s_new.append(b0)
            return tuple(zs_new), tuple(b0s_new)

        zero_row = jnp.zeros((1, tm), F32)
        zs, _ = lax.fori_loop(0, TOPK, round2, ((zero_row,) * P_HEADS, (zero_row,) * P_HEADS))
        for h in range(P_HEADS):
            z_scr[h] = jnp.broadcast_to(zs[h], (8, tm))

    def stage2_network():
        pad = jnp.full((8, 128), ninf, F32)

        def body(h, carry):
            for lb in range(tm // 128):
                ls = slice(lb * 128, (lb + 1) * 128)
                cand = _candidate_sums(t_scr[2 * h, :, ls], t_scr[2 * h + 1, :, ls])
                c = [cand[8 * v:8 * v + 8, :] for v in range(9)]
                top = _top16_network(c + [pad] * (TOPK - 9))
                for v in range(9):
                    sel_scr[h, 8 * v:8 * v + 8, ls] = jnp.where(c[v] >= top[TOPK - 1], 1.0, 0.0)
                z = jnp.ones((8, 128), F32)
                for r in range(1, TOPK):
                    z = z + jnp.exp(top[r] - top[0])
                z_scr[h, :, ls] = z
            return carry

        lax.fori_loop(0, P_HEADS, body, 0)

    excess = stage1_network()
    stage2_network()
    for hp in range(N_HP):
        if hp % 2 == 0:
            marks = jnp.where(s_scr[hp] >= t_scr[hp, TOPK - 1:TOPK, :], 1.0, 0.0)
        else:
            marks = jnp.where(rank_scr[hp] < float(TOPK), 1.0, 0.0)
        cnt = jnp.sum(marks, axis=0, keepdims=True)
        excess = jnp.maximum(excess, jnp.abs(cnt - float(TOPK)))
    for h in range(P_HEADS):
        cnt = jnp.sum(sel_scr[h], axis=0, keepdims=True)
        excess = jnp.maximum(excess, jnp.abs(cnt - float(TOPK)))
    tied = jnp.max(excess) > 0.0

    @pl.when(tied)
    def _():
        stage1_extract()
        stage2_extract()
        for h in range(P_HEADS):
            l_rows = _staircase_rows(sel_scr[h])
            rank1 = rank_scr[2 * h]
            a1 = jnp.zeros((N_KEYS, tm), F32)
            for r1 in range(TOPK):
                a1 = jnp.where(rank1 == float(r1), l_rows[r1], a1)
            a1_ref[h] = a1

    @pl.when(jnp.logical_not(tied))
    def finish_from_values():
        for h in range(P_HEADS):
            l_rows = _staircase_rows(sel_scr[h])
            s1 = s_scr[2 * h]
            a1 = jnp.zeros((N_KEYS, tm), F32)
            for r1 in range(TOPK - 1, -1, -1):
                a1 = jnp.where(s1 >= t_scr[2 * h, r1:r1 + 1, :], l_rows[r1], a1)
            a1_ref[h] = a1

    for h in range(P_HEADS):
        c_ref[h] = jnp.exp(s_scr[2 * h] - t_scr[2 * h, 0:1, :]) * (0.5 / z_scr[h, 0:1, :])
        b2_ref[h] = rank_scr[2 * h + 1].astype(BF16)
        e2_ref[h] = jnp.exp(s_scr[2 * h + 1] - t_scr[2 * h + 1, 0:1, :]).astype(BF16)


def _mixffn_kernel(x_ref, att_ref, rg_ref, mod_ref, wo_ref, nw_ref, wqt_ref, sk_ref,
                   x1_ref, h2t_ref, a1_ref, c_ref, b2_ref, e2_ref,
                   qt_scr, s_scr, w_scr, rank_scr, t_scr, cand_scr, sel_scr, z_scr):
    o = (jnp.dot(att_ref[...].astype(BF16), wo_ref[0:D_ATT, :], preferred_element_type=F32)
         + jnp.dot(rg_ref[...].astype(BF16), wo_ref[D_ATT:, :], preferred_element_type=F32))
    x1 = x_ref[...] + mod_ref[0, 2:3, :] * o
    x1_ref[...] = x1
    h2 = _rmsnorm(x1, nw_ref[...]) * (1.0 + mod_ref[0, 4:5, :]) + mod_ref[0, 3:4, :]
    h2t = h2.T.astype(BF16)
    h2t_ref[...] = h2t
    qt_scr[...] = jnp.dot(wqt_ref[...], h2t, preferred_element_type=F32).astype(BF16)
    _route_tile(sk_ref, qt_scr, s_scr, w_scr, rank_scr, t_scr, cand_scr, sel_scr, z_scr,
                a1_ref, c_ref, b2_ref, e2_ref)


def _mixffn_call(x2d, att, rg, mod3, mod_base, tiles_per_batch, w_out_bf, norm_w, w_qt_bf, sk_bf, tm):
    n = x2d.shape[0]
    tok = lambda t: (t, 0)
    route = pl.BlockSpec((P_HEADS, N_KEYS, tm), lambda t: (0, 0, t))
    route_shape = jax.ShapeDtypeStruct((P_HEADS, N_KEYS, n), F32)
    return pl.pallas_call(
        _mixffn_kernel,
        grid=(n // tm,),
        in_specs=[pl.BlockSpec((tm, D), tok),
                  pl.BlockSpec((tm, D_ATT), tok),
                  pl.BlockSpec((tm, D_RG), tok),
                  pl.BlockSpec((1, 6, D), lambda t: (mod_base + t // tiles_per_batch, 0, 0)),
                  pl.BlockSpec((D, D), lambda t: (0, 0)),
                  pl.BlockSpec((1, D), lambda t: (0, 0)),
                  pl.BlockSpec((2 * P_HEADS * N_KEYS, D), lambda t: (0, 0)),
                  pl.BlockSpec((2, N_KEYS, N_KEYS), lambda t: (0, 0, 0))],
        out_specs=[pl.BlockSpec((tm, D), tok),
                   pl.BlockSpec((D, tm), lambda t: (0, t)),
                   route, route, route, route],
        out_shape=[jax.ShapeDtypeStruct((n, D), F32), jax.ShapeDtypeStruct((D, n), BF16),
                   route_shape, route_shape,
                   jax.ShapeDtypeStruct((P_HEADS, N_KEYS, n), BF16), jax.ShapeDtypeStruct((P_HEADS, N_KEYS, n), BF16)],
        scratch_shapes=[pltpu.VMEM((2 * P_HEADS * N_KEYS, tm), BF16)]
                       + [pltpu.VMEM((N_HP, N_KEYS, tm), F32)] * 3
                       + [pltpu.VMEM((N_HP, TOPK, tm), F32)]
                       + [pltpu.VMEM((P_HEADS, 72, tm), F32)] * 2
                       + [pltpu.VMEM((P_HEADS, 8, tm), F32)],
        compiler_params=_cparams("arbitrary"),
        name="mixffn",
    )(x2d, att, rg, mod3, w_out_bf, norm_w, w_qt_bf, sk_bf)


def _expert_prep_kernel(u_ref, v_ref, ub_ref, vt_ref):
    ub_ref[...] = u_ref[...].astype(BF16)
    vt_ref[...] = v_ref[...].T.astype(BF16)


def _expert_prep_call(u, v):
    te = 512
    return pl.pallas_call(
        _expert_prep_kernel,
        grid=(N_EXPERTS // te,),
        in_specs=[pl.BlockSpec((te, D), lambda j: (j, 0)), pl.BlockSpec((te, D), lambda j: (j, 0))],
        out_specs=[pl.BlockSpec((te, D), lambda j: (j, 0)), pl.BlockSpec((D, te), lambda j: (0, j))],
        out_shape=[jax.ShapeDtypeStruct((N_EXPERTS, D), BF16), jax.ShapeDtypeStruct((D, N_EXPERTS), BF16)],
        compiler_params=_cparams("arbitrary"),
        name="expert_prep",
    )(u, v)


def _peer_kernel(h2t_ref, u_ref, vt_ref, a1_ref, c_ref, b2_ref, e2_ref, x1_ref, mod_ref, nw_ref,
                 y_ref, acc, g_scr, *, i1_per_step):
    j = pl.program_id(1)
    tm = acc.shape[1]

    @pl.when(j == 0)
    def _():
        acc[...] = jnp.zeros_like(acc)

    grp = min(i1_per_step, 8)
    one = jnp.ones((), BF16)
    zero = jnp.zeros((), BF16)
    for i0 in range(0, i1_per_step, grp):
        for l, half in [(l, half) for l in range(tm // 128) for half in range(2)]:
            ls = slice(l * 128, (l + 1) * 128)
            i2s = slice(half * (N_KEYS // 2), (half + 1) * (N_KEYS // 2))
            g = [None] * grp
            for h in range(P_HEADS):
                b2 = b2_ref[h, i2s, ls]
                e2 = e2_ref[h, i2s, ls]
                for k in range(grp):
                    a1 = a1_ref[h, i0 + k:i0 + k + 1, ls].astype(BF16)
                    ch = c_ref[h, i0 + k:i0 + k + 1, ls].astype(BF16)
                    m = jnp.minimum(jnp.maximum(a1 - b2, zero), one)
                    w = (ch * e2) * m
                    g[k] = w if g[k] is None else g[k] + w
            for k in range(grp):
                r0 = (i0 + k) * N_KEYS + half * (N_KEYS // 2)
                g_scr[r0:r0 + N_KEYS // 2, ls] = g[k]
    st = jnp.dot(u_ref[...], h2t_ref[...], preferred_element_type=F32)
    act = st * (1.0 + lax.erf(st * 0.7071067811865476))
    at = act.astype(BF16) * g_scr[...]
    acc[...] += jnp.dot(vt_ref[...], at, preferred_element_type=F32)

    @pl.when(j == pl.num_programs(1) - 1)
    def _():
        x2 = x1_ref[...] + mod_ref[0, 5:6, :] * acc[...].T
        y_ref[...] = _rmsnorm(x2, nw_ref[...])


def _peer_call(h2t, u_bf, vt_bf, a1, c, b2, e2, x1, mod3, mod_base, tiles_per_batch, norm_f_w, tm, te):
    n = x1.shape[0]
    ips = te // N_KEYS
    route_lo = pl.BlockSpec((P_HEADS, ips, tm), lambda t, j: (0, j, t))
    route_full = pl.BlockSpec((P_HEADS, N_KEYS, tm), lambda t, j: (0, 0, t))
    return pl.pallas_call(
        functools.partial(_peer_kernel, i1_per_step=ips),
        grid=(n // tm, N_EXPERTS // te),
        in_specs=[pl.BlockSpec((D, tm), lambda t, j: (0, t)),
                  pl.BlockSpec((te, D), lambda t, j: (j, 0)),
                  pl.BlockSpec((D, te), lambda t, j: (0, j)),
                  route_lo, route_lo, route_full, route_full,
                  pl.BlockSpec((tm, D), lambda t, j: (t, 0)),
                  pl.BlockSpec((1, 6, D), lambda t, j: (mod_base + t // tiles_per_batch, 0, 0)),
                  pl.BlockSpec((1, D), lambda t, j: (0, 0))],
        out_specs=pl.BlockSpec((tm, D), lambda t, j: (t, 0)),
        out_shape=jax.ShapeDtypeStruct((n, D), F32),
        scratch_shapes=[pltpu.VMEM((D, tm), F32), pltpu.VMEM((te, tm), BF16)],
        compiler_params=_cparams("arbitrary", "arbitrary"),
        name="peer",
    )(h2t, u_bf, vt_bf, a1, c, b2, e2, x1, mod3, norm_f_w)


def _path(x, mod3, mod_base, per_batch_mod, weights, attn_fn, h0, emit_cache):
    (norm_mix_w, w_in_bf, rg_params, w_out_bf, norm_ffn_w, w_qt_bf, sk_bf, u_bf, vt_bf, norm_f_w) = weights
    bsz, seq, _ = x.shape
    n = bsz * seq
    x2d = x.reshape(n, D)
    tpb = lambda tm: (seq // tm) if per_batch_mod else n
    q, kb, vb, xr, yr, *kv_f32 = _inproj_call(x2d, mod3, mod_base, tpb(TM_INPROJ), norm_mix_w, w_in_bf,
                                              emit_cache)
    att = attn_fn(q, kb, vb)
    rg_out, h_last = _rglru_call(xr, yr, h0, *rg_params, seq)
    tm_mix = 256
    x1, h2t, a1, c, b2, e2 = _mixffn_call(x2d, att, rg_out, mod3, mod_base, tpb(tm_mix), w_out_bf,
                                          norm_ffn_w, w_qt_bf, sk_bf, tm_mix)
    tm_peer = 1024
    y = _peer_call(h2t, u_bf, vt_bf, a1, c, b2, e2, x1, mod3, mod_base, tpb(tm_peer), norm_f_w,
                   tm_peer, 1024)
    return y.reshape(bsz, seq, D), kv_f32, h_last


def kernel(x_prompt, x_sample, c, cache_k, cache_v, state_rglru, c_ctx, w_mod, b_mod, norm_mix_w, w_in,
           rpb, conv_w, conv_b, rg_w_a, rg_b_a, rg_w_i, rg_b_i, rg_lambda, w_out, norm_ffn_w,
           peer_w_q, peer_sub_keys, peer_u, peer_v, norm_f_w):
    depth = w_mod.shape[0]
    assert depth == 1, "single-layer problem"
    l = 0
    bp, sp, _ = x_prompt.shape
    bs, ss, _ = x_sample.shape

    n_rows = 16
    cvec = jnp.concatenate([c_ctx[None, :], c, jnp.zeros((n_rows - 1 - bs, D), F32)], axis=0)
    mod3 = _mod_call(cvec, w_mod[l], b_mod[l][None, :]).reshape(n_rows, 6, D)

    wbd = jnp.stack([_block_diag_halves(rg_w_a[l]), _block_diag_halves(rg_w_i[l])], axis=1).astype(BF16)
    rg_params = (conv_w[l], conv_b[l][None, :], wbd, rg_b_a[l], rg_b_i[l], rg_lambda[l])
    u_bf, vt_bf = _expert_prep_call(peer_u[l], peer_v[l])
    weights = (norm_mix_w[l][None, :], w_in[l].astype(BF16), rg_params, w_out[l].astype(BF16),
               norm_ffn_w[l][None, :], peer_w_q[l].T.astype(BF16), peer_sub_keys[l].astype(BF16),
               u_bf, vt_bf, norm_f_w[None, :])

    ctx_attn = lambda q, kb, vb: _ctx_attn_call(q, kb, vb, sp)
    h0_p = jnp.zeros((bp, 2, D_RG), F32)
    y_prompt, (k_p, v_p), h_last = _path(x_prompt, mod3, 0, False, weights, ctx_attn, h0_p, True)

    tz = _bias_toeplitz(rpb[l])
    ck = cache_k[:, l].reshape(-1, D_ATT).astype(BF16)
    cv = cache_v[:, l].reshape(-1, D_ATT).astype(BF16)
    nbr_attn = lambda q, kb, vb: _nbr_attn_call(q, kb, vb, ck, cv, tz, bs, ss)
    y_sample, _, _ = _path(x_sample, mod3, 1, True, weights, nbr_attn, state_rglru[:, l], False)

    new_k = k_p.reshape(bp, 1, sp, N_HEADS, HEAD_DIM)
    new_v = v_p.reshape(bp, 1, sp, N_HEADS, HEAD_DIM)
    new_h = h_last.reshape(bp, 1, 2, D_RG)
    return (y_prompt, y_sample, new_k, new_v, new_h)
```
